```python
import jax
import jax.numpy as jnp
from jax import lax
import numpy as np

D_MODEL = 1024
BATCH = 8
SEQ = 2048
DEPTH = 1
DEC_BATCH = 128
DEC_SEQ = 4
PAST_LEN = 16384
PAGE_SIZE = 128

HEAD_DIM = 64
A_Q_HEADS = 8
A_KV_HEADS = 2
A_GROUP = A_Q_HEADS // A_KV_HEADS
A_WINDOW = 128
B_DIL = ((128, 1), (512, 4), (2048, 16))
B_SLOTS = 4
B_HEADS = B_SLOTS * len(B_DIL)
BLK = 128
ROPE_THETA = 10000.0
D_FF = ((8 * D_MODEL // 3 + 127) // 128) * 128
CONV_W = 3
ALPHA = (2 * DEPTH) ** 0.25
BETA = (8 * DEPTH) ** -0.25
LN_EPS = 1e-5
NEG = -1e30
SCALE = HEAD_DIM ** -0.5
A_Q = A_Q_HEADS * HEAD_DIM
A_KV = A_KV_HEADS * HEAD_DIM
B_W = B_HEADS * HEAD_DIM
N_PROJ = A_Q + 2 * A_KV + 3 * B_W + 2 * D_MODEL
SPLIT_POINTS = (A_Q, A_Q + A_KV, A_Q + 2 * A_KV, A_Q + 2 * A_KV + B_W,
                A_Q + 2 * A_KV + 2 * B_W, A_Q + 2 * A_KV + 3 * B_W,
                A_Q + 2 * A_KV + 3 * B_W + D_MODEL)

kernel_name = 'hybrid_swa_sink_dilated_convffn_step'


def _layernorm(x, g, b):
    xf = x.astype(jnp.float32)
    mu = jnp.mean(xf, axis=-1, keepdims=True)
    var = jnp.mean(jnp.square(xf - mu), axis=-1, keepdims=True)
    return ((xf - mu) * lax.rsqrt(var + LN_EPS) * g + b).astype(x.dtype)


def _rope(x, pos):
    half = HEAD_DIM // 2
    inv = ROPE_THETA ** (-jnp.arange(half, dtype=jnp.float32) / half)
    ang = pos.astype(jnp.float32)[:, None] * inv[None, :]
    cos = jnp.cos(ang)[:, None, :]
    sin = jnp.sin(ang)[:, None, :]
    xf = x.astype(jnp.float32)
    x1, x2 = xf[..., :half], xf[..., half:]
    return jnp.concatenate([x1 * cos - x2 * sin, x2 * cos + x1 * sin], axis=-1).astype(x.dtype)


def _project(x, w_in):
    n, t, _ = x.shape
    qa, ka, va, qb, kb, vb, ga, gb = jnp.split(x @ w_in, SPLIT_POINTS, axis=-1)
    heads = lambda z: z.reshape(n, t, -1, HEAD_DIM)
    return heads(qa), heads(ka), heads(va), heads(qb), heads(kb), heads(vb), ga, gb


def _sink_logits(sink_a):
    return sink_a.astype(jnp.float32).reshape(A_KV_HEADS, A_GROUP, 1, 1)


def _sink_softmax(s, sink):
    m = jnp.maximum(jnp.max(s, axis=-1, keepdims=True), sink)
    e = jnp.exp(s - m)
    return e / (jnp.sum(e, axis=-1, keepdims=True) + jnp.exp(sink - m))


def _softmax_lse(s):
    m = jnp.max(s, axis=-1, keepdims=True)
    e = jnp.exp(s - m)
    den = jnp.sum(e, axis=-1, keepdims=True)
    return e / den, (m + jnp.log(den))[..., 0]


def _blocks(x, nb):
    pad = [(0, 0), (0, nb * BLK - x.shape[1])] + [(0, 0)] * (x.ndim - 2)
    x = jnp.pad(x, pad)
    return x.reshape(x.shape[0], nb, BLK, *x.shape[2:])


def _with_prev(xb):
    prev = jnp.pad(xb[:, :-1], [(0, 0), (1, 0)] + [(0, 0)] * (xb.ndim - 2))
    return jnp.concatenate([prev, xb], axis=2)


def _band_scores(q, k, window):
    n, t = q.shape[:2]
    nb = -(-t // BLK)
    qb = _blocks(q, nb)
    kc = _with_prev(_blocks(k, nb))
    s = jnp.einsum('nbqkgd,nbskd->nbkgqs', qb, kc, preferred_element_type=jnp.float32) * SCALE
    blk = jnp.arange(nb)[:, None, None]
    qi = jnp.arange(BLK)[None, :, None] + BLK
    si = jnp.arange(2 * BLK)[None, None, :]
    dist = qi - si
    mask = (dist >= 0) & (dist < window) & ((blk > 0) | (si >= BLK))
    return jnp.where(mask[None, :, None, None], s, NEG)


def _band_values(p, v):
    n, t = v.shape[:2]
    nb = p.shape[1]
    vc = _with_prev(_blocks(v, nb))
    o = jnp.einsum('nbkgqs,nbskd->nbqkgd', p.astype(v.dtype), vc)
    return o.reshape(n, nb * BLK, *o.shape[3:])[:, :t]


def _band_rows(r, t):
    n, nb = r.shape[:2]
    return r.transpose(0, 1, 4, 2, 3).reshape(n, nb * BLK, *r.shape[2:4])[:, :t]


def _stride_split(x, d):
    n, t = x.shape[:2]
    x = x.reshape(n, t // d, d, *x.shape[2:]).swapaxes(1, 2)
    return x.reshape(n * d, t // d, *x.shape[3:])


def _stride_merge(x, n, d):
    x = x.reshape(n, d, *x.shape[1:]).swapaxes(1, 2)
    return x.reshape(n, d * x.shape[1], *x.shape[3:])


def _combine_groups(outs, lses):
    o = jnp.stack(outs)
    wts = jax.nn.softmax(jnp.stack(lses), axis=0)
    ob = jnp.sum(wts[..., None] * o.astype(jnp.float32), axis=0).astype(o.dtype)
    return ob.reshape(ob.shape[0], ob.shape[1], B_SLOTS * HEAD_DIM)


def _merge(oa, ob, ga, gb, w_br_a, w_br_b, w_o):
    m = jax.nn.sigmoid(ga) * (oa @ w_br_a) + jax.nn.sigmoid(gb) * (ob @ w_br_b)
    return m @ w_o


def _prompt_mixer(x, w_in, sink_a, w_br_a, w_br_b, w_o):
    n, t, _ = x.shape
    pos = jnp.arange(t, dtype=jnp.int32)
    qa, ka, va, qb, kb, vb, ga, gb = _project(x, w_in)
    qa, ka, qb, kb = _rope(qa, pos), _rope(ka, pos), _rope(qb, pos), _rope(kb, pos)
    s = _band_scores(qa.reshape(n, t, A_KV_HEADS, A_GROUP, HEAD_DIM), ka, A_WINDOW)
    p = _sink_softmax(s, _sink_logits(sink_a))
    oa = _band_values(p, va).reshape(n, t, A_Q)
    la = min(A_WINDOW, t)
    kv_a = jnp.stack([ka[:, t - la:], va[:, t - la:]], axis=2)
    outs, lses, kv_b = [], [], []
    for g, (win, dil) in enumerate(B_DIL):
        hs = slice(g * B_SLOTS, (g + 1) * B_SLOTS)
        qg, kg, vg = (_stride_split(z[:, :, hs], dil) for z in (qb, kb, vb))
        s = _band_scores(qg[:, :, :, None], kg, win // dil)
        p, lse = _softmax_lse(s)
        outs.append(_stride_merge(_band_values(p, vg)[:, :, :, 0], n, dil))
        lses.append(_stride_merge(_band_rows(lse, t // dil)[..., 0], n, dil))
        lb = min(win, t)
        kv_b.append(jnp.stack([kb[:, t - lb:, hs], vb[:, t - lb:, hs]], axis=2))
    ob = _combine_groups(outs, lses)
    return _merge(oa, ob, ga, gb, w_br_a, w_br_b, w_o), kv_a, kv_b


def _sample_mixer(x, cache_a, caches_b, w_in, sink_a, w_br_a, w_br_b, w_o):
    n, t, _ = x.shape
    qpos = PAST_LEN + jnp.arange(t, dtype=jnp.int32)
    qa, ka, va, qb, kb, vb, ga, gb = _project(x, w_in)
    qa, ka, qb, kb = _rope(qa, qpos), _rope(ka, qpos), _rope(qb, qpos), _rope(kb, qpos)
    la = cache_a.shape[1]
    k_all = jnp.concatenate([cache_a[:, :, 0], ka], axis=1)
    v_all = jnp.concatenate([cache_a[:, :, 1], va], axis=1)
    kpos = PAST_LEN - la + jnp.arange(la + t, dtype=jnp.int32)
    dist = qpos[:, None] - kpos[None, :]
    mask = (dist >= 0) & (dist < A_WINDOW)
    s = jnp.einsum('nqkgd,nskd->nkgqs', qa.reshape(n, t, A_KV_HEADS, A_GROUP, HEAD_DIM), k_all,
                   preferred_element_type=jnp.float32) * SCALE
    p = _sink_softmax(jnp.where(mask, s, NEG), _sink_logits(sink_a))
    oa = jnp.einsum('nkgqs,nskd->nqkgd', p.astype(v_all.dtype), v_all).reshape(n, t, A_Q)
    kv_a = jnp.stack([ka, va], axis=2)
    outs, lses, kv_b = [], [], []
    for g, ((win, dil), cache) in enumerate(zip(B_DIL, caches_b)):
        hs = slice(g * B_SLOTS, (g + 1) * B_SLOTS)
        lb = cache.shape[1]
        kg_new, vg_new = kb[:, :, hs], vb[:, :, hs]
        k_all = jnp.concatenate([cache[:, :, 0], kg_new], axis=1)
        v_all = jnp.concatenate([cache[:, :, 1], vg_new], axis=1)
        kp = qpos[:, None] - dil * jnp.arange(win // dil, dtype=jnp.int32)[None, :]
        idx = jnp.maximum(kp - (PAST_LEN - lb), 0)
        s = jnp.einsum('nqhd,nqjhd->nhqj', qb[:, :, hs], k_all[:, idx],
                       preferred_element_type=jnp.float32) * SCALE
        p, lse = _softmax_lse(jnp.where(kp >= 0, s, NEG))
        outs.append(jnp.einsum('nhqj,nqjhd->nqhd', p.astype(v_all.dtype), v_all[:, idx]))
        lses.append(lse.transpose(0, 2, 1))
        kv_b.append(jnp.stack([kg_new, vg_new], axis=2))
    ob = _combine_groups(outs, lses)
    return _merge(oa, ob, ga, gb, w_br_a, w_br_b, w_o), kv_a, kv_b


def _post_layer(x, mix, conv_prev, ln1_g, ln1_b, w_up, conv_w, conv_b, w_down, ln2_g, ln2_b):
    h = _layernorm(ALPHA * x + mix, ln1_g, ln1_b)
    u, v = jnp.split(h @ w_up, 2, axis=-1)
    if conv_prev is None:
        conv_prev = jnp.zeros((u.shape[0], CONV_W - 1, D_FF), u.dtype)
    ext = jnp.concatenate([conv_prev.astype(u.dtype), u], axis=1)
    t = u.shape[1]
    a = sum((conv_w[j] * ext[:, j:j + t] for j in range(CONV_W)), conv_b)
    f = (jax.nn.gelu(a, approximate=False) * v) @ w_down
    return _layernorm(ALPHA * h + f, ln2_g, ln2_b), ext[:, t:]


def setup_inputs(seed: int = 0) -> dict:
    key = jax.random.key(seed)
    ks = jax.random.split(key, 20)

    def nrm(k, shape, scale=1.0):
        return jax.random.normal(k, shape, jnp.float32) * scale

    la = min(A_WINDOW, PAST_LEN)
    lb1, lb2, lb3 = (min(w, PAST_LEN) for w, _ in B_DIL)
    bo = B_SLOTS * HEAD_DIM
    return {
        'x_prompt': nrm(ks[0], (BATCH, SEQ, D_MODEL)),
        'x_sample': nrm(ks[1], (DEC_BATCH, DEC_SEQ, D_MODEL)),
        'cache_a': nrm(ks[2], (DEPTH, DEC_BATCH, la, 2, A_KV_HEADS, HEAD_DIM)),
        'cache_b1': nrm(ks[3], (DEPTH, DEC_BATCH, lb1, 2, B_SLOTS, HEAD_DIM)),
        'cache_b2': nrm(ks[4], (DEPTH, DEC_BATCH, lb2, 2, B_SLOTS, HEAD_DIM)),
        'cache_b3': nrm(ks[5], (DEPTH, DEC_BATCH, lb3, 2, B_SLOTS, HEAD_DIM)),
        'state_conv': nrm(ks[6], (DEPTH, DEC_BATCH, CONV_W - 1, D_FF)),
        'w_in': nrm(ks[7], (DEPTH, D_MODEL, N_PROJ), D_MODEL ** -0.5),
        'sink_a': nrm(ks[8], (DEPTH, A_Q_HEADS), 0.5),
        'w_br_a': nrm(ks[9], (DEPTH, A_Q, D_MODEL), A_Q ** -0.5),
        'w_br_b': nrm(ks[10], (DEPTH, bo, D_MODEL), bo ** -0.5),
        'w_o': nrm(ks[11], (DEPTH, D_MODEL, D_MODEL), BETA * D_MODEL ** -0.5),
        'ln1_g': 1.0 + nrm(ks[12], (DEPTH, D_MODEL), 0.02),
        'ln1_b': nrm(ks[13], (DEPTH, D_MODEL), 0.02),
        'w_up': nrm(ks[14], (DEPTH, D_MODEL, 2 * D_FF), D_MODEL ** -0.5),
        'conv_w': nrm(ks[15], (DEPTH, CONV_W, D_FF), CONV_W ** -0.5),
        'conv_b': nrm(ks[16], (DEPTH, D_FF), 0.02),
        'w_down': nrm(ks[17], (DEPTH, D_FF, D_MODEL), BETA * D_FF ** -0.5),
        'ln2_g': 1.0 + nrm(ks[18], (DEPTH, D_MODEL), 0.02),
        'ln2_b': nrm(ks[19], (DEPTH, D_MODEL), 0.02),
    }


def reference(x_prompt, x_sample, cache_a, cache_b1, cache_b2, cache_b3, state_conv,
              w_in, sink_a, w_br_a, w_br_b, w_o, ln1_g, ln1_b,
              w_up, conv_w, conv_b, w_down, ln2_g, ln2_b):
    hp, hs = x_prompt, x_sample
    a_p, a_s, b_p, b_s, c_p, c_s = [], [], [], [], [], []
    for l in range(DEPTH):
        ffn = (ln1_g[l], ln1_b[l], w_up[l], conv_w[l], conv_b[l], w_down[l], ln2_g[l], ln2_b[l])
        mix_p, kva_p, kvb_p = _prompt_mixer(hp, w_in[l], sink_a[l], w_br_a[l], w_br_b[l], w_o[l])
        mix_s, kva_s, kvb_s = _sample_mixer(hs, cache_a[l], (cache_b1[l], cache_b2[l], cache_b3[l]),
                                            w_in[l], sink_a[l], w_br_a[l], w_br_b[l], w_o[l])
        hp, conv_p = _post_layer(hp, mix_p, None, *ffn)
        hs, conv_s = _post_layer(hs, mix_s, state_conv[l], *ffn)
        a_p.append(kva_p)
        a_s.append(kva_s)
        b_p.append(kvb_p)
        b_s.append(kvb_s)
        c_p.append(conv_p)
        c_s.append(conv_s)
    y_prompt, y_sample = hp, hs
    cache_a_prompt = jnp.stack(a_p)
    cache_a_sample = jnp.stack(a_s)
    cache_b1_prompt = jnp.stack([b[0] for b in b_p])
    cache_b1_sample = jnp.stack([b[0] for b in b_s])
    cache_b2_prompt = jnp.stack([b[1] for b in b_p])
    cache_b2_sample = jnp.stack([b[1] for b in b_s])
    cache_b3_prompt = jnp.stack([b[2] for b in b_p])
    cache_b3_sample = jnp.stack([b[2] for b in b_s])
    state_conv_prompt = jnp.stack(c_p)
    state_conv_sample = jnp.stack(c_s)
    return (y_prompt, y_sample, cache_a_prompt, cache_a_sample, cache_b1_prompt, cache_b1_sample,
            cache_b2_prompt, cache_b2_sample, cache_b3_prompt, cache_b3_sample,
            state_conv_prompt, state_conv_sample)
```

```python
import functools

import jax
import jax.numpy as jnp
from jax import lax
from jax.experimental import pallas as pl
from jax.experimental.pallas import tpu as pltpu

D_MODEL = 1024
BATCH = 8
SEQ = 2048
DEPTH = 1
DEC_BATCH = 128
DEC_SEQ = 4
PAST_LEN = 16384
HEAD_DIM = 64
A_Q_HEADS = 8
A_KV_HEADS = 2
A_GROUP = A_Q_HEADS // A_KV_HEADS
A_WINDOW = 128
B_DIL = ((128, 1), (512, 4), (2048, 16))
B_SLOTS = 4
BLK = 128
ROPE_THETA = 10000.0
D_FF = ((8 * D_MODEL // 3 + 127) // 128) * 128
CONV_W = 3
ALPHA = (2 * DEPTH) ** 0.25
LN_EPS = 1e-5
NEG = -1e30
SCALE = HEAD_DIM ** -0.5
A_Q = A_Q_HEADS * HEAD_DIM
A_KV = A_KV_HEADS * HEAD_DIM
B_G = B_SLOTS * HEAD_DIM
N_QKV = A_Q + 2 * A_KV + 3 * 3 * B_G

LANES = 128
VMEM_LIMIT = 56 * 1024 * 1024

BF16 = jnp.bfloat16
F32 = jnp.float32

_C_QA = 0
_C_KVA = _C_QA + A_Q
_C_QB = (_C_KVA + 2 * A_KV, _C_KVA + 2 * A_KV + 3 * B_G, _C_KVA + 2 * A_KV + 6 * B_G)
_C_KVB = tuple(c + B_G for c in _C_QB)


def _const_spec(shape):
    nd = len(shape)
    return pl.BlockSpec(shape, lambda *_: (0,) * nd, pipeline_mode=pl.Buffered(1))


def _nt_dot(a, b):
    return lax.dot_general(a, b, (((1,), (1,)), ((), ())), preferred_element_type=F32)


def _dot(a, b):
    return jnp.dot(a, b, preferred_element_type=F32)


def _rope(y, cos, sin_signed, first_half):
    outs = []
    for j in range(y.shape[1] // LANES):
        yj = y[:, j * LANES:(j + 1) * LANES]
        partner = jnp.where(first_half, pltpu.roll(yj, LANES - HEAD_DIM // 2, 1),
                            pltpu.roll(yj, HEAD_DIM // 2, 1))
        outs.append(yj * cos + partner * sin_signed)
    return outs[0] if len(outs) == 1 else jnp.concatenate(outs, axis=1)


def _qkv_kernel(x_ref, w_ref, cos_ref, sin_ref, *refs, tm, deint):
    if deint:
        (qa_ref, kva_ref, kva_bf_ref,
         qb1_ref, kvb1_ref, kvb1_bf_ref,
         qb2_ref, kvb2_ref, kvb2_bf_ref,
         qb3_ref, kvb3_ref, kvb3_bf_ref, xb_ref, y_ref) = refs
        qb_refs = (qb1_ref, qb2_ref, qb3_ref)
        kvb_refs = (kvb1_ref, kvb2_ref, kvb3_ref)
        kvb_bf_refs = (kvb1_bf_ref, kvb2_bf_ref, kvb3_bf_ref)
    else:
        (qa_ref, kva_ref, qb1_ref, kvb1_ref, qb2_ref, kvb2_ref, qb3_ref, kvb3_ref, xb_ref) = refs
        qb_refs = (qb1_ref, qb2_ref, qb3_ref)
        kvb_refs = (kvb1_ref, kvb2_ref, kvb3_ref)

    xb_ref[...] = x_ref[...].astype(BF16)
    cos = cos_ref[...]
    sin = sin_ref[...]
    lane = lax.broadcasted_iota(jnp.int32, (tm, LANES), 1)
    first_half = (lane & (HEAD_DIM - 1)) < HEAD_DIM // 2

    def proj(c0, width):
        return _dot(xb_ref[...], w_ref[:, c0:c0 + width])

    qa_ref[...] = (_rope(proj(_C_QA, A_Q), cos, sin, first_half) * SCALE).astype(BF16)
    kva = proj(_C_KVA, 2 * A_KV)
    kva = jnp.concatenate([_rope(kva[:, :A_KV], cos, sin, first_half), kva[:, A_KV:]], axis=1)
    kva_ref[...] = kva
    if deint:
        kva_bf_ref[...] = kva.astype(BF16)

    for g, (_, dil) in enumerate(B_DIL):
        q = _rope(proj(_C_QB[g], B_G), cos, sin, first_half) * SCALE
        kv = proj(_C_KVB[g], 2 * B_G)
        kv = jnp.concatenate([_rope(kv[:, :B_G], cos, sin, first_half), kv[:, B_G:]], axis=1)
        kvb_refs[g][...] = kv
        if not deint:
            qb_refs[g][...] = q.astype(BF16)
        elif dil == 1:
            qb_refs[g][...] = q.astype(BF16)
            kvb_bf_refs[g][...] = kv.astype(BF16)
        else:
            for val, out_ref in ((q, qb_refs[g]), (kv, kvb_bf_refs[g])):
                n_col = val.shape[1] // LANES
                for c in range(n_col):
                    y_ref[c] = val[:, c * LANES:(c + 1) * LANES]
                for r in range(dil):
                    out_ref[0, r] = jnp.concatenate(
                        [y_ref[c, pl.ds(r, tm // dil, stride=dil), :] for c in range(n_col)],
                        axis=1).astype(BF16)


def _qkv_call(x2d, w_qkv, cos_t, sin_t, *, tm, deint, tiles_per_seq, name):
    m = x2d.shape[0]
    n_tiles = m // tm
    row = lambda w: pl.BlockSpec((tm, w), lambda i: (i, 0))
    tab = pl.BlockSpec((tm, LANES), lambda i: (i % tiles_per_seq, 0))
    in_specs = [row(D_MODEL), _const_spec((D_MODEL, N_QKV)), tab, tab]
    sds = jax.ShapeDtypeStruct
    scratch = [pltpu.VMEM((tm, D_MODEL), BF16)]
    if deint:
        n_seq = m // SEQ

        def dspec(dil, w):
            return pl.BlockSpec((1, dil, tm // dil, w),
                                lambda i: (i // tiles_per_seq, 0, i % tiles_per_seq, 0))

        out_shape = [sds((m, A_Q), BF16), sds((m, 2 * A_KV), F32), sds((m, 2 * A_KV), BF16)]
        out_specs = [row(A_Q), row(2 * A_KV), row(2 * A_KV)]
        for _, dil in B_DIL:
            if dil == 1:
                out_shape += [sds((m, B_G), BF16), sds((m, 2 * B_G), F32), sds((m, 2 * B_G), BF16)]
                out_specs += [row(B_G), row(2 * B_G), row(2 * B_G)]
            else:
                out_shape += [sds((n_seq, dil, SEQ // dil, B_G), BF16), sds((m, 2 * B_G), F32),
                              sds((n_seq, dil, SEQ // dil, 2 * B_G), BF16)]
                out_specs += [dspec(dil, B_G), row(2 * B_G), dspec(dil, 2 * B_G)]
        scratch.append(pltpu.VMEM((2 * B_G // LANES, tm, LANES), F32))
    else:
        out_shape = [sds((m, A_Q), BF16), sds((m, 2 * A_KV), F32)]
        out_specs = [row(A_Q), row(2 * A_KV)]
        for _ in B_DIL:
            out_shape += [sds((m, B_G), BF16), sds((m, 2 * B_G), F32)]
            out_specs += [row(B_G), row(2 * B_G)]
    return pl.pallas_call(
        functools.partial(_qkv_kernel, tm=tm, deint=deint),
        grid=(n_tiles,),
        in_specs=in_specs,
        out_specs=out_specs,
        out_shape=out_shape,
        scratch_shapes=scratch,
        compiler_params=pltpu.CompilerParams(
            dimension_semantics=("arbitrary",), vmem_limit_bytes=VMEM_LIMIT),
        name=name,
    )(x2d, w_qkv, cos_t, sin_t)


def _fold_masks():
    row = lax.broadcasted_iota(jnp.int32, (BLK, BLK), 0)
    col = lax.broadcasted_iota(jnp.int32, (BLK, BLK), 1)
    return col > row


def _attn_a_kernel(sink_ref, q_ref, kv_ref, o_ref):
    n_blk = q_ref.shape[0] // BLK
    upper = _fold_masks()
    lane = lax.broadcasted_iota(jnp.int32, (BLK, LANES), 1)
    hi = lane >= HEAD_DIM
    mask_bf = (jnp.where(hi, 0.0, 1.0).astype(BF16), jnp.where(hi, 1.0, 0.0).astype(BF16))

    def body(b, carry):
        start = pl.multiple_of(b * BLK, BLK)
        pstart = pl.multiple_of(jnp.maximum(b - 1, 0) * BLK, BLK)
        has_prev = b > 0
        qblk = q_ref[pl.ds(start, BLK), :]
        k2 = jnp.concatenate([kv_ref[pl.ds(pstart, BLK), 0:A_KV],
                              kv_ref[pl.ds(start, BLK), 0:A_KV]], axis=0)
        v2 = jnp.concatenate([kv_ref[pl.ds(pstart, BLK), A_KV:2 * A_KV],
                              kv_ref[pl.ds(start, BLK), A_KV:2 * A_KV]], axis=0)
        outs = []
        for j in range(A_KV_HEADS):
            qs = jnp.concatenate(
                [qblk[:, g * LANES:(g + 1) * LANES] * mask_bf[j] for g in range(A_GROUP)], axis=0)
            s = _nt_dot(qs, k2)
            ps, dens = [], []
            for g in range(A_GROUP):
                sp = jnp.where(has_prev, s[g * BLK:(g + 1) * BLK, :BLK], NEG)
                sf = jnp.where(upper, sp, s[g * BLK:(g + 1) * BLK, BLK:])
                sink = sink_ref[j * A_GROUP + g]
                mx = jnp.maximum(jnp.max(sf, axis=-1, keepdims=True), sink)
                e = jnp.exp(sf - mx)
                dens.append(jnp.sum(e, axis=-1, keepdims=True) + jnp.exp(sink - mx))
                ps.append(jnp.concatenate([jnp.where(upper, e, 0.0), jnp.where(upper, 0.0, e)],
                                          axis=1).astype(BF16))
            o = _dot(jnp.concatenate(ps, axis=0), v2)
            outs.append([o[g * BLK:(g + 1) * BLK] / dens[g] for g in range(A_GROUP)])
        for g in range(A_GROUP):
            o_ref[pl.ds(start, BLK), g * LANES:(g + 1) * LANES] = jnp.where(
                hi, outs[1][g], outs[0][g]).astype(BF16)
        return carry

    lax.fori_loop(0, n_blk, body, 0)


def _attn_a_call(sink, q_a, kva_bf):
    n_seq = q_a.shape[0] // SEQ
    return pl.pallas_call(
        _attn_a_kernel,
        grid=(n_seq,),
        in_specs=[pl.BlockSpec(memory_space=pltpu.SMEM),
                  pl.BlockSpec((SEQ, A_Q), lambda n: (n, 0)),
                  pl.BlockSpec((SEQ, 2 * A_KV), lambda n: (n, 0))],
        out_specs=pl.BlockSpec((SEQ, A_Q), lambda n: (n, 0)),
        out_shape=jax.ShapeDtypeStruct(q_a.shape, BF16),
        compiler_params=pltpu.CompilerParams(
            dimension_semantics=("arbitrary",), vmem_limit_bytes=VMEM_LIMIT),
        name="attn_a_prompt",
    )(sink, q_a, kva_bf)


def _slot_masks(rows):
    lane = lax.broadcasted_iota(jnp.int32, (rows, B_G), 1)
    slot = lane >> 6
    masks = [slot == s for s in range(B_SLOTS)]
    masks_bf = [jnp.where(mk, 1.0, 0.0).astype(BF16) for mk in masks]
    return masks, masks_bf


def _band_block_b(qblk, kprev, kcur, vprev, vcur, has_prev, upper, masks, masks_bf):
    qs = jnp.concatenate([qblk * masks_bf[s] for s in range(B_SLOTS)], axis=0)
    if kprev is None:
        s_all = _nt_dot(qs, kcur)
        v2 = vcur
    else:
        s_all = _nt_dot(qs, jnp.concatenate([kprev, kcur], axis=0))
        v2 = jnp.concatenate([vprev, vcur], axis=0)
    ps, ms, ls = [], [], []
    for s in range(B_SLOTS):
        blk = s_all[s * BLK:(s + 1) * BLK]
        if kprev is None:
            sf = jnp.where(upper, NEG, blk)
        else:
            sf = jnp.where(upper, jnp.where(has_prev, blk[:, :BLK], NEG), blk[:, BLK:])
        mx = jnp.max(sf, axis=-1, keepdims=True)
        e = jnp.exp(sf - mx)
        ms.append(mx)
        ls.append(jnp.sum(e, axis=-1, keepdims=True))
        if kprev is None:
            ps.append(e.astype(BF16))
        else:
            ps.append(jnp.concatenate([jnp.where(upper, e, 0.0), jnp.where(upper, 0.0, e)],
                                      axis=1).astype(BF16))
    o = _dot(jnp.concatenate(ps, axis=0), v2)
    acc = jnp.where(masks[0], o[0:BLK], 0.0)
    mfull = jnp.broadcast_to(ms[0], (BLK, B_G))
    lfull = jnp.broadcast_to(ls[0], (BLK, B_G))
    for s in range(1, B_SLOTS):
        acc = jnp.where(masks[s], o[s * BLK:(s + 1) * BLK], acc)
        mfull = jnp.where(masks[s], ms[s], mfull)
        lfull = jnp.where(masks[s], ls[s], lfull)
    return acc, mfull, lfull


def _attn_b_kernel(q1_ref, kv1_ref, q2_ref, kv2_ref, q3_ref, kv3_ref, o_ref,
                   acc2_ref, m2_ref, l2_ref, acc3_ref, m3_ref, l3_ref):
    upper = _fold_masks()
    masks, masks_bf = _slot_masks(BLK)
    block = functools.partial(_band_block_b, upper=upper, masks=masks, masks_bf=masks_bf)
    d2, d3 = B_DIL[1][1], B_DIL[2][1]
    nb2 = SEQ // d2 // BLK
    assert SEQ // d3 == BLK
    n_plane = B_G // LANES

    def put(ref, rows, val):
        for c in range(n_plane):
            ref[c, rows, :] = val[:, c * LANES:(c + 1) * LANES]

    def get(ref, rows):
        return jnp.concatenate([ref[c, rows, :] for c in range(n_plane)], axis=1)

    def body3(r, carry):
        acc, mx, l = block(q3_ref[r], None, kv3_ref[r, :, 0:B_G], None, kv3_ref[r, :, B_G:2 * B_G],
                           None)
        rows = pl.ds(r, BLK, stride=d3)
        put(acc3_ref, rows, acc)
        put(m3_ref, rows, mx)
        put(l3_ref, rows, l)
        return carry

    lax.fori_loop(0, d3, body3, 0)

    def body2(idx, carry):
        r = idx // nb2
        b = idx % nb2
        start = pl.multiple_of(b * BLK, BLK)
        pstart = pl.multiple_of(jnp.maximum(b - 1, 0) * BLK, BLK)
        acc, mx, l = block(q2_ref[r, pl.ds(start, BLK), :],
                           kv2_ref[r, pl.ds(pstart, BLK), 0:B_G], kv2_ref[r, pl.ds(start, BLK), 0:B_G],
                           kv2_ref[r, pl.ds(pstart, BLK), B_G:2 * B_G],
                           kv2_ref[r, pl.ds(start, BLK), B_G:2 * B_G], b > 0)
        rows = pl.ds(r + b * (BLK * d2), BLK, stride=d2)
        put(acc2_ref, rows, acc)
        put(m2_ref, rows, mx)
        put(l2_ref, rows, l)
        return carry

    lax.fori_loop(0, d2 * nb2, body2, 0)

    def body1(b, carry):
        start = pl.multiple_of(b * BLK, BLK)
        pstart = pl.multiple_of(jnp.maximum(b - 1, 0) * BLK, BLK)
        acc1, m1, l1 = block(q1_ref[pl.ds(start, BLK), :],
                             kv1_ref[pl.ds(pstart, BLK), 0:B_G], kv1_ref[pl.ds(start, BLK), 0:B_G],
                             kv1_ref[pl.ds(pstart, BLK), B_G:2 * B_G],
                             kv1_ref[pl.ds(start, BLK), B_G:2 * B_G], b > 0)
        rows = pl.ds(start, BLK)
        m2, m3 = get(m2_ref, rows), get(m3_ref, rows)
        mx = jnp.maximum(jnp.maximum(m1, m2), m3)
        w1, w2, w3 = jnp.exp(m1 - mx), jnp.exp(m2 - mx), jnp.exp(m3 - mx)
        num = w1 * acc1 + w2 * get(acc2_ref, rows) + w3 * get(acc3_ref, rows)
        den = w1 * l1 + w2 * get(l2_ref, rows) + w3 * get(l3_ref, rows)
        o_ref[rows, :] = (num / den).astype(BF16)
        return carry

    lax.fori_loop(0, SEQ // BLK, body1, 0)


def _attn_b_call(q1, kv1, q2, kv2, q3, kv3):
    n_seq = q1.shape[0] // SEQ
    d2, d3 = B_DIL[1][1], B_DIL[2][1]
    rows = lambda w: pl.BlockSpec((SEQ, w), lambda n: (n, 0))
    dsp = lambda d, w: pl.BlockSpec((None, d, SEQ // d, w), lambda n: (n, 0, 0, 0))
    return pl.pallas_call(
        _attn_b_kernel,
        grid=(n_seq,),
        in_specs=[rows(B_G), rows(2 * B_G), dsp(d2, B_G), dsp(d2, 2 * B_G), dsp(d3, B_G),
                  dsp(d3, 2 * B_G)],
        out_specs=rows(B_G),
        out_shape=jax.ShapeDtypeStruct((q1.shape[0], B_G), BF16),
        scratch_shapes=[pltpu.VMEM((B_G // LANES, SEQ, LANES), F32) for _ in range(6)],
        compiler_params=pltpu.CompilerParams(
            dimension_semantics=("arbitrary",), vmem_limit_bytes=VMEM_LIMIT),
        name="attn_b_prompt",
    )(q1, kv1, q2, kv2, q3, kv3)


def _sample_attn_kernel(sink_ref, qa_ref, kna_ref, ca_ref, q1_ref, kn1_ref, c1_ref,
                        q2_ref, kn2_ref, c2_ref, q3_ref, kn3_ref, c3_ref, oa_ref, ob_ref, *, nb):
    tt = nb * DEC_SEQ
    pad = BLK - tt
    tt_shift, seq_shift = tt.bit_length() - 1, DEC_SEQ.bit_length() - 1
    assert tt == 1 << tt_shift and DEC_SEQ == 1 << seq_shift

    def pad_rows(x):
        return jnp.concatenate([x.astype(BF16), jnp.zeros((pad, x.shape[1]), BF16)], axis=0)

    def index_helpers(rows):
        row = lax.broadcasted_iota(jnp.int32, (rows, BLK), 0)
        col = lax.broadcasted_iota(jnp.int32, (rows, BLK), 1)
        tok = row & (tt - 1)
        return row, col, tok

    rows_a = A_Q_HEADS * tt
    row, col, tok = index_helpers(rows_a)
    qi = tok & (DEC_SEQ - 1)
    is_new = col <= qi
    lane_t = lax.broadcasted_iota(jnp.int32, (tt, LANES), 1)
    hi_t = lane_t >= HEAD_DIM
    mask_bf = (jnp.where(hi_t, 0.0, 1.0).astype(BF16), jnp.where(hi_t, 1.0, 0.0).astype(BF16))
    qa = qa_ref[...]
    qm = jnp.concatenate([qa[:, g * LANES:(g + 1) * LANES] * mask_bf[j]
                          for j in range(A_KV_HEADS) for g in range(A_GROUP)], axis=0)
    kna = kna_ref[...]
    knpad = pad_rows(kna[:, :A_KV])
    vnpad = pad_rows(kna[:, A_KV:])
    s_new = _nt_dot(qm, knpad)
    head = lax.broadcasted_iota(jnp.int32, (rows_a, 1), 0) >> tt_shift
    sinkv = jnp.zeros((rows_a, 1), F32)
    for h in range(A_Q_HEADS):
        sinkv = jnp.where(head == h, sink_ref[h], sinkv)
    seq_t = lax.broadcasted_iota(jnp.int32, (tt, LANES), 0) >> seq_shift
    out_a = [jnp.zeros((tt, LANES), F32) for _ in range(A_GROUP)]
    for n in range(nb):
        kc = ca_ref[n, :, 0:A_KV].astype(BF16)
        vc = ca_ref[n, :, A_KV:2 * A_KV].astype(BF16)
        s_c = _nt_dot(qm, kc)
        s_n = pltpu.roll(s_new, BLK - DEC_SEQ * n, 1) if n else s_new
        s = jnp.where(is_new, s_n, s_c)
        mx = jnp.maximum(jnp.max(s, axis=-1, keepdims=True), sinkv)
        e = jnp.exp(s - mx)
        den = jnp.sum(e, axis=-1, keepdims=True) + jnp.exp(sinkv - mx)
        p_new = jnp.where(is_new, e, 0.0)
        if n:
            p_new = pltpu.roll(p_new, DEC_SEQ * n, 1)
        o = (_dot(jnp.where(is_new, 0.0, e).astype(BF16), vc) + _dot(p_new.astype(BF16), vnpad)) / den
        for g in range(A_GROUP):
            og = jnp.where(hi_t, o[(A_GROUP + g) * tt:(A_GROUP + g + 1) * tt], o[g * tt:(g + 1) * tt])
            out_a[g] = jnp.where(seq_t == n, og, out_a[g])
    for g in range(A_GROUP):
        oa_ref[:, g * LANES:(g + 1) * LANES] = out_a[g].astype(BF16)

    rows_b = B_SLOTS * tt
    row, col, tok = index_helpers(rows_b)
    qi = tok & (DEC_SEQ - 1)
    masks_t, masks_bf_t = _slot_masks(tt)
    tok_t = lax.broadcasted_iota(jnp.int32, (tt, B_G), 0)

    def stack_q(q):
        return jnp.concatenate([q * masks_bf_t[s] for s in range(B_SLOTS)], axis=0)

    def unstack(o, mx, l):
        acc = jnp.where(masks_t[0], o[0:tt], 0.0)
        mf = jnp.broadcast_to(mx[0:tt], (tt, B_G))
        lf = jnp.broadcast_to(l[0:tt], (tt, B_G))
        for s in range(1, B_SLOTS):
            acc = jnp.where(masks_t[s], o[s * tt:(s + 1) * tt], acc)
            mf = jnp.where(masks_t[s], mx[s * tt:(s + 1) * tt], mf)
            lf = jnp.where(masks_t[s], l[s * tt:(s + 1) * tt], lf)
        return acc, mf, lf

    qm = stack_q(q1_ref[...])
    kn = kn1_ref[...]
    knpad = pad_rows(kn[:, :B_G])
    vnpad = pad_rows(kn[:, B_G:])
    s_new = _nt_dot(qm, knpad)
    is_new = col <= qi
    zero = jnp.zeros((tt, B_G), F32)
    acc1, m1, l1 = zero, zero, zero
    for n in range(nb):
        kc = c1_ref[n, :, 0:B_G].astype(BF16)
        vc = c1_ref[n, :, B_G:2 * B_G].astype(BF16)
        s_c = _nt_dot(qm, kc)
        s_n = pltpu.roll(s_new, BLK - DEC_SEQ * n, 1) if n else s_new
        s = jnp.where(is_new, s_n, s_c)
        mx = jnp.max(s, axis=-1, keepdims=True)
        e = jnp.exp(s - mx)
        l = jnp.sum(e, axis=-1, keepdims=True)
        p_new = jnp.where(is_new, e, 0.0)
        if n:
            p_new = pltpu.roll(p_new, DEC_SEQ * n, 1)
        o = _dot(jnp.where(is_new, 0.0, e).astype(BF16), vc) + _dot(p_new.astype(BF16), vnpad)
        a_n, m_n, l_n = unstack(o, mx, l)
        sel = (tok_t >> seq_shift) == n
        acc1, m1, l1 = jnp.where(sel, a_n, acc1), jnp.where(sel, m_n, m1), jnp.where(sel, l_n, l1)

    def strided_group(q_ref, kn_ref, c_ref):
        qm = stack_q(q_ref[...])
        kn = kn_ref[...]
        s_all = _nt_dot(qm, pad_rows(kn[:, :B_G]))
        s_own = jnp.sum(jnp.where(col == tok, s_all, 0.0), axis=-1, keepdims=True)
        vn = kn[:, B_G:]
        vn_rows = jnp.concatenate([vn] * B_SLOTS, axis=0)
        first = col == 0
        kv_w = 2 * B_G
        acc, mf, lf = zero, zero, zero
        for n in range(nb):
            for q in range(DEC_SEQ):
                kc = c_ref[n, :, q * kv_w:q * kv_w + B_G].astype(BF16)
                vc = c_ref[n, :, q * kv_w + B_G:(q + 1) * kv_w].astype(BF16)
                s = jnp.where(first, s_own, _nt_dot(qm, kc))
                mx = jnp.max(s, axis=-1, keepdims=True)
                e = jnp.exp(s - mx)
                l = jnp.sum(e, axis=-1, keepdims=True)
                o = _dot(jnp.where(first, 0.0, e).astype(BF16), vc) + jnp.exp(s_own - mx) * vn_rows
                a_n, m_n, l_n = unstack(o, mx, l)
                sel = tok_t == n * DEC_SEQ + q
                acc, mf, lf = jnp.where(sel, a_n, acc), jnp.where(sel, m_n, mf), jnp.where(sel, l_n, lf)
        return acc, mf, lf

    acc2, m2, l2 = strided_group(q2_ref, kn2_ref, c2_ref)
    acc3, m3, l3 = strided_group(q3_ref, kn3_ref, c3_ref)
    mx = jnp.maximum(jnp.maximum(m1, m2), m3)
    w1, w2, w3 = jnp.exp(m1 - mx), jnp.exp(m2 - mx), jnp.exp(m3 - mx)
    ob_ref[...] = ((w1 * acc1 + w2 * acc2 + w3 * acc3) / (w1 * l1 + w2 * l2 + w3 * l3)).astype(BF16)


def _sample_attn_call(sink, qa, kna, cache_a, q1, kn1, cache_b1, q2, kn2, cache_b2, q3, kn3, cache_b3,
                      *, nb):
    n_seq = cache_a.shape[0]
    tt = nb * DEC_SEQ
    kv_w = 2 * B_G
    tok = lambda w: pl.BlockSpec((tt, w), lambda i: (i, 0))
    cache = lambda w: pl.BlockSpec((nb, BLK, w), lambda i: (i, 0, 0))
    return pl.pallas_call(
        functools.partial(_sample_attn_kernel, nb=nb),
        grid=(n_seq // nb,),
        in_specs=[pl.BlockSpec(memory_space=pltpu.SMEM),
                  tok(A_Q), tok(2 * A_KV), cache(2 * A_KV),
                  tok(B_G), tok(kv_w), cache(kv_w),
                  tok(B_G), tok(kv_w), cache(DEC_SEQ * kv_w),
                  tok(B_G), tok(kv_w), cache(DEC_SEQ * kv_w)],
        out_specs=[tok(A_Q), tok(B_G)],
        out_shape=[jax.ShapeDtypeStruct((n_seq * DEC_SEQ, A_Q), BF16),
                   jax.ShapeDtypeStruct((n_seq * DEC_SEQ, B_G), BF16)],
        compiler_params=pltpu.CompilerParams(
            dimension_semantics=("arbitrary",), vmem_limit_bytes=VMEM_LIMIT),
        name="attn_sample",
    )(sink, qa, kna, cache_a, q1, kn1, cache_b1, q2, kn2, cache_b2, q3, kn3, cache_b3)


def _layernorm(x, g, b):
    mu = jnp.mean(x, axis=-1, keepdims=True)
    xc = x - mu
    var = jnp.mean(xc * xc, axis=-1, keepdims=True)
    return xc * lax.rsqrt(var + LN_EPS) * g + b


def _gelu_exact(x):
    return 0.5 * x * (1.0 + lax.erf(x * (0.5 ** 0.5)))


_FF_CHUNK = 256
_CARRY = 8


def _ffn_kernel(*refs, tm, sample, tiles_per_seq):
    if sample:
        (x_ref, oa_ref, ob_ref, e1_ref, e2_ref, wg_ref, wa_ref, wb_ref, wo_ref, g1_ref, b1_ref,
         wup_ref, cw_ref, cb_ref, wdn_ref, g2_ref, b2_ref, y_ref, u_ref,
         xb_ref, m_ref, h_ref, hb_ref, gg_ref, ext_ref) = refs
    else:
        (x_ref, oa_ref, ob_ref, wg_ref, wa_ref, wb_ref, wo_ref, g1_ref, b1_ref,
         wup_ref, cw_ref, cb_ref, wdn_ref, g2_ref, b2_ref, y_ref, ulast_ref,
         xb_ref, m_ref, h_ref, hb_ref, gg_ref, ext_ref, carry_ref) = refs

    x = x_ref[...]
    xb_ref[...] = x.astype(BF16)
    half = D_MODEL // 2
    for c in range(2):
        cs = slice(c * half, (c + 1) * half)
        ga = _dot(xb_ref[...], wg_ref[:, c * half:(c + 1) * half])
        gb = _dot(xb_ref[...], wg_ref[:, D_MODEL + c * half:D_MODEL + (c + 1) * half])
        ta = _dot(oa_ref[...], wa_ref[:, cs])
        tb = _dot(ob_ref[...], wb_ref[:, cs])
        m_ref[:, cs] = (jax.nn.sigmoid(ga) * ta + jax.nn.sigmoid(gb) * tb).astype(BF16)
    mix = _dot(m_ref[...], wo_ref[...])
    h = _layernorm(ALPHA * x + mix, g1_ref[...], b1_ref[...])
    h_ref[...] = h
    hb_ref[...] = h.astype(BF16)

    if sample:
        ext_ref[0:_CARRY, :] = jnp.zeros((_CARRY, _FF_CHUNK), F32)
        t = lax.broadcasted_iota(jnp.int32, (tm, _FF_CHUNK), 0) & (DEC_SEQ - 1)
    else:
        @pl.when(pl.program_id(0) % tiles_per_seq == 0)
        def _():
            carry_ref[...] = jnp.zeros_like(carry_ref)

    for c in range(D_FF // _FF_CHUNK):
        cs = slice(c * _FF_CHUNK, (c + 1) * _FF_CHUNK)
        u = _dot(hb_ref[...], wup_ref[:, c * _FF_CHUNK:(c + 1) * _FF_CHUNK])
        v = _dot(hb_ref[...], wup_ref[:, D_FF + c * _FF_CHUNK:D_FF + (c + 1) * _FF_CHUNK])
        if not sample:
            ext_ref[0:_CARRY, :] = carry_ref[:, cs]
        ext_ref[_CARRY:_CARRY + tm, :] = u
        u1 = ext_ref[_CARRY - 1:_CARRY - 1 + tm, :]
        u2 = ext_ref[_CARRY - 2:_CARRY - 2 + tm, :]
        if sample:
            u1 = jnp.where(t >= 1, u1, e1_ref[:, cs])
            u2 = jnp.where(t >= 2, u2, e2_ref[:, cs])
            u_ref[:, cs] = u
        else:
            tail = u[tm - _CARRY:tm, :]
            carry_ref[:, cs] = tail
            ulast_ref[0, :, cs] = tail
        a = cb_ref[:, cs] + cw_ref[0:1, cs] * u2 + cw_ref[1:2, cs] * u1 + cw_ref[2:3, cs] * u
        gg_ref[:, cs] = (_gelu_exact(a) * v).astype(BF16)
    f = _dot(gg_ref[...], wdn_ref[...])
    y_ref[...] = _layernorm(ALPHA * h_ref[...] + f, g2_ref[...], b2_ref[...])


def _ffn_call(x2d, oa, ob, weights, *, tm, sample, tiles_per_seq, conv_fill=None, name):
    m = x2d.shape[0]
    n_tiles = m // tm
    row = lambda w: pl.BlockSpec((tm, w), lambda i: (i, 0))
    sds = jax.ShapeDtypeStruct
    in_specs = [row(D_MODEL), row(A_Q), row(B_G)]
    args = [x2d, oa, ob]
    if sample:
        in_specs += [row(D_FF), row(D_FF)]
        args += list(conv_fill)
    in_specs += [_const_spec(w.shape) for w in weights]
    args += list(weights)
    scratch = [pltpu.VMEM((tm, D_MODEL), BF16), pltpu.VMEM((tm, D_MODEL), BF16),
               pltpu.VMEM((tm, D_MODEL), F32), pltpu.VMEM((tm, D_MODEL), BF16),
               pltpu.VMEM((tm, D_FF), BF16), pltpu.VMEM((tm + _CARRY, _FF_CHUNK), F32)]
    if sample:
        out_shape = [sds((m, D_MODEL), F32), sds((m, D_FF), F32)]
        out_specs = [row(D_MODEL), row(D_FF)]
    else:
        out_shape = [sds((m, D_MODEL), F32), sds((n_tiles, _CARRY, D_FF), F32)]
        out_specs = [row(D_MODEL), pl.BlockSpec((1, _CARRY, D_FF), lambda i: (i, 0, 0))]
        scratch.append(pltpu.VMEM((_CARRY, D_FF), F32))
    return pl.pallas_call(
        functools.partial(_ffn_kernel, tm=tm, sample=sample, tiles_per_seq=tiles_per_seq),
        grid=(n_tiles,),
        in_specs=in_specs,
        out_specs=out_specs,
        out_shape=out_shape,
        scratch_shapes=scratch,
        compiler_params=pltpu.CompilerParams(
            dimension_semantics=("arbitrary",), vmem_limit_bytes=VMEM_LIMIT),
        name=name,
    )(*args)


def _rope_tables(pos):
    half = HEAD_DIM // 2
    inv = ROPE_THETA ** (-jnp.arange(half, dtype=F32) / half)
    ang = pos.astype(F32)[:, None] * inv[None, :]
    cos, sin = jnp.cos(ang), jnp.sin(ang)
    reps = LANES // HEAD_DIM
    cos_t = jnp.tile(jnp.concatenate([cos, cos], axis=1), (1, reps))
    sin_t = jnp.tile(jnp.concatenate([-sin, sin], axis=1), (1, reps))
    return cos_t, sin_t


def _prep_qkv_weight(w_in):
    qa = w_in[:, 0:A_Q]
    ka = w_in[:, A_Q:A_Q + A_KV]
    va = w_in[:, A_Q + A_KV:A_Q + 2 * A_KV]
    b0 = A_Q + 2 * A_KV
    qb = w_in[:, b0:b0 + 3 * B_G]
    kb = w_in[:, b0 + 3 * B_G:b0 + 6 * B_G]
    vb = w_in[:, b0 + 6 * B_G:b0 + 9 * B_G]
    cols = []
    for c in range(A_GROUP):
        cols += [qa[:, c * HEAD_DIM:(c + 1) * HEAD_DIM],
                 qa[:, (A_GROUP + c) * HEAD_DIM:(A_GROUP + c + 1) * HEAD_DIM]]
    cols += [ka, va]
    for g in range(len(B_DIL)):
        gs = slice(g * B_G, (g + 1) * B_G)
        cols += [qb[:, gs], kb[:, gs], vb[:, gs]]
    return jnp.concatenate(cols, axis=1).astype(BF16)


def _prep_br_a(w_br_a):
    rows = []
    for c in range(A_GROUP):
        rows += [w_br_a[c * HEAD_DIM:(c + 1) * HEAD_DIM],
                 w_br_a[(A_GROUP + c) * HEAD_DIM:(A_GROUP + c + 1) * HEAD_DIM]]
    return jnp.concatenate(rows, axis=0).astype(BF16)


_TM_PROMPT = 512
_SAMPLE_SEQS_PER_STEP = 4


def kernel(x_prompt, x_sample, cache_a, cache_b1, cache_b2, cache_b3, state_conv, w_in, sink_a, w_br_a, w_br_b, w_o, ln1_g, ln1_b, w_up, conv_w, conv_b, w_down, ln2_g, ln2_b):
    assert DEPTH == 1
    l = 0
    w_qkv = _prep_qkv_weight(w_in[l])
    w_gate = w_in[l][:, N_QKV:].astype(BF16)
    ffn_weights = (w_gate, _prep_br_a(w_br_a[l]), w_br_b[l].astype(BF16), w_o[l].astype(BF16),
                   ln1_g[l][None], ln1_b[l][None], w_up[l].astype(BF16), conv_w[l], conv_b[l][None],
                   w_down[l].astype(BF16), ln2_g[l][None], ln2_b[l][None])
    sink = sink_a[l].astype(F32)

    mp = BATCH * SEQ
    xp = x_prompt.reshape(mp, D_MODEL)
    cos_p, sin_p = _rope_tables(jnp.arange(SEQ, dtype=jnp.int32))
    tps = SEQ // _TM_PROMPT
    (qa, kva, kva_bf, qb1, kvb1, kvb1_bf, qb2, kvb2, kvb2_bf, qb3, kvb3, kvb3_bf) = _qkv_call(
        xp, w_qkv, cos_p, sin_p, tm=_TM_PROMPT, deint=True, tiles_per_seq=tps, name="qkv_prompt")
    oa = _attn_a_call(sink, qa, kva_bf)
    ob = _attn_b_call(qb1, kvb1_bf, qb2, kvb2_bf, qb3, kvb3_bf)
    y_p, ulast = _ffn_call(xp, oa, ob, ffn_weights, tm=_TM_PROMPT, sample=False, tiles_per_seq=tps,
                           name="ffn_prompt")

    y_prompt = y_p.reshape(BATCH, SEQ, D_MODEL)
    cache_a_prompt = kva.reshape(BATCH, SEQ, 2, A_KV_HEADS, HEAD_DIM)[None, :, SEQ - min(A_WINDOW, SEQ):]
    kvb_p = []
    for (win, _), kv in zip(B_DIL, (kvb1, kvb2, kvb3)):
        kvb_p.append(kv.reshape(BATCH, SEQ, 2, B_SLOTS, HEAD_DIM)[None, :, SEQ - min(win, SEQ):])
    state_conv_prompt = ulast.reshape(BATCH, tps, _CARRY, D_FF)[None, :, tps - 1, _CARRY - (CONV_W - 1):]

    ms = DEC_BATCH * DEC_SEQ
    xs = x_sample.reshape(ms, D_MODEL)
    cos_s, sin_s = _rope_tables(PAST_LEN + (jnp.arange(ms, dtype=jnp.int32) % DEC_SEQ))
    (qa_s, kva_s, qb1_s, kvb1_s, qb2_s, kvb2_s, qb3_s, kvb3_s) = _qkv_call(
        xs, w_qkv, cos_s, sin_s, tm=ms, deint=False, tiles_per_seq=1, name="qkv_sample")
    ca = cache_a[l].reshape(DEC_BATCH, A_WINDOW, 2 * A_KV)
    cb = [c[l].reshape(DEC_BATCH, BLK, (win // BLK) * 2 * B_G)
          for c, (win, _) in zip((cache_b1, cache_b2, cache_b3), B_DIL)]
    oa_s, ob_s = _sample_attn_call(sink, qa_s, kva_s, ca, qb1_s, kvb1_s, cb[0], qb2_s, kvb2_s, cb[1],
                                   qb3_s, kvb3_s, cb[2], nb=_SAMPLE_SEQS_PER_STEP)
    sc = state_conv[l]
    zeros = lambda k: jnp.zeros((DEC_BATCH, k, D_FF), F32)
    e1 = jnp.concatenate([sc[:, 1:2], zeros(DEC_SEQ - 1)], axis=1).reshape(ms, D_FF)
    e2 = jnp.concatenate([sc, zeros(DEC_SEQ - 2)], axis=1).reshape(ms, D_FF)
    y_s, u_s = _ffn_call(xs, oa_s, ob_s, ffn_weights, tm=ms, sample=True, tiles_per_seq=1,
                         conv_fill=(e1, e2), name="ffn_sample")

    y_sample = y_s.reshape(DEC_BATCH, DEC_SEQ, D_MODEL)
    cache_a_sample = kva_s.reshape(DEC_BATCH, DEC_SEQ, 2, A_KV_HEADS, HEAD_DIM)[None]
    kvb_s = [kv.reshape(DEC_BATCH, DEC_SEQ, 2, B_SLOTS, HEAD_DIM)[None] for kv in (kvb1_s, kvb2_s, kvb3_s)]
    state_conv_sample = u_s.reshape(DEC_BATCH, DEC_SEQ, D_FF)[None, :, DEC_SEQ - (CONV_W - 1):]

    return (y_prompt, y_sample, cache_a_prompt, cache_a_sample, kvb_p[0], kvb_s[0], kvb_p[1], kvb_s[1],
            kvb_p[2], kvb_s[2], state_conv_prompt, state_conv_sample)
```

```python
import functools

import jax
import jax.numpy as jnp
from jax import lax
from jax.experimental import pallas as pl
from jax.experimental.pallas import tpu as pltpu

D_MODEL = 1024
BATCH = 8
SEQ = 2048
DEPTH = 1
DEC_BATCH = 128
DEC_SEQ = 4
PAST_LEN = 16384
HEAD_DIM = 64
A_Q_HEADS = 8
A_KV_HEADS = 2
A_GROUP = A_Q_HEADS // A_KV_HEADS
A_WINDOW = 128
B_DIL = ((128, 1), (512, 4), (2048, 16))
B_SLOTS = 4
BLK = 128
ROPE_THETA = 10000.0
D_FF = ((8 * D_MODEL // 3 + 127) // 128) * 128
CONV_W = 3
ALPHA = (2 * DEPTH) ** 0.25
LN_EPS = 1e-5
NEG = -1e30
SCALE = HEAD_DIM ** -0.5
A_Q = A_Q_HEADS * HEAD_DIM
A_KV = A_KV_HEADS * HEAD_DIM
B_G = B_SLOTS * HEAD_DIM
N_QKV = A_Q + 2 * A_KV + 3 * 3 * B_G

LANES = 128
VMEM_LIMIT = 56 * 1024 * 1024

BF16 = jnp.bfloat16
F32 = jnp.float32

_C_QA = 0
_C_KVA = _C_QA + A_Q
_C_QB = (_C_KVA + 2 * A_KV, _C_KVA + 2 * A_KV + 3 * B_G, _C_KVA + 2 * A_KV + 6 * B_G)
_C_KVB = tuple(c + B_G for c in _C_QB)


def _const_spec(shape):
    nd = len(shape)
    return pl.BlockSpec(shape, lambda *_: (0,) * nd, pipeline_mode=pl.Buffered(1))


def _nt_dot(a, b):
    return lax.dot_general(a, b, (((1,), (1,)), ((), ())), preferred_element_type=F32)


def _dot(a, b):
    return jnp.dot(a, b, preferred_element_type=F32)


def _rope(y, cos, sin_signed, first_half):
    outs = []
    for j in range(y.shape[1] // LANES):
        yj = y[:, j * LANES:(j + 1) * LANES]
        partner = jnp.where(first_half, pltpu.roll(yj, LANES - HEAD_DIM // 2, 1),
                            pltpu.roll(yj, HEAD_DIM // 2, 1))
        outs.append(yj * cos + partner * sin_signed)
    return outs[0] if len(outs) == 1 else jnp.concatenate(outs, axis=1)


def _qkv_kernel(x_ref, w_ref, cos_ref, sin_ref, *refs, tm, prompt, tiles_per_seq):
    out_refs, (xb_ref, y_ref) = refs[:-2], refs[-2:]
    q_refs, cache_refs, kv_refs = out_refs[0::3], out_refs[1::3], out_refs[2::3]
    groups = ((A_WINDOW, 1),) + B_DIL

    xb_ref[...] = x_ref[...].astype(BF16)
    cos = cos_ref[...]
    sin = sin_ref[...]
    lane = lax.broadcasted_iota(jnp.int32, (tm, LANES), 1)
    first_half = (lane & (HEAD_DIM - 1)) < HEAD_DIM // 2
    last_tile = (pl.program_id(0) % tiles_per_seq) == tiles_per_seq - 1

    def proj(c0, width):
        return _dot(xb_ref[...], w_ref[:, c0:c0 + width])

    def to_planes(val):
        n_col = val.shape[1] // LANES
        for c in range(n_col):
            y_ref[c] = val[:, c * LANES:(c + 1) * LANES]
        return n_col

    col_q = (_C_QA,) + _C_QB
    col_kv = (_C_KVA,) + _C_KVB
    for g, (win, dil) in enumerate(groups):
        wq = A_Q if g == 0 else B_G
        wk = A_KV if g == 0 else B_G
        q = _rope(proj(col_q[g], wq), cos, sin, first_half) * SCALE
        kv = proj(col_kv[g], 2 * wk)
        kv = jnp.concatenate([_rope(kv[:, :wk], cos, sin, first_half), kv[:, wk:]], axis=1)
        cache_ref = cache_refs[g]
        if not prompt:
            q_refs[g][...] = q.astype(BF16)
            kv_refs[g][...] = kv
            n_col = to_planes(kv)
            n_seq = tm // DEC_SEQ
            for t in range(DEC_SEQ):
                for c in range(n_col):
                    cache_ref[t, c * LANES:(c + 1) * LANES, :] = y_ref[c, pl.ds(t, n_seq, stride=DEC_SEQ), :].T
            continue

        keep = min(win, SEQ)
        if keep == SEQ:
            cache_ref[0] = kv.T
        else:
            assert keep <= tm

            @pl.when(last_tile)
            def _(kv=kv, cache_ref=cache_ref, keep=keep):
                cache_ref[0] = kv[tm - keep:, :].T

        if dil == 1:
            q_refs[g][...] = q.astype(BF16)
            kv_refs[g][...] = kv.astype(BF16)
        else:
            for val, out_ref in ((q, q_refs[g]), (kv, kv_refs[g])):
                n_col = to_planes(val)
                for r in range(dil):
                    out_ref[0, r] = jnp.concatenate(
                        [y_ref[c, pl.ds(r, tm // dil, stride=dil), :] for c in range(n_col)],
                        axis=1).astype(BF16)


def _qkv_call(x2d, w_qkv, cos_t, sin_t, *, tm, prompt, tiles_per_seq, name):
    m = x2d.shape[0]
    n_tiles = m // tm
    row = lambda w: pl.BlockSpec((tm, w), lambda i: (i, 0))
    tab = pl.BlockSpec((tm, LANES), lambda i: (i % tiles_per_seq, 0))
    in_specs = [row(D_MODEL), _const_spec((D_MODEL, N_QKV)), tab, tab]
    sds = jax.ShapeDtypeStruct
    out_shape, out_specs = [], []
    for g, (win, dil) in enumerate(((A_WINDOW, 1),) + B_DIL):
        wq = A_Q if g == 0 else B_G
        wkv = 2 * (A_KV if g == 0 else B_G)
        if not prompt:
            assert n_tiles == 1
            out_shape += [sds((m, wq), BF16), sds((DEC_SEQ, wkv, m // DEC_SEQ), F32), sds((m, wkv), F32)]
            out_specs += [row(wq), pl.BlockSpec((DEC_SEQ, wkv, m // DEC_SEQ), lambda i: (0, 0, 0)),
                          row(wkv)]
            continue
        n_seq = m // SEQ
        keep = min(win, SEQ)
        if keep == SEQ:
            cache_spec = pl.BlockSpec((1, wkv, tm), lambda i: (i // tiles_per_seq, 0, i % tiles_per_seq))
        else:
            cache_spec = pl.BlockSpec((1, wkv, keep), lambda i: (i // tiles_per_seq, 0, 0))
        if dil == 1:
            out_shape += [sds((m, wq), BF16), sds((n_seq, wkv, keep), F32), sds((m, wkv), BF16)]
            out_specs += [row(wq), cache_spec, row(wkv)]
        else:
            dspec = lambda w, dil=dil: pl.BlockSpec(
                (1, dil, tm // dil, w), lambda i: (i // tiles_per_seq, 0, i % tiles_per_seq, 0))
            out_shape += [sds((n_seq, dil, SEQ // dil, wq), BF16), sds((n_seq, wkv, keep), F32),
                          sds((n_seq, dil, SEQ // dil, wkv), BF16)]
            out_specs += [dspec(wq), cache_spec, dspec(wkv)]
    return pl.pallas_call(
        functools.partial(_qkv_kernel, tm=tm, prompt=prompt, tiles_per_seq=tiles_per_seq),
        grid=(n_tiles,),
        in_specs=in_specs,
        out_specs=out_specs,
        out_shape=out_shape,
        scratch_shapes=[pltpu.VMEM((tm, D_MODEL), BF16),
                        pltpu.VMEM((2 * B_G // LANES, tm, LANES), F32)],
        compiler_params=pltpu.CompilerParams(
            dimension_semantics=("arbitrary",), vmem_limit_bytes=VMEM_LIMIT),
        name=name,
    )(x2d, w_qkv, cos_t, sin_t)


def _fold_masks():
    row = lax.broadcasted_iota(jnp.int32, (BLK, BLK), 0)
    col = lax.broadcasted_iota(jnp.int32, (BLK, BLK), 1)
    return col > row


def _attn_a_kernel(sink_ref, q_ref, kv_ref, o_ref):
    n_blk = q_ref.shape[0] // BLK
    upper = _fold_masks()
    lane = lax.broadcasted_iota(jnp.int32, (BLK, LANES), 1)
    hi = lane >= HEAD_DIM
    mask_bf = (jnp.where(hi, 0.0, 1.0).astype(BF16), jnp.where(hi, 1.0, 0.0).astype(BF16))

    def body(b, carry):
        start = pl.multiple_of(b * BLK, BLK)
        pstart = pl.multiple_of(jnp.maximum(b - 1, 0) * BLK, BLK)
        has_prev = b > 0
        qblk = q_ref[pl.ds(start, BLK), :]
        k2 = jnp.concatenate([kv_ref[pl.ds(pstart, BLK), 0:A_KV],
                              kv_ref[pl.ds(start, BLK), 0:A_KV]], axis=0)
        v2 = jnp.concatenate([kv_ref[pl.ds(pstart, BLK), A_KV:2 * A_KV],
                              kv_ref[pl.ds(start, BLK), A_KV:2 * A_KV]], axis=0)
        outs = []
        for j in range(A_KV_HEADS):
            qs = jnp.concatenate(
                [qblk[:, g * LANES:(g + 1) * LANES] * mask_bf[j] for g in range(A_GROUP)], axis=0)
            s = _nt_dot(qs, k2)
            ps, dens = [], []
            for g in range(A_GROUP):
                sp = jnp.where(has_prev, s[g * BLK:(g + 1) * BLK, :BLK], NEG)
                sf = jnp.where(upper, sp, s[g * BLK:(g + 1) * BLK, BLK:])
                sink = sink_ref[j * A_GROUP + g]
                mx = jnp.maximum(jnp.max(sf, axis=-1, keepdims=True), sink)
                e = jnp.exp(sf - mx)
                dens.append(jnp.sum(e, axis=-1, keepdims=True) + jnp.exp(sink - mx))
                ps.append(jnp.concatenate([jnp.where(upper, e, 0.0), jnp.where(upper, 0.0, e)],
                                          axis=1).astype(BF16))
            o = _dot(jnp.concatenate(ps, axis=0), v2)
            outs.append([o[g * BLK:(g + 1) * BLK] / dens[g] for g in range(A_GROUP)])
        for g in range(A_GROUP):
            o_ref[pl.ds(start, BLK), g * LANES:(g + 1) * LANES] = jnp.where(
                hi, outs[1][g], outs[0][g]).astype(BF16)
        return carry

    lax.fori_loop(0, n_blk, body, 0)


def _attn_a_call(sink, q_a, kva_bf):
    n_seq = q_a.shape[0] // SEQ
    return pl.pallas_call(
        _attn_a_kernel,
        grid=(n_seq,),
        in_specs=[pl.BlockSpec(memory_space=pltpu.SMEM),
                  pl.BlockSpec((SEQ, A_Q), lambda n: (n, 0)),
                  pl.BlockSpec((SEQ, 2 * A_KV), lambda n: (n, 0))],
        out_specs=pl.BlockSpec((SEQ, A_Q), lambda n: (n, 0)),
        out_shape=jax.ShapeDtypeStruct(q_a.shape, BF16),
        compiler_params=pltpu.CompilerParams(
            dimension_semantics=("arbitrary",), vmem_limit_bytes=VMEM_LIMIT),
        name="attn_a_prompt",
    )(sink, q_a, kva_bf)


def _slot_masks(rows):
    lane = lax.broadcasted_iota(jnp.int32, (rows, B_G), 1)
    slot = lane >> 6
    masks = [slot == s for s in range(B_SLOTS)]
    masks_bf = [jnp.where(mk, 1.0, 0.0).astype(BF16) for mk in masks]
    return masks, masks_bf


def _band_block_b(qblk, kprev, kcur, vprev, vcur, has_prev, upper, masks, masks_bf):
    qs = jnp.concatenate([qblk * masks_bf[s] for s in range(B_SLOTS)], axis=0)
    if kprev is None:
        s_all = _nt_dot(qs, kcur)
        v2 = vcur
    else:
        s_all = _nt_dot(qs, jnp.concatenate([kprev, kcur], axis=0))
        v2 = jnp.concatenate([vprev, vcur], axis=0)
    ps, ms, ls = [], [], []
    for s in range(B_SLOTS):
        blk = s_all[s * BLK:(s + 1) * BLK]
        if kprev is None:
            sf = jnp.where(upper, NEG, blk)
        else:
            sf = jnp.where(upper, jnp.where(has_prev, blk[:, :BLK], NEG), blk[:, BLK:])
        mx = jnp.max(sf, axis=-1, keepdims=True)
        e = jnp.exp(sf - mx)
        ms.append(mx)
        ls.append(jnp.sum(e, axis=-1, keepdims=True))
        if kprev is None:
            ps.append(e.astype(BF16))
        else:
            ps.append(jnp.concatenate([jnp.where(upper, e, 0.0), jnp.where(upper, 0.0, e)],
                                      axis=1).astype(BF16))
    o = _dot(jnp.concatenate(ps, axis=0), v2)
    acc = jnp.where(masks[0], o[0:BLK], 0.0)
    mfull = jnp.broadcast_to(ms[0], (BLK, B_G))
    lfull = jnp.broadcast_to(ls[0], (BLK, B_G))
    for s in range(1, B_SLOTS):
        acc = jnp.where(masks[s], o[s * BLK:(s + 1) * BLK], acc)
        mfull = jnp.where(masks[s], ms[s], mfull)
        lfull = jnp.where(masks[s], ls[s], lfull)
    return acc, mfull, lfull


def _attn_b_kernel(q1_ref, kv1_ref, q2_ref, kv2_ref, q3_ref, kv3_ref, o_ref,
                   acc2_ref, m2_ref, l2_ref, acc3_ref, m3_ref, l3_ref):
    upper = _fold_masks()
    masks, masks_bf = _slot_masks(BLK)
    block = functools.partial(_band_block_b, upper=upper, masks=masks, masks_bf=masks_bf)
    d2, d3 = B_DIL[1][1], B_DIL[2][1]
    nb2 = SEQ // d2 // BLK
    assert SEQ // d3 == BLK
    n_plane = B_G // LANES

    def put(ref, rows, val):
        for c in range(n_plane):
            ref[c, rows, :] = val[:, c * LANES:(c + 1) * LANES]

    def get(ref, rows):
        return jnp.concatenate([ref[c, rows, :] for c in range(n_plane)], axis=1)

    def body3(r, carry):
        acc, mx, l = block(q3_ref[r], None, kv3_ref[r, :, 0:B_G], None, kv3_ref[r, :, B_G:2 * B_G],
                           None)
        rows = pl.ds(r, BLK, stride=d3)
        put(acc3_ref, rows, acc)
        put(m3_ref, rows, mx)
        put(l3_ref, rows, l)
        return carry

    lax.fori_loop(0, d3, body3, 0)

    def body2(idx, carry):
        r = idx // nb2
        b = idx % nb2
        start = pl.multiple_of(b * BLK, BLK)
        pstart = pl.multiple_of(jnp.maximum(b - 1, 0) * BLK, BLK)
        acc, mx, l = block(q2_ref[r, pl.ds(start, BLK), :],
                           kv2_ref[r, pl.ds(pstart, BLK), 0:B_G], kv2_ref[r, pl.ds(start, BLK), 0:B_G],
                           kv2_ref[r, pl.ds(pstart, BLK), B_G:2 * B_G],
                           kv2_ref[r, pl.ds(start, BLK), B_G:2 * B_G], b > 0)
        rows = pl.ds(r + b * (BLK * d2), BLK, stride=d2)
        put(acc2_ref, rows, acc)
        put(m2_ref, rows, mx)
        put(l2_ref, rows, l)
        return carry

    lax.fori_loop(0, d2 * nb2, body2, 0)

    def body1(b, carry):
        start = pl.multiple_of(b * BLK, BLK)
        pstart = pl.multiple_of(jnp.maximum(b - 1, 0) * BLK, BLK)
        acc1, m1, l1 = block(q1_ref[pl.ds(start, BLK), :],
                             kv1_ref[pl.ds(pstart, BLK), 0:B_G], kv1_ref[pl.ds(start, BLK), 0:B_G],
                             kv1_ref[pl.ds(pstart, BLK), B_G:2 * B_G],
                             kv1_ref[pl.ds(start, BLK), B_G:2 * B_G], b > 0)
        rows = pl.ds(start, BLK)
        m2, m3 = get(m2_ref, rows), get(m3_ref, rows)
        mx = jnp.maximum(jnp.maximum(m1, m2), m3)
        w1, w2, w3 = jnp.exp(m1 - mx), jnp.exp(m2 - mx), jnp.exp(m3 - mx)
        num = w1 * acc1 + w2 * get(acc2_ref, rows) + w3 * get(acc3_ref, rows)
        den = w1 * l1 + w2 * get(l2_ref, rows) + w3 * get(l3_ref, rows)
        o_ref[rows, :] = (num / den).astype(BF16)
        return carry

    lax.fori_loop(0, SEQ // BLK, body1, 0)


def _attn_b_call(q1, kv1, q2, kv2, q3, kv3):
    n_seq = q1.shape[0] // SEQ
    d2, d3 = B_DIL[1][1], B_DIL[2][1]
    rows = lambda w: pl.BlockSpec((SEQ, w), lambda n: (n, 0))
    dsp = lambda d, w: pl.BlockSpec((None, d, SEQ // d, w), lambda n: (n, 0, 0, 0))
    return pl.pallas_call(
        _attn_b_kernel,
        grid=(n_seq,),
        in_specs=[rows(B_G), rows(2 * B_G), dsp(d2, B_G), dsp(d2, 2 * B_G), dsp(d3, B_G),
                  dsp(d3, 2 * B_G)],
        out_specs=rows(B_G),
        out_shape=jax.ShapeDtypeStruct((q1.shape[0], B_G), BF16),
        scratch_shapes=[pltpu.VMEM((B_G // LANES, SEQ, LANES), F32) for _ in range(6)],
        compiler_params=pltpu.CompilerParams(
            dimension_semantics=("arbitrary",), vmem_limit_bytes=VMEM_LIMIT),
        name="attn_b_prompt",
    )(q1, kv1, q2, kv2, q3, kv3)


def _sample_attn_kernel(sink_ref, qa_ref, kna_ref, ca_ref, q1_ref, kn1_ref, c1_ref,
                        q2_ref, kn2_ref, c2_ref, q3_ref, kn3_ref, c3_ref, oa_ref, ob_ref, *, nb):
    tt = nb * DEC_SEQ
    pad = BLK - tt
    tt_shift, seq_shift = tt.bit_length() - 1, DEC_SEQ.bit_length() - 1
    assert tt == 1 << tt_shift and DEC_SEQ == 1 << seq_shift

    def pad_rows(x):
        return jnp.concatenate([x.astype(BF16), jnp.zeros((pad, x.shape[1]), BF16)], axis=0)

    def window_group(qm, kn_ref, c_ref, dil, sinkv, finish):
        rows, kd = qm.shape
        lc = c_ref.shape[2]
        kn = kn_ref[...]
        knpad = pad_rows(kn[:, :kd])
        vnpad = pad_rows(kn[:, kd:])
        s_new = _nt_dot(qm, knpad)
        def masks(width):
            col = lax.broadcasted_iota(jnp.int32, (rows, width), 1)
            t = lax.broadcasted_iota(jnp.int32, (rows, width), 0) & (DEC_SEQ - 1)
            same_res = (col & (dil - 1)) == (t & (dil - 1))
            return (col > t) & same_res, (col <= t) & same_res

        valid, _ = masks(lc)
        _, is_new = masks(BLK)
        for n in range(nb):
            k_t = c_ref[n, 0:kd, :].astype(BF16)
            v_t = c_ref[n, kd:2 * kd, :].astype(BF16)
            s_c = jnp.where(valid, _dot(qm, k_t), NEG)
            s_n = pltpu.roll(s_new, BLK - DEC_SEQ * n, 1) if n else s_new
            first = jnp.where(is_new, s_n, s_c[:, :BLK])
            s = first if lc == BLK else jnp.concatenate([first, s_c[:, BLK:]], axis=1)
            mx = jnp.max(s, axis=-1, keepdims=True)
            if sinkv is not None:
                mx = jnp.maximum(mx, sinkv)
            e = jnp.exp(s - mx)
            l = jnp.sum(e, axis=-1, keepdims=True)
            if sinkv is not None:
                l = l + jnp.exp(sinkv - mx)
            p_new = jnp.where(is_new, e[:, :BLK], 0.0)
            if n:
                p_new = pltpu.roll(p_new, DEC_SEQ * n, 1)
            o = _nt_dot(jnp.where(valid, e, 0.0).astype(BF16), v_t) + _dot(p_new.astype(BF16), vnpad)
            finish(n, o, mx, l)

    rows_a = A_Q_HEADS * tt
    lane_t = lax.broadcasted_iota(jnp.int32, (tt, LANES), 1)
    hi_t = lane_t >= HEAD_DIM
    mask_bf = (jnp.where(hi_t, 0.0, 1.0).astype(BF16), jnp.where(hi_t, 1.0, 0.0).astype(BF16))
    qa = qa_ref[...]
    qm = jnp.concatenate([qa[:, g * LANES:(g + 1) * LANES] * mask_bf[j]
                          for j in range(A_KV_HEADS) for g in range(A_GROUP)], axis=0)
    head = lax.broadcasted_iota(jnp.int32, (rows_a, 1), 0) >> tt_shift
    sinkv = jnp.zeros((rows_a, 1), F32)
    for h in range(A_Q_HEADS):
        sinkv = jnp.where(head == h, sink_ref[h], sinkv)
    seq_t = lax.broadcasted_iota(jnp.int32, (tt, LANES), 0) >> seq_shift
    out_a = [jnp.zeros((tt, LANES), F32) for _ in range(A_GROUP)]

    def finish_a(n, o, mx, l):
        o = o / l
        for g in range(A_GROUP):
            og = jnp.where(hi_t, o[(A_GROUP + g) * tt:(A_GROUP + g + 1) * tt], o[g * tt:(g + 1) * tt])
            out_a[g] = jnp.where(seq_t == n, og, out_a[g])

    window_group(qm, kna_ref, ca_ref, 1, sinkv, finish_a)
    for g in range(A_GROUP):
        oa_ref[:, g * LANES:(g + 1) * LANES] = out_a[g].astype(BF16)

    masks_t, masks_bf_t = _slot_masks(tt)
    tok_t = lax.broadcasted_iota(jnp.int32, (tt, B_G), 0)

    def stack_q(q):
        return jnp.concatenate([q * masks_bf_t[s] for s in range(B_SLOTS)], axis=0)

    def unstack(o, mx, l):
        acc = jnp.where(masks_t[0], o[0:tt], 0.0)
        mf = jnp.broadcast_to(mx[0:tt], (tt, B_G))
        lf = jnp.broadcast_to(l[0:tt], (tt, B_G))
        for s in range(1, B_SLOTS):
            acc = jnp.where(masks_t[s], o[s * tt:(s + 1) * tt], acc)
            mf = jnp.where(masks_t[s], mx[s * tt:(s + 1) * tt], mf)
            lf = jnp.where(masks_t[s], l[s * tt:(s + 1) * tt], lf)
        return acc, mf, lf

    zero = jnp.zeros((tt, B_G), F32)
    stats = []
    for (win, dil), q_ref, kn_ref, c_ref in zip(B_DIL, (q1_ref, q2_ref, q3_ref),
                                                (kn1_ref, kn2_ref, kn3_ref), (c1_ref, c2_ref, c3_ref)):
        assert c_ref.shape[2] == win
        group = [zero, zero, zero]

        def finish_b(n, o, mx, l, group=group):
            sel = (tok_t >> seq_shift) == n
            for k, new in enumerate(unstack(o, mx, l)):
                group[k] = jnp.where(sel, new, group[k])

        window_group(stack_q(q_ref[...]), kn_ref, c_ref, dil, None, finish_b)
        stats.append(group)

    (acc1, m1, l1), (acc2, m2, l2), (acc3, m3, l3) = stats
    mx = jnp.maximum(jnp.maximum(m1, m2), m3)
    w1, w2, w3 = jnp.exp(m1 - mx), jnp.exp(m2 - mx), jnp.exp(m3 - mx)
    ob_ref[...] = ((w1 * acc1 + w2 * acc2 + w3 * acc3) / (w1 * l1 + w2 * l2 + w3 * l3)).astype(BF16)


def _sample_attn_call(sink, qa, kna, cache_a, q1, kn1, cache_b1, q2, kn2, cache_b2, q3, kn3, cache_b3,
                      *, nb):
    n_seq = cache_a.shape[0]
    tt = nb * DEC_SEQ
    kv_w = 2 * B_G
    tok = lambda w: pl.BlockSpec((tt, w), lambda i: (i, 0))
    cache = lambda c: pl.BlockSpec((nb,) + c.shape[1:], lambda i: (i, 0, 0))
    return pl.pallas_call(
        functools.partial(_sample_attn_kernel, nb=nb),
        grid=(n_seq // nb,),
        in_specs=[pl.BlockSpec(memory_space=pltpu.SMEM),
                  tok(A_Q), tok(2 * A_KV), cache(cache_a),
                  tok(B_G), tok(kv_w), cache(cache_b1),
                  tok(B_G), tok(kv_w), cache(cache_b2),
                  tok(B_G), tok(kv_w), cache(cache_b3)],
        out_specs=[tok(A_Q), tok(B_G)],
        out_shape=[jax.ShapeDtypeStruct((n_seq * DEC_SEQ, A_Q), BF16),
                   jax.ShapeDtypeStruct((n_seq * DEC_SEQ, B_G), BF16)],
        compiler_params=pltpu.CompilerParams(
            dimension_semantics=("arbitrary",), vmem_limit_bytes=VMEM_LIMIT),
        name="attn_sample",
    )(sink, qa, kna, cache_a, q1, kn1, cache_b1, q2, kn2, cache_b2, q3, kn3, cache_b3)


def _layernorm(x, g, b):
    mu = jnp.mean(x, axis=-1, keepdims=True)
    xc = x - mu
    var = jnp.mean(xc * xc, axis=-1, keepdims=True)
    return xc * lax.rsqrt(var + LN_EPS) * g + b


def _gelu_exact(x):
    return 0.5 * x * (1.0 + lax.erf(x * (0.5 ** 0.5)))


_FF_CHUNK = 256
_CARRY = 8


def _ffn_kernel(*refs, tm, sample, tiles_per_seq):
    if sample:
        (x_ref, oa_ref, ob_ref, e1_ref, e2_ref, wg_ref, wa_ref, wb_ref, wo_ref, g1_ref, b1_ref,
         wup_ref, cw_ref, cb_ref, wdn_ref, g2_ref, b2_ref, y_ref, u_ref,
         xb_ref, m_ref, h_ref, hb_ref, gg_ref, ext_ref) = refs
    else:
        (x_ref, oa_ref, ob_ref, wg_ref, wa_ref, wb_ref, wo_ref, g1_ref, b1_ref,
         wup_ref, cw_ref, cb_ref, wdn_ref, g2_ref, b2_ref, y_ref, ulast_ref,
         xb_ref, m_ref, h_ref, hb_ref, gg_ref, ext_ref, carry_ref) = refs

    x = x_ref[...]
    xb_ref[...] = x.astype(BF16)
    half = D_MODEL // 2
    for c in range(2):
        cs = slice(c * half, (c + 1) * half)
        ga = _dot(xb_ref[...], wg_ref[:, c * half:(c + 1) * half])
        gb = _dot(xb_ref[...], wg_ref[:, D_MODEL + c * half:D_MODEL + (c + 1) * half])
        ta = _dot(oa_ref[...], wa_ref[:, cs])
        tb = _dot(ob_ref[...], wb_ref[:, cs])
        m_ref[:, cs] = (jax.nn.sigmoid(ga) * ta + jax.nn.sigmoid(gb) * tb).astype(BF16)
    mix = _dot(m_ref[...], wo_ref[...])
    h = _layernorm(ALPHA * x + mix, g1_ref[...], b1_ref[...])
    h_ref[...] = h
    hb_ref[...] = h.astype(BF16)

    if sample:
        ext_ref[0:_CARRY, :] = jnp.zeros((_CARRY, _FF_CHUNK), F32)
        t = lax.broadcasted_iota(jnp.int32, (tm, _FF_CHUNK), 0) & (DEC_SEQ - 1)
    else:
        @pl.when(pl.program_id(0) % tiles_per_seq == 0)
        def _():
            carry_ref[...] = jnp.zeros_like(carry_ref)

    for c in range(D_FF // _FF_CHUNK):
        cs = slice(c * _FF_CHUNK, (c + 1) * _FF_CHUNK)
        u = _dot(hb_ref[...], wup_ref[:, c * _FF_CHUNK:(c + 1) * _FF_CHUNK])
        v = _dot(hb_ref[...], wup_ref[:, D_FF + c * _FF_CHUNK:D_FF + (c + 1) * _FF_CHUNK])
        if not sample:
            ext_ref[0:_CARRY, :] = carry_ref[:, cs]
        ext_ref[_CARRY:_CARRY + tm, :] = u
        u1 = ext_ref[_CARRY - 1:_CARRY - 1 + tm, :]
        u2 = ext_ref[_CARRY - 2:_CARRY - 2 + tm, :]
        if sample:
            u1 = jnp.where(t >= 1, u1, e1_ref[:, cs])
            u2 = jnp.where(t >= 2, u2, e2_ref[:, cs])
            u_ref[:, cs] = u
        else:
            tail = u[tm - _CARRY:tm, :]
            carry_ref[:, cs] = tail
            ulast_ref[0, :, cs] = tail
        a = cb_ref[:, cs] + cw_ref[0:1, cs] * u2 + cw_ref[1:2, cs] * u1 + cw_ref[2:3, cs] * u
        gg_ref[:, cs] = (_gelu_exact(a) * v).astype(BF16)
    f = _dot(gg_ref[...], wdn_ref[...])
    y_ref[...] = _layernorm(ALPHA * h_ref[...] + f, g2_ref[...], b2_ref[...])


def _ffn_call(x2d, oa, ob, weights, *, tm, sample, tiles_per_seq, conv_fill=None, name):
    m = x2d.shape[0]
    n_tiles = m // tm
    row = lambda w: pl.BlockSpec((tm, w), lambda i: (i, 0))
    sds = jax.ShapeDtypeStruct
    in_specs = [row(D_MODEL), row(A_Q), row(B_G)]
    args = [x2d, oa, ob]
    if sample:
        in_specs += [row(D_FF), row(D_FF)]
        args += list(conv_fill)
    in_specs += [_const_spec(w.shape) for w in weights]
    args += list(weights)
    scratch = [pltpu.VMEM((tm, D_MODEL), BF16), pltpu.VMEM((tm, D_MODEL), BF16),
               pltpu.VMEM((tm, D_MODEL), F32), pltpu.VMEM((tm, D_MODEL), BF16),
               pltpu.VMEM((tm, D_FF), BF16), pltpu.VMEM((tm + _CARRY, _FF_CHUNK), F32)]
    if sample:
        out_shape = [sds((m, D_MODEL), F32), sds((m, D_FF), F32)]
        out_specs = [row(D_MODEL), row(D_FF)]
    else:
        out_shape = [sds((m, D_MODEL), F32), sds((n_tiles, _CARRY, D_FF), F32)]
        out_specs = [row(D_MODEL), pl.BlockSpec((1, _CARRY, D_FF), lambda i: (i, 0, 0))]
        scratch.append(pltpu.VMEM((_CARRY, D_FF), F32))
    return pl.pallas_call(
        functools.partial(_ffn_kernel, tm=tm, sample=sample, tiles_per_seq=tiles_per_seq),
        grid=(n_tiles,),
        in_specs=in_specs,
        out_specs=out_specs,
        out_shape=out_shape,
        scratch_shapes=scratch,
        compiler_params=pltpu.CompilerParams(
            dimension_semantics=("arbitrary",), vmem_limit_bytes=VMEM_LIMIT),
        name=name,
    )(*args)


def _rope_tables(pos):
    half = HEAD_DIM // 2
    inv = ROPE_THETA ** (-jnp.arange(half, dtype=F32) / half)
    ang = pos.astype(F32)[:, None] * inv[None, :]
    cos, sin = jnp.cos(ang), jnp.sin(ang)
    reps = LANES // HEAD_DIM
    cos_t = jnp.tile(jnp.concatenate([cos, cos], axis=1), (1, reps))
    sin_t = jnp.tile(jnp.concatenate([-sin, sin], axis=1), (1, reps))
    return cos_t, sin_t


def _prep_qkv_weight(w_in):
    qa = w_in[:, 0:A_Q]
    ka = w_in[:, A_Q:A_Q + A_KV]
    va = w_in[:, A_Q + A_KV:A_Q + 2 * A_KV]
    b0 = A_Q + 2 * A_KV
    qb = w_in[:, b0:b0 + 3 * B_G]
    kb = w_in[:, b0 + 3 * B_G:b0 + 6 * B_G]
    vb = w_in[:, b0 + 6 * B_G:b0 + 9 * B_G]
    cols = []
    for c in range(A_GROUP):
        cols += [qa[:, c * HEAD_DIM:(c + 1) * HEAD_DIM],
                 qa[:, (A_GROUP + c) * HEAD_DIM:(A_GROUP + c + 1) * HEAD_DIM]]
    cols += [ka, va]
    for g in range(len(B_DIL)):
        gs = slice(g * B_G, (g + 1) * B_G)
        cols += [qb[:, gs], kb[:, gs], vb[:, gs]]
    return jnp.concatenate(cols, axis=1).astype(BF16)


def _prep_br_a(w_br_a):
    rows = []
    for c in range(A_GROUP):
        rows += [w_br_a[c * HEAD_DIM:(c + 1) * HEAD_DIM],
                 w_br_a[(A_GROUP + c) * HEAD_DIM:(A_GROUP + c + 1) * HEAD_DIM]]
    return jnp.concatenate(rows, axis=0).astype(BF16)


_TM_PROMPT = 512
_SAMPLE_SEQS_PER_STEP = 4


def kernel(x_prompt, x_sample, cache_a, cache_b1, cache_b2, cache_b3, state_conv, w_in, sink_a, w_br_a, w_br_b, w_o, ln1_g, ln1_b, w_up, conv_w, conv_b, w_down, ln2_g, ln2_b):
    assert DEPTH == 1
    l = 0
    w_qkv = _prep_qkv_weight(w_in[l])
    w_gate = w_in[l][:, N_QKV:].astype(BF16)
    ffn_weights = (w_gate, _prep_br_a(w_br_a[l]), w_br_b[l].astype(BF16), w_o[l].astype(BF16),
                   ln1_g[l][None], ln1_b[l][None], w_up[l].astype(BF16), conv_w[l], conv_b[l][None],
                   w_down[l].astype(BF16), ln2_g[l][None], ln2_b[l][None])
    sink = sink_a[l].astype(F32)

    mp = BATCH * SEQ
    xp = x_prompt.reshape(mp, D_MODEL)
    cos_p, sin_p = _rope_tables(jnp.arange(SEQ, dtype=jnp.int32))
    tps = SEQ // _TM_PROMPT
    (qa, ca_p, kva_bf, qb1, cb1_p, kvb1_bf, qb2, cb2_p, kvb2_bf, qb3, cb3_p, kvb3_bf) = _qkv_call(
        xp, w_qkv, cos_p, sin_p, tm=_TM_PROMPT, prompt=True, tiles_per_seq=tps, name="qkv_prompt")
    oa = _attn_a_call(sink, qa, kva_bf)
    ob = _attn_b_call(qb1, kvb1_bf, qb2, kvb2_bf, qb3, kvb3_bf)
    y_p, ulast = _ffn_call(xp, oa, ob, ffn_weights, tm=_TM_PROMPT, sample=False, tiles_per_seq=tps,
                           name="ffn_prompt")

    def prompt_cache(c, heads):
        return c.reshape(BATCH, 2, heads, HEAD_DIM, c.shape[-1]).transpose(0, 4, 1, 2, 3)[None]

    y_prompt = y_p.reshape(BATCH, SEQ, D_MODEL)
    cache_a_prompt = prompt_cache(ca_p, A_KV_HEADS)
    kvb_p = [prompt_cache(c, B_SLOTS) for c in (cb1_p, cb2_p, cb3_p)]
    state_conv_prompt = ulast.reshape(BATCH, tps, _CARRY, D_FF)[None, :, tps - 1, _CARRY - (CONV_W - 1):]

    ms = DEC_BATCH * DEC_SEQ
    xs = x_sample.reshape(ms, D_MODEL)
    cos_s, sin_s = _rope_tables(PAST_LEN + (jnp.arange(ms, dtype=jnp.int32) % DEC_SEQ))
    (qa_s, ca_s, kva_s, qb1_s, cb1_s, kvb1_s, qb2_s, cb2_s, kvb2_s, qb3_s, cb3_s, kvb3_s) = _qkv_call(
        xs, w_qkv, cos_s, sin_s, tm=ms, prompt=False, tiles_per_seq=1, name="qkv_sample")

    def window_buffer(c):
        return c.transpose(0, 2, 3, 4, 1).reshape(c.shape[0], -1, c.shape[1])

    ca = window_buffer(cache_a[l])
    cb = [window_buffer(c[l]) for c in (cache_b1, cache_b2, cache_b3)]
    oa_s, ob_s = _sample_attn_call(sink, qa_s, kva_s, ca, qb1_s, kvb1_s, cb[0], qb2_s, kvb2_s, cb[1],
                                   qb3_s, kvb3_s, cb[2], nb=_SAMPLE_SEQS_PER_STEP)
    sc = state_conv[l]
    zeros = lambda k: jnp.zeros((DEC_BATCH, k, D_FF), F32)
    e1 = jnp.concatenate([sc[:, 1:2], zeros(DEC_SEQ - 1)], axis=1).reshape(ms, D_FF)
    e2 = jnp.concatenate([sc, zeros(DEC_SEQ - 2)], axis=1).reshape(ms, D_FF)
    y_s, u_s = _ffn_call(xs, oa_s, ob_s, ffn_weights, tm=ms, sample=True, tiles_per_seq=1,
                         conv_fill=(e1, e2), name="ffn_sample")

    def sample_cache(c, heads):
        return c.reshape(DEC_SEQ, 2, heads, HEAD_DIM, DEC_BATCH).transpose(4, 0, 1, 2, 3)[None]

    y_sample = y_s.reshape(DEC_BATCH, DEC_SEQ, D_MODEL)
    cache_a_sample = sample_cache(ca_s, A_KV_HEADS)
    kvb_s = [sample_cache(c, B_SLOTS) for c in (cb1_s, cb2_s, cb3_s)]
    state_conv_sample = u_s.reshape(DEC_BATCH, DEC_SEQ, D_FF)[None, :, DEC_SEQ - (CONV_W - 1):]

    return (y_prompt, y_sample, cache_a_prompt, cache_a_sample, kvb_p[0], kvb_s[0], kvb_p[1], kvb_s[1],
            kvb_p[2], kvb_s[2], state_conv_prompt, state_conv_sample)
```

```python
import functools

import jax
import jax.numpy as jnp
from jax import lax
from jax.experimental import pallas as pl
from jax.experimental.pallas import tpu as pltpu

D_MODEL = 1024
BATCH = 8
SEQ = 2048
DEPTH = 1
DEC_BATCH = 128
DEC_SEQ = 4
PAST_LEN = 16384
HEAD_DIM = 64
A_Q_HEADS = 8
A_KV_HEADS = 2
A_GROUP = A_Q_HEADS // A_KV_HEADS
A_WINDOW = 128
B_DIL = ((128, 1), (512, 4), (2048, 16))
B_SLOTS = 4
BLK = 128
ROPE_THETA = 10000.0
D_FF = ((8 * D_MODEL // 3 + 127) // 128) * 128
CONV_W = 3
ALPHA = (2 * DEPTH) ** 0.25
LN_EPS = 1e-5
NEG = -1e30
SCALE = HEAD_DIM ** -0.5
A_Q = A_Q_HEADS * HEAD_DIM
A_KV = A_KV_HEADS * HEAD_DIM
B_G = B_SLOTS * HEAD_DIM
N_QKV = A_Q + 2 * A_KV + 3 * 3 * B_G

LANES = 128
VMEM_LIMIT = 56 * 1024 * 1024

BF16 = jnp.bfloat16
F32 = jnp.float32

_C_QA = 0
_C_KVA = _C_QA + A_Q
_C_QB = (_C_KVA + 2 * A_KV, _C_KVA + 2 * A_KV + 3 * B_G, _C_KVA + 2 * A_KV + 6 * B_G)
_C_KVB = tuple(c + B_G for c in _C_QB)


def _const_spec(shape):
    nd = len(shape)
    return pl.BlockSpec(shape, lambda *_: (0,) * nd, pipeline_mode=pl.Buffered(1))


def _nt_dot(a, b):
    return lax.dot_general(a, b, (((1,), (1,)), ((), ())), preferred_element_type=F32)


def _dot(a, b):
    return jnp.dot(a, b, preferred_element_type=F32)


def _rope(y, cos, sin_signed, first_half):
    outs = []
    for j in range(y.shape[1] // LANES):
        yj = y[:, j * LANES:(j + 1) * LANES]
        partner = jnp.where(first_half, pltpu.roll(yj, LANES - HEAD_DIM // 2, 1),
                            pltpu.roll(yj, HEAD_DIM // 2, 1))
        outs.append(yj * cos + partner * sin_signed)
    return outs[0] if len(outs) == 1 else jnp.concatenate(outs, axis=1)


def _qkv_kernel(x_ref, w_ref, cos_ref, sin_ref, *refs, tm, prompt, tiles_per_seq):
    out_refs, (xb_ref, y_ref) = refs[:-2], refs[-2:]
    q_refs, cache_refs, kv_refs = out_refs[0::3], out_refs[1::3], out_refs[2::3]
    groups = ((A_WINDOW, 1),) + B_DIL

    xb_ref[...] = x_ref[...].astype(BF16)
    cos = cos_ref[...]
    sin = sin_ref[...]
    lane = lax.broadcasted_iota(jnp.int32, (tm, LANES), 1)
    first_half = (lane & (HEAD_DIM - 1)) < HEAD_DIM // 2
    last_tile = (pl.program_id(0) % tiles_per_seq) == tiles_per_seq - 1

    def proj(c0, width):
        return _dot(xb_ref[...], w_ref[:, c0:c0 + width])

    def to_planes(val):
        n_col = val.shape[1] // LANES
        for c in range(n_col):
            y_ref[c] = val[:, c * LANES:(c + 1) * LANES]
        return n_col

    col_q = (_C_QA,) + _C_QB
    col_kv = (_C_KVA,) + _C_KVB
    for g, (win, dil) in enumerate(groups):
        wq = A_Q if g == 0 else B_G
        wk = A_KV if g == 0 else B_G
        q = _rope(proj(col_q[g], wq), cos, sin, first_half) * SCALE
        kv = proj(col_kv[g], 2 * wk)
        kv = jnp.concatenate([_rope(kv[:, :wk], cos, sin, first_half), kv[:, wk:]], axis=1)
        cache_ref = cache_refs[g]
        if not prompt:
            q_refs[g][...] = q.astype(BF16)
            kv_refs[g][...] = kv
            n_col = to_planes(kv)
            n_seq = tm // DEC_SEQ
            for t in range(DEC_SEQ):
                for c in range(n_col):
                    cache_ref[t, c * LANES:(c + 1) * LANES, :] = y_ref[c, pl.ds(t, n_seq, stride=DEC_SEQ), :].T
            continue

        keep = min(win, SEQ)
        if keep == SEQ:
            cache_ref[0] = kv.T
        else:
            assert keep <= tm

            @pl.when(last_tile)
            def _(kv=kv, cache_ref=cache_ref, keep=keep):
                cache_ref[0] = kv[tm - keep:, :].T

        if dil == 1:
            q_refs[g][...] = q.astype(BF16)
            kv_refs[g][...] = kv.astype(BF16)
        else:
            for val, out_ref in ((q, q_refs[g]), (kv, kv_refs[g])):
                n_col = to_planes(val)
                for r in range(dil):
                    out_ref[0, r] = jnp.concatenate(
                        [y_ref[c, pl.ds(r, tm // dil, stride=dil), :] for c in range(n_col)],
                        axis=1).astype(BF16)


def _qkv_call(x2d, w_qkv, cos_t, sin_t, *, tm, prompt, tiles_per_seq, name):
    m = x2d.shape[0]
    n_tiles = m // tm
    row = lambda w: pl.BlockSpec((tm, w), lambda i: (i, 0))
    tab = pl.BlockSpec((tm, LANES), lambda i: (i % tiles_per_seq, 0))
    in_specs = [row(D_MODEL), _const_spec((D_MODEL, N_QKV)), tab, tab]
    sds = jax.ShapeDtypeStruct
    out_shape, out_specs = [], []
    for g, (win, dil) in enumerate(((A_WINDOW, 1),) + B_DIL):
        wq = A_Q if g == 0 else B_G
        wkv = 2 * (A_KV if g == 0 else B_G)
        if not prompt:
            assert n_tiles == 1
            out_shape += [sds((m, wq), BF16), sds((DEC_SEQ, wkv, m // DEC_SEQ), F32), sds((m, wkv), F32)]
            out_specs += [row(wq), pl.BlockSpec((DEC_SEQ, wkv, m // DEC_SEQ), lambda i: (0, 0, 0)),
                          row(wkv)]
            continue
        n_seq = m // SEQ
        keep = min(win, SEQ)
        if keep == SEQ:
            cache_spec = pl.BlockSpec((1, wkv, tm), lambda i: (i // tiles_per_seq, 0, i % tiles_per_seq))
        else:
            cache_spec = pl.BlockSpec((1, wkv, keep), lambda i: (i // tiles_per_seq, 0, 0))
        if dil == 1:
            out_shape += [sds((m, wq), BF16), sds((n_seq, wkv, keep), F32), sds((m, wkv), BF16)]
            out_specs += [row(wq), cache_spec, row(wkv)]
        else:
            dspec = lambda w, dil=dil: pl.BlockSpec(
                (1, dil, tm // dil, w), lambda i: (i // tiles_per_seq, 0, i % tiles_per_seq, 0))
            out_shape += [sds((n_seq, dil, SEQ // dil, wq), BF16), sds((n_seq, wkv, keep), F32),
                          sds((n_seq, dil, SEQ // dil, wkv), BF16)]
            out_specs += [dspec(wq), cache_spec, dspec(wkv)]
    return pl.pallas_call(
        functools.partial(_qkv_kernel, tm=tm, prompt=prompt, tiles_per_seq=tiles_per_seq),
        grid=(n_tiles,),
        in_specs=in_specs,
        out_specs=out_specs,
        out_shape=out_shape,
        scratch_shapes=[pltpu.VMEM((tm, D_MODEL), BF16),
                        pltpu.VMEM((2 * B_G // LANES, tm, LANES), F32)],
        compiler_params=pltpu.CompilerParams(
            dimension_semantics=("arbitrary",), vmem_limit_bytes=VMEM_LIMIT),
        name=name,
    )(x2d, w_qkv, cos_t, sin_t)


def _fold_masks():
    row = lax.broadcasted_iota(jnp.int32, (BLK, BLK), 0)
    col = lax.broadcasted_iota(jnp.int32, (BLK, BLK), 1)
    return col > row


def _attn_scratch(heads):
    return [pltpu.VMEM((2, heads * BLK, 2 * BLK), F32), pltpu.VMEM((2, heads * BLK, 2 * BLK), BF16),
            pltpu.VMEM((2, heads * BLK, LANES), F32), pltpu.VMEM((2, heads * BLK, LANES), F32)]


def _attn_pipeline(n_units, bufs, *, heads, load_q, load_k, load_v, has_prev, sink_of_head, finish,
                   mxu_row_sum=False):
    sbuf, pbuf, mbuf, lbuf = bufs
    upper = _fold_masks()
    upper_bf = jnp.where(upper, 1.0, 0.0).astype(BF16)
    width = BLK if has_prev is None else 2 * BLK
    assert n_units % 2 == 0

    sbuf[1] = jnp.zeros(sbuf.shape[1:], F32)
    pbuf[0] = jnp.zeros(pbuf.shape[1:], BF16)
    mbuf[0] = jnp.zeros(mbuf.shape[1:], F32)
    lbuf[0] = jnp.ones(lbuf.shape[1:], F32)

    def scores(u, par):
        sbuf[par, :, 0:width] = _nt_dot(load_q(u), load_k(u))

    def softmax(u, par):
        prev_ok = None if has_prev is None else has_prev(u)
        for h in range(heads):
            rows = slice(h * BLK, (h + 1) * BLK)
            if has_prev is None:
                sf = jnp.where(upper, NEG, sbuf[par, rows, 0:BLK])
            else:
                sf = jnp.where(upper, jnp.where(prev_ok, sbuf[par, rows, 0:BLK], NEG),
                               sbuf[par, rows, BLK:2 * BLK])
            mx = jnp.max(sf, axis=-1, keepdims=True)
            if sink_of_head is not None:
                mx = jnp.maximum(mx, sink_of_head(h))
            e = jnp.exp(sf - mx)
            if not mxu_row_sum:
                lbuf[par, rows, :] = jnp.broadcast_to(jnp.sum(e, axis=-1, keepdims=True), (BLK, LANES))
            e = e.astype(BF16)
            if has_prev is None:
                pbuf[par, rows, 0:BLK] = e
            else:
                p_prev = e * upper_bf
                pbuf[par, rows, 0:BLK] = p_prev
                pbuf[par, rows, BLK:2 * BLK] = e - p_prev
            mbuf[par, rows, :] = jnp.broadcast_to(mx, (BLK, LANES))

    def values(u, par):
        finish(u, _dot(pbuf[par, :, 0:width], load_v(u)), mbuf[par], lbuf[par])

    last = n_units - 1

    def trip_pair(t, carry):
        for par in (0, 1):
            i = 2 * t + par
            scores(jnp.minimum(i, last), par)
            softmax(jnp.clip(i - 1, 0, last), 1 - par)
            values(jnp.clip(i - 2, 0, last), par)
        return carry

    lax.fori_loop(0, n_units // 2 + 1, trip_pair, 0)


def _attn_a_kernel(sink_ref, q_ref, kv_ref, o_ref, *bufs):
    n_blk = q_ref.shape[0] // BLK
    lane = lax.broadcasted_iota(jnp.int32, (BLK, LANES), 1)
    hi = lane >= HEAD_DIM
    mask_bf = (jnp.where(hi, 0.0, 1.0).astype(BF16), jnp.where(hi, 1.0, 0.0).astype(BF16))

    def rows_of(b):
        return pl.ds(pl.multiple_of(b * BLK, BLK), BLK)

    def prev_cur(b, cols):
        return jnp.concatenate([kv_ref[rows_of(jnp.maximum(b - 1, 0)), cols], kv_ref[rows_of(b), cols]],
                               axis=0)

    def load_q(b):
        qblk = q_ref[rows_of(b), :]
        return jnp.concatenate([qblk[:, g * LANES:(g + 1) * LANES] * mask_bf[j]
                                for j in range(A_KV_HEADS) for g in range(A_GROUP)], axis=0)

    ones = jnp.ones((2 * BLK, LANES), BF16)

    def load_v(b):
        return jnp.concatenate([prev_cur(b, slice(A_KV, 2 * A_KV)), ones], axis=1)

    def finish(b, o, m, _):
        def normalised(h):
            rows = slice(h * BLK, (h + 1) * BLK)
            den = o[rows, LANES:] + jnp.exp(sink_ref[h] - m[rows])
            return o[rows, :LANES] / den

        for g in range(A_GROUP):
            o_ref[rows_of(b), g * LANES:(g + 1) * LANES] = jnp.where(
                hi, normalised(A_GROUP + g), normalised(g)).astype(BF16)

    _attn_pipeline(n_blk, bufs, heads=A_Q_HEADS, load_q=load_q,
                   load_k=lambda b: prev_cur(b, slice(0, A_KV)), load_v=load_v,
                   has_prev=lambda b: b > 0, sink_of_head=lambda h: sink_ref[h], finish=finish,
                   mxu_row_sum=True)


def _attn_a_call(sink, q_a, kva_bf):
    n_seq = q_a.shape[0] // SEQ
    return pl.pallas_call(
        _attn_a_kernel,
        grid=(n_seq,),
        in_specs=[pl.BlockSpec(memory_space=pltpu.SMEM),
                  pl.BlockSpec((SEQ, A_Q), lambda n: (n, 0)),
                  pl.BlockSpec((SEQ, 2 * A_KV), lambda n: (n, 0))],
        out_specs=pl.BlockSpec((SEQ, A_Q), lambda n: (n, 0)),
        out_shape=jax.ShapeDtypeStruct(q_a.shape, BF16),
        scratch_shapes=_attn_scratch(A_Q_HEADS),
        compiler_params=pltpu.CompilerParams(
            dimension_semantics=("arbitrary",), vmem_limit_bytes=VMEM_LIMIT),
        name="attn_a_prompt",
    )(sink, q_a, kva_bf)


def _slot_masks(rows):
    lane = lax.broadcasted_iota(jnp.int32, (rows, B_G), 1)
    slot = lane >> 6
    masks = [slot == s for s in range(B_SLOTS)]
    masks_bf = [jnp.where(mk, 1.0, 0.0).astype(BF16) for mk in masks]
    return masks, masks_bf


def _attn_b_kernel(q1_ref, kv1_ref, q2_ref, kv2_ref, q3_ref, kv3_ref, o_ref,
                   acc2_ref, m2_ref, l2_ref, acc3_ref, m3_ref, l3_ref, *bufs):
    masks, masks_bf = _slot_masks(BLK)
    d2, d3 = B_DIL[1][1], B_DIL[2][1]
    nb2 = SEQ // d2 // BLK
    assert SEQ // d3 == BLK
    n_plane = B_G // LANES

    def put(ref, rows, val):
        for c in range(n_plane):
            ref[c, rows, :] = val[:, c * LANES:(c + 1) * LANES]

    def get(ref, rows):
        return jnp.concatenate([ref[c, rows, :] for c in range(n_plane)], axis=1)

    def stack_q(qblk):
        return jnp.concatenate([qblk * masks_bf[s] for s in range(B_SLOTS)], axis=0)

    def unstack(o, m, l):
        acc = jnp.where(masks[0], o[0:BLK], 0.0)
        rep = lambda x, s: jnp.concatenate([x[s * BLK:(s + 1) * BLK]] * n_plane, axis=1)
        mf, lf = rep(m, 0), rep(l, 0)
        for s in range(1, B_SLOTS):
            acc = jnp.where(masks[s], o[s * BLK:(s + 1) * BLK], acc)
            mf = jnp.where(masks[s], rep(m, s), mf)
            lf = jnp.where(masks[s], rep(l, s), lf)
        return acc, mf, lf

    def rows_of(b):
        return pl.ds(pl.multiple_of(b * BLK, BLK), BLK)

    run = functools.partial(_attn_pipeline, bufs=bufs, heads=B_SLOTS, sink_of_head=None)

    def finish3(r, o, m, l):
        rows = pl.ds(r, BLK, stride=d3)
        for ref, val in zip((acc3_ref, m3_ref, l3_ref), unstack(o, m, l)):
            put(ref, rows, val)

    run(d3, load_q=lambda r: stack_q(q3_ref[r]), load_k=lambda r: kv3_ref[r, :, 0:B_G],
        load_v=lambda r: kv3_ref[r, :, B_G:2 * B_G], has_prev=None, finish=finish3)

    def prev_cur2(u, cols):
        r, b = u // nb2, u % nb2
        return jnp.concatenate([kv2_ref[r, rows_of(jnp.maximum(b - 1, 0)), cols], kv2_ref[r, rows_of(b), cols]],
                               axis=0)

    def finish2(u, o, m, l):
        r, b = u // nb2, u % nb2
        rows = pl.ds(r + b * (BLK * d2), BLK, stride=d2)
        for ref, val in zip((acc2_ref, m2_ref, l2_ref), unstack(o, m, l)):
            put(ref, rows, val)

    run(d2 * nb2, load_q=lambda u: stack_q(q2_ref[u // nb2, rows_of(u % nb2), :]),
        load_k=lambda u: prev_cur2(u, slice(0, B_G)), load_v=lambda u: prev_cur2(u, slice(B_G, 2 * B_G)),
        has_prev=lambda u: (u % nb2) > 0, finish=finish2)

    def prev_cur1(b, cols):
        return jnp.concatenate([kv1_ref[rows_of(jnp.maximum(b - 1, 0)), cols], kv1_ref[rows_of(b), cols]],
                               axis=0)

    def finish1(b, o, m, l):
        acc1, m1, l1 = unstack(o, m, l)
        rows = rows_of(b)
        m2, m3 = get(m2_ref, rows), get(m3_ref, rows)
        mx = jnp.maximum(jnp.maximum(m1, m2), m3)
        w1, w2, w3 = jnp.exp(m1 - mx), jnp.exp(m2 - mx), jnp.exp(m3 - mx)
        num = w1 * acc1 + w2 * get(acc2_ref, rows) + w3 * get(acc3_ref, rows)
        den = w1 * l1 + w2 * get(l2_ref, rows) + w3 * get(l3_ref, rows)
        o_ref[rows, :] = (num / den).astype(BF16)

    run(SEQ // BLK, load_q=lambda b: stack_q(q1_ref[rows_of(b), :]),
        load_k=lambda b: prev_cur1(b, slice(0, B_G)), load_v=lambda b: prev_cur1(b, slice(B_G, 2 * B_G)),
        has_prev=lambda b: b > 0, finish=finish1)


def _attn_b_call(q1, kv1, q2, kv2, q3, kv3):
    n_seq = q1.shape[0] // SEQ
    d2, d3 = B_DIL[1][1], B_DIL[2][1]
    rows = lambda w: pl.BlockSpec((SEQ, w), lambda n: (n, 0))
    dsp = lambda d, w: pl.BlockSpec((None, d, SEQ // d, w), lambda n: (n, 0, 0, 0))
    return pl.pallas_call(
        _attn_b_kernel,
        grid=(n_seq,),
        in_specs=[rows(B_G), rows(2 * B_G), dsp(d2, B_G), dsp(d2, 2 * B_G), dsp(d3, B_G),
                  dsp(d3, 2 * B_G)],
        out_specs=rows(B_G),
        out_shape=jax.ShapeDtypeStruct((q1.shape[0], B_G), BF16),
        scratch_shapes=[pltpu.VMEM((B_G // LANES, SEQ, LANES), F32) for _ in range(6)]
        + _attn_scratch(B_SLOTS),
        compiler_params=pltpu.CompilerParams(
            dimension_semantics=("arbitrary",), vmem_limit_bytes=VMEM_LIMIT),
        name="attn_b_prompt",
    )(q1, kv1, q2, kv2, q3, kv3)


def _sample_attn_kernel(sink_ref, qa_ref, kna_ref, ca_ref, q1_ref, kn1_ref, c1_ref,
                        q2_ref, kn2_ref, c2_ref, q3_ref, kn3_ref, c3_ref, oa_ref, ob_ref, *, nb):
    tt = nb * DEC_SEQ
    pad = BLK - tt
    tt_shift, seq_shift = tt.bit_length() - 1, DEC_SEQ.bit_length() - 1
    assert tt == 1 << tt_shift and DEC_SEQ == 1 << seq_shift

    def pad_rows(x):
        return jnp.concatenate([x.astype(BF16), jnp.zeros((pad, x.shape[1]), BF16)], axis=0)

    def window_group(qm, kn_ref, c_ref, dil, sinkv, finish):
        rows, kd = qm.shape
        lc = c_ref.shape[2]
        kn = kn_ref[...]
        knpad = pad_rows(kn[:, :kd])
        vnpad = pad_rows(kn[:, kd:])
        s_new = _nt_dot(qm, knpad)
        def masks(width):
            col = lax.broadcasted_iota(jnp.int32, (rows, width), 1)
            t = lax.broadcasted_iota(jnp.int32, (rows, width), 0) & (DEC_SEQ - 1)
            same_res = (col & (dil - 1)) == (t & (dil - 1))
            return (col > t) & same_res, (col <= t) & same_res

        valid, _ = masks(lc)
        _, is_new = masks(BLK)
        for n in range(nb):
            k_t = c_ref[n, 0:kd, :].astype(BF16)
            v_t = c_ref[n, kd:2 * kd, :].astype(BF16)
            s_c = jnp.where(valid, _dot(qm, k_t), NEG)
            s_n = pltpu.roll(s_new, BLK - DEC_SEQ * n, 1) if n else s_new
            first = jnp.where(is_new, s_n, s_c[:, :BLK])
            s = first if lc == BLK else jnp.concatenate([first, s_c[:, BLK:]], axis=1)
            mx = jnp.max(s, axis=-1, keepdims=True)
            if sinkv is not None:
                mx = jnp.maximum(mx, sinkv)
            e = jnp.exp(s - mx)
            l = jnp.sum(e, axis=-1, keepdims=True)
            if sinkv is not None:
                l = l + jnp.exp(sinkv - mx)
            p_new = jnp.where(is_new, e[:, :BLK], 0.0)
            if n:
                p_new = pltpu.roll(p_new, DEC_SEQ * n, 1)
            o = _nt_dot(jnp.where(valid, e, 0.0).astype(BF16), v_t) + _dot(p_new.astype(BF16), vnpad)
            finish(n, o, mx, l)

    rows_a = A_Q_HEADS * tt
    lane_t = lax.broadcasted_iota(jnp.int32, (tt, LANES), 1)
    hi_t = lane_t >= HEAD_DIM
    mask_bf = (jnp.where(hi_t, 0.0, 1.0).astype(BF16), jnp.where(hi_t, 1.0, 0.0).astype(BF16))
    qa = qa_ref[...]
    qm = jnp.concatenate([qa[:, g * LANES:(g + 1) * LANES] * mask_bf[j]
                          for j in range(A_KV_HEADS) for g in range(A_GROUP)], axis=0)
    head = lax.broadcasted_iota(jnp.int32, (rows_a, 1), 0) >> tt_shift
    sinkv = jnp.zeros((rows_a, 1), F32)
    for h in range(A_Q_HEADS):
        sinkv = jnp.where(head == h, sink_ref[h], sinkv)
    seq_t = lax.broadcasted_iota(jnp.int32, (tt, LANES), 0) >> seq_shift
    out_a = [jnp.zeros((tt, LANES), F32) for _ in range(A_GROUP)]

    def finish_a(n, o, mx, l):
        o = o / l
        for g in range(A_GROUP):
            og = jnp.where(hi_t, o[(A_GROUP + g) * tt:(A_GROUP + g + 1) * tt], o[g * tt:(g + 1) * tt])
            out_a[g] = jnp.where(seq_t == n, og, out_a[g])

    window_group(qm, kna_ref, ca_ref, 1, sinkv, finish_a)
    for g in range(A_GROUP):
        oa_ref[:, g * LANES:(g + 1) * LANES] = out_a[g].astype(BF16)

    masks_t, masks_bf_t = _slot_masks(tt)
    tok_t = lax.broadcasted_iota(jnp.int32, (tt, B_G), 0)

    def stack_q(q):
        return jnp.concatenate([q * masks_bf_t[s] for s in range(B_SLOTS)], axis=0)

    def unstack(o, mx, l):
        acc = jnp.where(masks_t[0], o[0:tt], 0.0)
        mf = jnp.broadcast_to(mx[0:tt], (tt, B_G))
        lf = jnp.broadcast_to(l[0:tt], (tt, B_G))
        for s in range(1, B_SLOTS):
            acc = jnp.where(masks_t[s], o[s * tt:(s + 1) * tt], acc)
            mf = jnp.where(masks_t[s], mx[s * tt:(s + 1) * tt], mf)
            lf = jnp.where(masks_t[s], l[s * tt:(s + 1) * tt], lf)
        return acc, mf, lf

    zero = jnp.zeros((tt, B_G), F32)
    stats = []
    for (win, dil), q_ref, kn_ref, c_ref in zip(B_DIL, (q1_ref, q2_ref, q3_ref),
                                                (kn1_ref, kn2_ref, kn3_ref), (c1_ref, c2_ref, c3_ref)):
        assert c_ref.shape[2] == win
        group = [zero, zero, zero]

        def finish_b(n, o, mx, l, group=group):
            sel = (tok_t >> seq_shift) == n
            for k, new in enumerate(unstack(o, mx, l)):
                group[k] = jnp.where(sel, new, group[k])

        window_group(stack_q(q_ref[...]), kn_ref, c_ref, dil, None, finish_b)
        stats.append(group)

    (acc1, m1, l1), (acc2, m2, l2), (acc3, m3, l3) = stats
    mx = jnp.maximum(jnp.maximum(m1, m2), m3)
    w1, w2, w3 = jnp.exp(m1 - mx), jnp.exp(m2 - mx), jnp.exp(m3 - mx)
    ob_ref[...] = ((w1 * acc1 + w2 * acc2 + w3 * acc3) / (w1 * l1 + w2 * l2 + w3 * l3)).astype(BF16)


def _sample_attn_call(sink, qa, kna, cache_a, q1, kn1, cache_b1, q2, kn2, cache_b2, q3, kn3, cache_b3,
                      *, nb):
    n_seq = cache_a.shape[0]
    tt = nb * DEC_SEQ
    kv_w = 2 * B_G
    tok = lambda w: pl.BlockSpec((tt, w), lambda i: (i, 0))
    cache = lambda c: pl.BlockSpec((nb,) + c.shape[1:], lambda i: (i, 0, 0))
    return pl.pallas_call(
        functools.partial(_sample_attn_kernel, nb=nb),
        grid=(n_seq // nb,),
        in_specs=[pl.BlockSpec(memory_space=pltpu.SMEM),
                  tok(A_Q), tok(2 * A_KV), cache(cache_a),
                  tok(B_G), tok(kv_w), cache(cache_b1),
                  tok(B_G), tok(kv_w), cache(cache_b2),
                  tok(B_G), tok(kv_w), cache(cache_b3)],
        out_specs=[tok(A_Q), tok(B_G)],
        out_shape=[jax.ShapeDtypeStruct((n_seq * DEC_SEQ, A_Q), BF16),
                   jax.ShapeDtypeStruct((n_seq * DEC_SEQ, B_G), BF16)],
        compiler_params=pltpu.CompilerParams(
            dimension_semantics=("arbitrary",), vmem_limit_bytes=VMEM_LIMIT),
        name="attn_sample",
    )(sink, qa, kna, cache_a, q1, kn1, cache_b1, q2, kn2, cache_b2, q3, kn3, cache_b3)


def _layernorm(x, g, b):
    mu = jnp.mean(x, axis=-1, keepdims=True)
    xc = x - mu
    var = jnp.mean(xc * xc, axis=-1, keepdims=True)
    return xc * lax.rsqrt(var + LN_EPS) * g + b


def _gelu_exact(x):
    return 0.5 * x * (1.0 + lax.erf(x * (0.5 ** 0.5)))


_FF_CHUNK = 256
_CARRY = 8


def _ffn_kernel(*refs, tm, sample, tiles_per_seq):
    if sample:
        (x_ref, oa_ref, ob_ref, e1_ref, e2_ref, wg_ref, wa_ref, wb_ref, wo_ref, g1_ref, b1_ref,
         wup_ref, cw_ref, cb_ref, wdn_ref, g2_ref, b2_ref, y_ref, u_ref,
         xb_ref, m_ref, h_ref, hb_ref, gg_ref, ext_ref) = refs
    else:
        (x_ref, oa_ref, ob_ref, wg_ref, wa_ref, wb_ref, wo_ref, g1_ref, b1_ref,
         wup_ref, cw_ref, cb_ref, wdn_ref, g2_ref, b2_ref, y_ref, ulast_ref,
         xb_ref, m_ref, h_ref, hb_ref, gg_ref, ext_ref, carry_ref) = refs

    x = x_ref[...]
    xb_ref[...] = x.astype(BF16)
    half = D_MODEL // 2
    for c in range(2):
        cs = slice(c * half, (c + 1) * half)
        ga = _dot(xb_ref[...], wg_ref[:, c * half:(c + 1) * half])
        gb = _dot(xb_ref[...], wg_ref[:, D_MODEL + c * half:D_MODEL + (c + 1) * half])
        ta = _dot(oa_ref[...], wa_ref[:, cs])
        tb = _dot(ob_ref[...], wb_ref[:, cs])
        m_ref[:, cs] = (jax.nn.sigmoid(ga) * ta + jax.nn.sigmoid(gb) * tb).astype(BF16)
    mix = _dot(m_ref[...], wo_ref[...])
    h = _layernorm(ALPHA * x + mix, g1_ref[...], b1_ref[...])
    h_ref[...] = h
    hb_ref[...] = h.astype(BF16)

    if sample:
        ext_ref[0:_CARRY, :] = jnp.zeros((_CARRY, _FF_CHUNK), F32)
        t = lax.broadcasted_iota(jnp.int32, (tm, _FF_CHUNK), 0) & (DEC_SEQ - 1)
    else:
        @pl.when(pl.program_id(0) % tiles_per_seq == 0)
        def _():
            carry_ref[...] = jnp.zeros_like(carry_ref)

    for c in range(D_FF // _FF_CHUNK):
        cs = slice(c * _FF_CHUNK, (c + 1) * _FF_CHUNK)
        u = _dot(hb_ref[...], wup_ref[:, c * _FF_CHUNK:(c + 1) * _FF_CHUNK])
        v = _dot(hb_ref[...], wup_ref[:, D_FF + c * _FF_CHUNK:D_FF + (c + 1) * _FF_CHUNK])
        if not sample:
            ext_ref[0:_CARRY, :] = carry_ref[:, cs]
        ext_ref[_CARRY:_CARRY + tm, :] = u
        u1 = ext_ref[_CARRY - 1:_CARRY - 1 + tm, :]
        u2 = ext_ref[_CARRY - 2:_CARRY - 2 + tm, :]
        if sample:
            u1 = jnp.where(t >= 1, u1, e1_ref[:, cs])
            u2 = jnp.where(t >= 2, u2, e2_ref[:, cs])
            u_ref[:, cs] = u
        else:
            tail = u[tm - _CARRY:tm, :]
            carry_ref[:, cs] = tail
            ulast_ref[0, :, cs] = tail
        a = cb_ref[:, cs] + cw_ref[0:1, cs] * u2 + cw_ref[1:2, cs] * u1 + cw_ref[2:3, cs] * u
        gg_ref[:, cs] = (_gelu_exact(a) * v).astype(BF16)
    f = _dot(gg_ref[...], wdn_ref[...])
    y_ref[...] = _layernorm(ALPHA * h_ref[...] + f, g2_ref[...], b2_ref[...])


def _ffn_call(x2d, oa, ob, weights, *, tm, sample, tiles_per_seq, conv_fill=None, name):
    m = x2d.shape[0]
    n_tiles = m // tm
    row = lambda w: pl.BlockSpec((tm, w), lambda i: (i, 0))
    sds = jax.ShapeDtypeStruct
    in_specs = [row(D_MODEL), row(A_Q), row(B_G)]
    args = [x2d, oa, ob]
    if sample:
        in_specs += [row(D_FF), row(D_FF)]
        args += list(conv_fill)
    in_specs += [_const_spec(w.shape) for w in weights]
    args += list(weights)
    scratch = [pltpu.VMEM((tm, D_MODEL), BF16), pltpu.VMEM((tm, D_MODEL), BF16),
               pltpu.VMEM((tm, D_MODEL), F32), pltpu.VMEM((tm, D_MODEL), BF16),
               pltpu.VMEM((tm, D_FF), BF16), pltpu.VMEM((tm + _CARRY, _FF_CHUNK), F32)]
    if sample:
        out_shape = [sds((m, D_MODEL), F32), sds((m, D_FF), F32)]
        out_specs = [row(D_MODEL), row(D_FF)]
    else:
        out_shape = [sds((m, D_MODEL), F32), sds((n_tiles, _CARRY, D_FF), F32)]
        out_specs = [row(D_MODEL), pl.BlockSpec((1, _CARRY, D_FF), lambda i: (i, 0, 0))]
        scratch.append(pltpu.VMEM((_CARRY, D_FF), F32))
    return pl.pallas_call(
        functools.partial(_ffn_kernel, tm=tm, sample=sample, tiles_per_seq=tiles_per_seq),
        grid=(n_tiles,),
        in_specs=in_specs,
        out_specs=out_specs,
        out_shape=out_shape,
        scratch_shapes=scratch,
        compiler_params=pltpu.CompilerParams(
            dimension_semantics=("arbitrary",), vmem_limit_bytes=VMEM_LIMIT),
        name=name,
    )(*args)


def _rope_tables(pos):
    half = HEAD_DIM // 2
    inv = ROPE_THETA ** (-jnp.arange(half, dtype=F32) / half)
    ang = pos.astype(F32)[:, None] * inv[None, :]
    cos, sin = jnp.cos(ang), jnp.sin(ang)
    reps = LANES // HEAD_DIM
    cos_t = jnp.tile(jnp.concatenate([cos, cos], axis=1), (1, reps))
    sin_t = jnp.tile(jnp.concatenate([-sin, sin], axis=1), (1, reps))
    return cos_t, sin_t


def _prep_qkv_weight(w_in):
    qa = w_in[:, 0:A_Q]
    ka = w_in[:, A_Q:A_Q + A_KV]
    va = w_in[:, A_Q + A_KV:A_Q + 2 * A_KV]
    b0 = A_Q + 2 * A_KV
    qb = w_in[:, b0:b0 + 3 * B_G]
    kb = w_in[:, b0 + 3 * B_G:b0 + 6 * B_G]
    vb = w_in[:, b0 + 6 * B_G:b0 + 9 * B_G]
    cols = []
    for c in range(A_GROUP):
        cols += [qa[:, c * HEAD_DIM:(c + 1) * HEAD_DIM],
                 qa[:, (A_GROUP + c) * HEAD_DIM:(A_GROUP + c + 1) * HEAD_DIM]]
    cols += [ka, va]
    for g in range(len(B_DIL)):
        gs = slice(g * B_G, (g + 1) * B_G)
        cols += [qb[:, gs], kb[:, gs], vb[:, gs]]
    return jnp.concatenate(cols, axis=1).astype(BF16)


def _prep_br_a(w_br_a):
    rows = []
    for c in range(A_GROUP):
        rows += [w_br_a[c * HEAD_DIM:(c + 1) * HEAD_DIM],
                 w_br_a[(A_GROUP + c) * HEAD_DIM:(A_GROUP + c + 1) * HEAD_DIM]]
    return jnp.concatenate(rows, axis=0).astype(BF16)


_TM_PROMPT = 512
_SAMPLE_SEQS_PER_STEP = 4


def kernel(x_prompt, x_sample, cache_a, cache_b1, cache_b2, cache_b3, state_conv, w_in, sink_a, w_br_a, w_br_b, w_o, ln1_g, ln1_b, w_up, conv_w, conv_b, w_down, ln2_g, ln2_b):
    assert DEPTH == 1
    l = 0
    w_qkv = _prep_qkv_weight(w_in[l])
    w_gate = w_in[l][:, N_QKV:].astype(BF16)
    ffn_weights = (w_gate, _prep_br_a(w_br_a[l]), w_br_b[l].astype(BF16), w_o[l].astype(BF16),
                   ln1_g[l][None], ln1_b[l][None], w_up[l].astype(BF16), conv_w[l], conv_b[l][None],
                   w_down[l].astype(BF16), ln2_g[l][None], ln2_b[l][None])
    sink = sink_a[l].astype(F32)

    mp = BATCH * SEQ
    xp = x_prompt.reshape(mp, D_MODEL)
    cos_p, sin_p = _rope_tables(jnp.arange(SEQ, dtype=jnp.int32))
    tps = SEQ // _TM_PROMPT
    (qa, ca_p, kva_bf, qb1, cb1_p, kvb1_bf, qb2, cb2_p, kvb2_bf, qb3, cb3_p, kvb3_bf) = _qkv_call(
        xp, w_qkv, cos_p, sin_p, tm=_TM_PROMPT, prompt=True, tiles_per_seq=tps, name="qkv_prompt")
    oa = _attn_a_call(sink, qa, kva_bf)
    ob = _attn_b_call(qb1, kvb1_bf, qb2, kvb2_bf, qb3, kvb3_bf)
    y_p, ulast = _ffn_call(xp, oa, ob, ffn_weights, tm=_TM_PROMPT, sample=False, tiles_per_seq=tps,
                           name="ffn_prompt")

    def prompt_cache(c, heads):
        return c.reshape(BATCH, 2, heads, HEAD_DIM, c.shape[-1]).transpose(0, 4, 1, 2, 3)[None]

    y_prompt = y_p.reshape(BATCH, SEQ, D_MODEL)
    cache_a_prompt = prompt_cache(ca_p, A_KV_HEADS)
    kvb_p = [prompt_cache(c, B_SLOTS) for c in (cb1_p, cb2_p, cb3_p)]
    state_conv_prompt = ulast.reshape(BATCH, tps, _CARRY, D_FF)[None, :, tps - 1, _CARRY - (CONV_W - 1):]

    ms = DEC_BATCH * DEC_SEQ
    xs = x_sample.reshape(ms, D_MODEL)
    cos_s, sin_s = _rope_tables(PAST_LEN + (jnp.arange(ms, dtype=jnp.int32) % DEC_SEQ))
    (qa_s, ca_s, kva_s, qb1_s, cb1_s, kvb1_s, qb2_s, cb2_s, kvb2_s, qb3_s, cb3_s, kvb3_s) = _qkv_call(
        xs, w_qkv, cos_s, sin_s, tm=ms, prompt=False, tiles_per_seq=1, name="qkv_sample")

    def window_buffer(c):
        return c.transpose(0, 2, 3, 4, 1).reshape(c.shape[0], -1, c.shape[1])

    ca = window_buffer(cache_a[l])
    cb = [window_buffer(c[l]) for c in (cache_b1, cache_b2, cache_b3)]
    oa_s, ob_s = _sample_attn_call(sink, qa_s, kva_s, ca, qb1_s, kvb1_s, cb[0], qb2_s, kvb2_s, cb[1],
                                   qb3_s, kvb3_s, cb[2], nb=_SAMPLE_SEQS_PER_STEP)
    sc = state_conv[l]
    zeros = lambda k: jnp.zeros((DEC_BATCH, k, D_FF), F32)
    e1 = jnp.concatenate([sc[:, 1:2], zeros(DEC_SEQ - 1)], axis=1).reshape(ms, D_FF)
    e2 = jnp.concatenate([sc, zeros(DEC_SEQ - 2)], axis=1).reshape(ms, D_FF)
    y_s, u_s = _ffn_call(xs, oa_s, ob_s, ffn_weights, tm=ms, sample=True, tiles_per_seq=1,
                         conv_fill=(e1, e2), name="ffn_sample")

    def sample_cache(c, heads):
        return c.reshape(DEC_SEQ, 2, heads, HEAD_DIM, DEC_BATCH).transpose(4, 0, 1, 2, 3)[None]

    y_sample = y_s.reshape(DEC_BATCH, DEC_SEQ, D_MODEL)
    cache_a_sample = sample_cache(ca_s, A_KV_HEADS)
    kvb_s = [sample_cache(c, B_SLOTS) for c in (cb1_s, cb2_s, cb3_s)]
    state_conv_sample = u_s.reshape(DEC_BATCH, DEC_SEQ, D_FF)[None, :, DEC_SEQ - (CONV_W - 1):]

    return (y_prompt, y_sample, cache_a_prompt, cache_a_sample, kvb_p[0], kvb_s[0], kvb_p[1], kvb_s[1],
            kvb_p[2], kvb_s[2], state_conv_prompt, state_conv_sample)
```

```python
import functools

import jax
import jax.numpy as jnp
from jax import lax
from jax.experimental import pallas as pl
from jax.experimental.pallas import tpu as pltpu

D_MODEL = 1024
BATCH = 8
SEQ = 2048
DEPTH = 1
DEC_BATCH = 128
DEC_SEQ = 4
PAST_LEN = 16384
HEAD_DIM = 64
A_Q_HEADS = 8
A_KV_HEADS = 2
A_GROUP = A_Q_HEADS // A_KV_HEADS
A_WINDOW = 128
B_DIL = ((128, 1), (512, 4), (2048, 16))
B_SLOTS = 4
BLK = 128
ROPE_THETA = 10000.0
D_FF = ((8 * D_MODEL // 3 + 127) // 128) * 128
CONV_W = 3
ALPHA = (2 * DEPTH) ** 0.25
LN_EPS = 1e-5
NEG = -1e30
SCALE = HEAD_DIM ** -0.5
A_Q = A_Q_HEADS * HEAD_DIM
A_KV = A_KV_HEADS * HEAD_DIM
B_G = B_SLOTS * HEAD_DIM
N_QKV = A_Q + 2 * A_KV + 3 * 3 * B_G

LANES = 128
VMEM_LIMIT = 56 * 1024 * 1024

BF16 = jnp.bfloat16
F32 = jnp.float32

_C_QA = 0
_C_KVA = _C_QA + A_Q
_C_QB = (_C_KVA + 2 * A_KV, _C_KVA + 2 * A_KV + 3 * B_G, _C_KVA + 2 * A_KV + 6 * B_G)
_C_KVB = tuple(c + B_G for c in _C_QB)


def _const_spec(shape):
    nd = len(shape)
    return pl.BlockSpec(shape, lambda *_: (0,) * nd, pipeline_mode=pl.Buffered(1))


def _nt_dot(a, b):
    return lax.dot_general(a, b, (((1,), (1,)), ((), ())), preferred_element_type=F32)


def _dot(a, b):
    return jnp.dot(a, b, preferred_element_type=F32)


HALF = HEAD_DIM // 2


def _rope_split(y, cos4, sin4):
    x1, x2 = y[:, :LANES], y[:, LANES:]
    return jnp.concatenate([x1 * cos4 - x2 * sin4, x2 * cos4 + x1 * sin4], axis=1)


def _rope_rot64(y, cos4, sin_signed):
    outs = []
    for j in range(y.shape[1] // LANES):
        yj = y[:, j * LANES:(j + 1) * LANES]
        outs.append(yj * cos4 + pltpu.roll(yj, LANES // 2, 1) * sin_signed)
    return outs[0] if len(outs) == 1 else jnp.concatenate(outs, axis=1)


def _k_feature_rows(n_heads):
    return [(HEAD_DIM * h + HALF * part, n_heads * HALF * part + HALF * h)
            for part in range(2) for h in range(n_heads)]


def _qkv_kernel(x_ref, w_ref, cos_ref, sin_ref, *refs, tm, prompt, tiles_per_seq):
    out_refs, (xb_ref, y_ref, proj0_ref, proj1_ref) = refs[:-4], refs[-4:]
    q_refs, cache_refs, kv_refs = out_refs[0::3], out_refs[1::3], out_refs[2::3]
    groups = ((A_WINDOW, 1),) + B_DIL

    col_q = (_C_QA,) + _C_QB
    col_kv = (_C_KVA,) + _C_KVB
    width_q = (A_Q,) + (B_G,) * len(B_DIL)
    heads_kv = (A_KV_HEADS,) + (B_SLOTS,) * len(B_DIL)

    def project(proj_ref, c0, width):
        proj_ref[:, c0:c0 + width] = _dot(xb_ref[...], w_ref[:, c0:c0 + width])

    def put_cache_rows(cache_ref, lead, kv_t, n_heads):
        wk = n_heads * HEAD_DIM
        for ref_row, our_row in _k_feature_rows(n_heads):
            cache_ref[lead, ref_row:ref_row + HALF, :] = kv_t[our_row:our_row + HALF]
        cache_ref[lead, wk:2 * wk, :] = kv_t[wk:2 * wk]

    def to_planes(val):
        n_col = val.shape[1] // LANES
        for c in range(n_col):
            y_ref[c] = val[:, c * LANES:(c + 1) * LANES]
        return n_col

    def rope(y, n_heads):
        cos = cos_ref[...]
        sin = sin_ref[...]
        if n_heads == B_SLOTS:
            return _rope_split(y, cos, sin)
        lane = lax.broadcasted_iota(jnp.int32, (tm, LANES), 1)
        return _rope_rot64(y, cos, jnp.where(lane < LANES // 2, -sin, sin))

    def roped_q(proj_ref, g):
        return rope(proj_ref[:, col_q[g]:col_q[g] + width_q[g]], heads_kv[g]) * SCALE

    def roped_kv(proj_ref, g):
        wk = heads_kv[g] * HEAD_DIM
        return jnp.concatenate([rope(proj_ref[:, col_kv[g]:col_kv[g] + wk], heads_kv[g]),
                                proj_ref[:, col_kv[g] + wk:col_kv[g] + 2 * wk]], axis=1)

    def deinterleave(val, out_ref, dil):
        n_col = to_planes(val)
        for r in range(dil):
            out_ref[0, r] = jnp.concatenate(
                [y_ref[c, pl.ds(r, tm // dil, stride=dil), :] for c in range(n_col)], axis=1).astype(BF16)

    if not prompt:
        xb_ref[...] = x_ref[...].astype(BF16)
        project(proj0_ref, 0, N_QKV)
        n_seq = tm // DEC_SEQ
        for g in range(len(groups)):
            q_refs[g][...] = roped_q(proj0_ref, g).astype(BF16)
            kv = roped_kv(proj0_ref, g)
            kv_refs[g][...] = kv
            n_col = to_planes(kv)
            for t in range(DEC_SEQ):
                kv_t = jnp.concatenate(
                    [y_ref[c, pl.ds(t, n_seq, stride=DEC_SEQ), :].T for c in range(n_col)], axis=0)
                put_cache_rows(cache_refs[g], t, kv_t, heads_kv[g])
        return

    step = pl.program_id(0)
    tile = jnp.maximum(step - 1, 0)
    last_tile = (tile % tiles_per_seq) == tiles_per_seq - 1

    @pl.when(step == 0)
    def _():
        proj1_ref[...] = jnp.zeros_like(proj1_ref)

    def step_body(mine, other):
        xb_ref[...] = x_ref[...].astype(BF16)
        for g, (win, dil) in enumerate(groups):
            project(mine, col_q[g], width_q[g])
            q = roped_q(other, g)
            if dil == 1:
                q_refs[g][...] = q.astype(BF16)
            else:
                deinterleave(q, q_refs[g], dil)
            project(mine, col_kv[g], 2 * heads_kv[g] * HEAD_DIM)
            kv = roped_kv(other, g)
            if dil == 1:
                kv_refs[g][...] = kv.astype(BF16)
            else:
                deinterleave(kv, kv_refs[g], dil)
            if min(win, SEQ) == SEQ:
                put_cache_rows(cache_refs[g], 0, kv.T, heads_kv[g])

        @pl.when(last_tile)
        def _():
            for g, (win, _) in enumerate(groups):
                keep = min(win, SEQ)
                if keep < SEQ:
                    assert keep <= tm
                    put_cache_rows(cache_refs[g], 0, roped_kv(other, g)[tm - keep:, :].T, heads_kv[g])

    for par, (mine, other) in enumerate(((proj0_ref, proj1_ref), (proj1_ref, proj0_ref))):
        @pl.when(step % 2 == par)
        def _(mine=mine, other=other):
            step_body(mine, other)


def _qkv_call(x2d, w_qkv, cos_t, sin_t, *, tm, prompt, tiles_per_seq, name):
    m = x2d.shape[0]
    n_tiles = m // tm
    if prompt:
        tile = lambda i: jnp.maximum(i - 1, 0)
        x_spec = pl.BlockSpec((tm, D_MODEL), lambda i: (jnp.minimum(i, n_tiles - 1), 0))
    else:
        assert n_tiles == 1
        tile = lambda i: i
        x_spec = pl.BlockSpec((tm, D_MODEL), lambda i: (i, 0))
    row = lambda w: pl.BlockSpec((tm, w), lambda i: (tile(i), 0))
    tab = pl.BlockSpec((tm, LANES), lambda i: (tile(i) % tiles_per_seq, 0))
    in_specs = [x_spec, _const_spec((D_MODEL, N_QKV)), tab, tab]
    sds = jax.ShapeDtypeStruct
    out_shape, out_specs = [], []
    for g, (win, dil) in enumerate(((A_WINDOW, 1),) + B_DIL):
        wq = A_Q if g == 0 else B_G
        wkv = 2 * (A_KV if g == 0 else B_G)
        if not prompt:
            out_shape += [sds((m, wq), BF16), sds((DEC_SEQ, wkv, m // DEC_SEQ), F32), sds((m, wkv), F32)]
            out_specs += [row(wq), pl.BlockSpec((DEC_SEQ, wkv, m // DEC_SEQ), lambda i: (0, 0, 0)),
                          row(wkv)]
            continue
        n_seq = m // SEQ
        keep = min(win, SEQ)
        if keep == SEQ:
            cache_spec = pl.BlockSpec(
                (1, wkv, tm), lambda i: (tile(i) // tiles_per_seq, 0, tile(i) % tiles_per_seq))
        else:
            cache_spec = pl.BlockSpec((1, wkv, keep), lambda i: (tile(i) // tiles_per_seq, 0, 0))
        if dil == 1:
            out_shape += [sds((m, wq), BF16), sds((n_seq, wkv, keep), F32), sds((m, wkv), BF16)]
            out_specs += [row(wq), cache_spec, row(wkv)]
        else:
            dspec = lambda w, dil=dil: pl.BlockSpec(
                (1, dil, tm // dil, w),
                lambda i: (tile(i) // tiles_per_seq, 0, tile(i) % tiles_per_seq, 0))
            out_shape += [sds((n_seq, dil, SEQ // dil, wq), BF16), sds((n_seq, wkv, keep), F32),
                          sds((n_seq, dil, SEQ // dil, wkv), BF16)]
            out_specs += [dspec(wq), cache_spec, dspec(wkv)]
    proj_bufs = 2 if prompt else 1
    return pl.pallas_call(
        functools.partial(_qkv_kernel, tm=tm, prompt=prompt, tiles_per_seq=tiles_per_seq),
        grid=(n_tiles + 1 if prompt else n_tiles,),
        in_specs=in_specs,
        out_specs=out_specs,
        out_shape=out_shape,
        scratch_shapes=[pltpu.VMEM((tm, D_MODEL), BF16),
                        pltpu.VMEM((2 * B_G // LANES, tm, LANES), F32)]
        + [pltpu.VMEM((tm, N_QKV) if k < proj_bufs else (8, LANES), F32) for k in range(2)],
        compiler_params=pltpu.CompilerParams(
            dimension_semantics=("arbitrary",), vmem_limit_bytes=VMEM_LIMIT),
        name=name,
    )(x2d, w_qkv, cos_t, sin_t)


def _a_query_masks(rows):
    lane = lax.broadcasted_iota(jnp.int32, (rows, LANES), 1)
    return [jnp.where(((lane >> 5) & (A_KV_HEADS - 1)) == j, 1.0, 0.0).astype(BF16)
            for j in range(A_KV_HEADS)]


def _fold_masks():
    row = lax.broadcasted_iota(jnp.int32, (BLK, BLK), 0)
    col = lax.broadcasted_iota(jnp.int32, (BLK, BLK), 1)
    return col > row


def _attn_scratch(heads):
    return [pltpu.VMEM((2, heads * BLK, 2 * BLK), F32), pltpu.VMEM((2, heads * BLK, 2 * BLK), BF16),
            pltpu.VMEM((2, heads * BLK, LANES), F32), pltpu.VMEM((2, heads * BLK, LANES), F32)]


def _attn_pipeline(n_units, bufs, *, heads, load_q, load_k, load_v, has_prev, sink_of_head, finish,
                   mxu_row_sum=False):
    sbuf, pbuf, mbuf, lbuf = bufs
    upper = _fold_masks()
    upper_bf = jnp.where(upper, 1.0, 0.0).astype(BF16)
    width = BLK if has_prev is None else 2 * BLK
    assert n_units % 2 == 0

    sbuf[1] = jnp.zeros(sbuf.shape[1:], F32)
    pbuf[0] = jnp.zeros(pbuf.shape[1:], BF16)
    mbuf[0] = jnp.zeros(mbuf.shape[1:], F32)
    lbuf[0] = jnp.ones(lbuf.shape[1:], F32)

    def scores(u, par):
        sbuf[par, :, 0:width] = _nt_dot(load_q(u), load_k(u))

    def softmax(u, par):
        prev_ok = None if has_prev is None else has_prev(u)
        for h in range(heads):
            rows = slice(h * BLK, (h + 1) * BLK)
            if has_prev is None:
                sf = jnp.where(upper, NEG, sbuf[par, rows, 0:BLK])
            else:
                sf = jnp.where(upper, jnp.where(prev_ok, sbuf[par, rows, 0:BLK], NEG),
                               sbuf[par, rows, BLK:2 * BLK])
            mx = jnp.max(sf, axis=-1, keepdims=True)
            if sink_of_head is not None:
                mx = jnp.maximum(mx, sink_of_head(h))
            e = jnp.exp(sf - mx)
            if not mxu_row_sum:
                lbuf[par, rows, :] = jnp.broadcast_to(jnp.sum(e, axis=-1, keepdims=True), (BLK, LANES))
            e = e.astype(BF16)
            if has_prev is None:
                pbuf[par, rows, 0:BLK] = e
            else:
                p_prev = e * upper_bf
                pbuf[par, rows, 0:BLK] = p_prev
                pbuf[par, rows, BLK:2 * BLK] = e - p_prev
            mbuf[par, rows, :] = jnp.broadcast_to(mx, (BLK, LANES))

    def values(u, par):
        finish(u, _dot(pbuf[par, :, 0:width], load_v(u)), mbuf[par], lbuf[par])

    last = n_units - 1

    def trip_pair(t, carry):
        for par in (0, 1):
            i = 2 * t + par
            scores(jnp.minimum(i, last), par)
            softmax(jnp.clip(i - 1, 0, last), 1 - par)
            values(jnp.clip(i - 2, 0, last), par)
        return carry

    lax.fori_loop(0, n_units // 2 + 1, trip_pair, 0)


def _attn_a_kernel(sink_ref, q_ref, kv_ref, o_ref, *bufs):
    n_blk = q_ref.shape[0] // BLK
    lane = lax.broadcasted_iota(jnp.int32, (BLK, LANES), 1)
    hi = lane >= HEAD_DIM
    mask_bf = _a_query_masks(BLK)

    def rows_of(b):
        return pl.ds(pl.multiple_of(b * BLK, BLK), BLK)

    def prev_cur(b, cols):
        return jnp.concatenate([kv_ref[rows_of(jnp.maximum(b - 1, 0)), cols], kv_ref[rows_of(b), cols]],
                               axis=0)

    def load_q(b):
        qblk = q_ref[rows_of(b), :]
        return jnp.concatenate([qblk[:, g * LANES:(g + 1) * LANES] * mask_bf[j]
                                for j in range(A_KV_HEADS) for g in range(A_GROUP)], axis=0)

    ones = jnp.ones((2 * BLK, LANES), BF16)

    def load_v(b):
        return jnp.concatenate([prev_cur(b, slice(A_KV, 2 * A_KV)), ones], axis=1)

    def finish(b, o, m, _):
        def normalised(h):
            rows = slice(h * BLK, (h + 1) * BLK)
            den = o[rows, LANES:] + jnp.exp(sink_ref[h] - m[rows])
            return o[rows, :LANES] / den

        for g in range(A_GROUP):
            o_ref[rows_of(b), g * LANES:(g + 1) * LANES] = jnp.where(
                hi, normalised(A_GROUP + g), normalised(g)).astype(BF16)

    _attn_pipeline(n_blk, bufs, heads=A_Q_HEADS, load_q=load_q,
                   load_k=lambda b: prev_cur(b, slice(0, A_KV)), load_v=load_v,
                   has_prev=lambda b: b > 0, sink_of_head=lambda h: sink_ref[h], finish=finish,
                   mxu_row_sum=True)


def _attn_a_call(sink, q_a, kva_bf):
    n_seq = q_a.shape[0] // SEQ
    return pl.pallas_call(
        _attn_a_kernel,
        grid=(n_seq,),
        in_specs=[pl.BlockSpec(memory_space=pltpu.SMEM),
                  pl.BlockSpec((SEQ, A_Q), lambda n: (n, 0)),
                  pl.BlockSpec((SEQ, 2 * A_KV), lambda n: (n, 0))],
        out_specs=pl.BlockSpec((SEQ, A_Q), lambda n: (n, 0)),
        out_shape=jax.ShapeDtypeStruct(q_a.shape, BF16),
        scratch_shapes=_attn_scratch(A_Q_HEADS),
        compiler_params=pltpu.CompilerParams(
            dimension_semantics=("arbitrary",), vmem_limit_bytes=VMEM_LIMIT),
        name="attn_a_prompt",
    )(sink, q_a, kva_bf)


def _slot_masks(rows):
    lane = lax.broadcasted_iota(jnp.int32, (rows, B_G), 1)
    masks = [(lane >> 6) == s for s in range(B_SLOTS)]
    masks_bf = [jnp.where(((lane >> 5) & (B_SLOTS - 1)) == s, 1.0, 0.0).astype(BF16) for s in range(B_SLOTS)]
    return masks, masks_bf


def _attn_b_kernel(q1_ref, kv1_ref, q2_ref, kv2_ref, q3_ref, kv3_ref, o_ref,
                   acc2_ref, m2_ref, l2_ref, acc3_ref, m3_ref, l3_ref, *bufs):
    masks, masks_bf = _slot_masks(BLK)
    d2, d3 = B_DIL[1][1], B_DIL[2][1]
    nb2 = SEQ // d2 // BLK
    assert SEQ // d3 == BLK
    n_plane = B_G // LANES

    def put(ref, rows, val):
        for c in range(n_plane):
            ref[c, rows, :] = val[:, c * LANES:(c + 1) * LANES]

    def get(ref, rows):
        return jnp.concatenate([ref[c, rows, :] for c in range(n_plane)], axis=1)

    def stack_q(qblk):
        return jnp.concatenate([qblk * masks_bf[s] for s in range(B_SLOTS)], axis=0)

    def unstack(o, m, l):
        acc = jnp.where(masks[0], o[0:BLK], 0.0)
        rep = lambda x, s: jnp.concatenate([x[s * BLK:(s + 1) * BLK]] * n_plane, axis=1)
        mf, lf = rep(m, 0), rep(l, 0)
        for s in range(1, B_SLOTS):
            acc = jnp.where(masks[s], o[s * BLK:(s + 1) * BLK], acc)
            mf = jnp.where(masks[s], rep(m, s), mf)
            lf = jnp.where(masks[s], rep(l, s), lf)
        return acc, mf, lf

    def rows_of(b):
        return pl.ds(pl.multiple_of(b * BLK, BLK), BLK)

    run = functools.partial(_attn_pipeline, bufs=bufs, heads=B_SLOTS, sink_of_head=None)

    def finish3(r, o, m, l):
        rows = pl.ds(r, BLK, stride=d3)
        for ref, val in zip((acc3_ref, m3_ref, l3_ref), unstack(o, m, l)):
            put(ref, rows, val)

    run(d3, load_q=lambda r: stack_q(q3_ref[r]), load_k=lambda r: kv3_ref[r, :, 0:B_G],
        load_v=lambda r: kv3_ref[r, :, B_G:2 * B_G], has_prev=None, finish=finish3)

    def prev_cur2(u, cols):
        r, b = u // nb2, u % nb2
        return jnp.concatenate([kv2_ref[r, rows_of(jnp.maximum(b - 1, 0)), cols], kv2_ref[r, rows_of(b), cols]],
                               axis=0)

    def finish2(u, o, m, l):
        r, b = u // nb2, u % nb2
        rows = pl.ds(r + b * (BLK * d2), BLK, stride=d2)
        for ref, val in zip((acc2_ref, m2_ref, l2_ref), unstack(o, m, l)):
            put(ref, rows, val)

    run(d2 * nb2, load_q=lambda u: stack_q(q2_ref[u // nb2, rows_of(u % nb2), :]),
        load_k=lambda u: prev_cur2(u, slice(0, B_G)), load_v=lambda u: prev_cur2(u, slice(B_G, 2 * B_G)),
        has_prev=lambda u: (u % nb2) > 0, finish=finish2)

    def prev_cur1(b, cols):
        return jnp.concatenate([kv1_ref[rows_of(jnp.maximum(b - 1, 0)), cols], kv1_ref[rows_of(b), cols]],
                               axis=0)

    def finish1(b, o, m, l):
        acc1, m1, l1 = unstack(o, m, l)
        rows = rows_of(b)
        m2, m3 = get(m2_ref, rows), get(m3_ref, rows)
        mx = jnp.maximum(jnp.maximum(m1, m2), m3)
        w1, w2, w3 = jnp.exp(m1 - mx), jnp.exp(m2 - mx), jnp.exp(m3 - mx)
        num = w1 * acc1 + w2 * get(acc2_ref, rows) + w3 * get(acc3_ref, rows)
        den = w1 * l1 + w2 * get(l2_ref, rows) + w3 * get(l3_ref, rows)
        o_ref[rows, :] = (num / den).astype(BF16)

    run(SEQ // BLK, load_q=lambda b: stack_q(q1_ref[rows_of(b), :]),
        load_k=lambda b: prev_cur1(b, slice(0, B_G)), load_v=lambda b: prev_cur1(b, slice(B_G, 2 * B_G)),
        has_prev=lambda b: b > 0, finish=finish1)


def _attn_b_call(q1, kv1, q2, kv2, q3, kv3):
    n_seq = q1.shape[0] // SEQ
    d2, d3 = B_DIL[1][1], B_DIL[2][1]
    rows = lambda w: pl.BlockSpec((SEQ, w), lambda n: (n, 0))
    dsp = lambda d, w: pl.BlockSpec((None, d, SEQ // d, w), lambda n: (n, 0, 0, 0))
    return pl.pallas_call(
        _attn_b_kernel,
        grid=(n_seq,),
        in_specs=[rows(B_G), rows(2 * B_G), dsp(d2, B_G), dsp(d2, 2 * B_G), dsp(d3, B_G),
                  dsp(d3, 2 * B_G)],
        out_specs=rows(B_G),
        out_shape=jax.ShapeDtypeStruct((q1.shape[0], B_G), BF16),
        scratch_shapes=[pltpu.VMEM((B_G // LANES, SEQ, LANES), F32) for _ in range(6)]
        + _attn_scratch(B_SLOTS),
        compiler_params=pltpu.CompilerParams(
            dimension_semantics=("arbitrary",), vmem_limit_bytes=VMEM_LIMIT),
        name="attn_b_prompt",
    )(q1, kv1, q2, kv2, q3, kv3)


def _sample_attn_kernel(sink_ref, qa_ref, kna_ref, ca_ref, q1_ref, kn1_ref, c1_ref,
                        q2_ref, kn2_ref, c2_ref, q3_ref, kn3_ref, c3_ref, oa_ref, ob_ref, *, nb):
    tt = nb * DEC_SEQ
    pad = BLK - tt
    tt_shift, seq_shift = tt.bit_length() - 1, DEC_SEQ.bit_length() - 1
    assert tt == 1 << tt_shift and DEC_SEQ == 1 << seq_shift

    def pad_rows(x):
        return jnp.concatenate([x.astype(BF16), jnp.zeros((pad, x.shape[1]), BF16)], axis=0)

    def window_group(qm, kn_ref, c_ref, dil, sinkv, finish):
        rows, kd = qm.shape
        lc = c_ref.shape[2]
        kn = kn_ref[...]
        knpad = pad_rows(kn[:, :kd])
        vnpad = pad_rows(kn[:, kd:])
        s_new = _nt_dot(qm, knpad)
        def masks(width):
            col = lax.broadcasted_iota(jnp.int32, (rows, width), 1)
            t = lax.broadcasted_iota(jnp.int32, (rows, width), 0) & (DEC_SEQ - 1)
            same_res = (col & (dil - 1)) == (t & (dil - 1))
            return (col > t) & same_res, (col <= t) & same_res

        valid, _ = masks(lc)
        _, is_new = masks(BLK)
        k_rows = sorted(_k_feature_rows(kd // HEAD_DIM), key=lambda rows: rows[1])
        for n in range(nb):
            k_t = jnp.concatenate([c_ref[n, ref_row:ref_row + HALF, :] for ref_row, _ in k_rows],
                                  axis=0).astype(BF16)
            v_t = c_ref[n, kd:2 * kd, :].astype(BF16)
            s_c = jnp.where(valid, _dot(qm, k_t), NEG)
            s_n = pltpu.roll(s_new, BLK - DEC_SEQ * n, 1) if n else s_new
            first = jnp.where(is_new, s_n, s_c[:, :BLK])
            s = first if lc == BLK else jnp.concatenate([first, s_c[:, BLK:]], axis=1)
            mx = jnp.max(s, axis=-1, keepdims=True)
            if sinkv is not None:
                mx = jnp.maximum(mx, sinkv)
            e = jnp.exp(s - mx)
            l = jnp.sum(e, axis=-1, keepdims=True)
            if sinkv is not None:
                l = l + jnp.exp(sinkv - mx)
            p_new = jnp.where(is_new, e[:, :BLK], 0.0)
            if n:
                p_new = pltpu.roll(p_new, DEC_SEQ * n, 1)
            o = _nt_dot(jnp.where(valid, e, 0.0).astype(BF16), v_t) + _dot(p_new.astype(BF16), vnpad)
            finish(n, o, mx, l)

    rows_a = A_Q_HEADS * tt
    lane_t = lax.broadcasted_iota(jnp.int32, (tt, LANES), 1)
    hi_t = lane_t >= HEAD_DIM
    mask_bf = _a_query_masks(tt)
    qa = qa_ref[...]
    qm = jnp.concatenate([qa[:, g * LANES:(g + 1) * LANES] * mask_bf[j]
                          for j in range(A_KV_HEADS) for g in range(A_GROUP)], axis=0)
    head = lax.broadcasted_iota(jnp.int32, (rows_a, 1), 0) >> tt_shift
    sinkv = jnp.zeros((rows_a, 1), F32)
    for h in range(A_Q_HEADS):
        sinkv = jnp.where(head == h, sink_ref[h], sinkv)
    seq_t = lax.broadcasted_iota(jnp.int32, (tt, LANES), 0) >> seq_shift
    out_a = [jnp.zeros((tt, LANES), F32) for _ in range(A_GROUP)]

    def finish_a(n, o, mx, l):
        o = o / l
        for g in range(A_GROUP):
            og = jnp.where(hi_t, o[(A_GROUP + g) * tt:(A_GROUP + g + 1) * tt], o[g * tt:(g + 1) * tt])
            out_a[g] = jnp.where(seq_t == n, og, out_a[g])

    window_group(qm, kna_ref, ca_ref, 1, sinkv, finish_a)
    for g in range(A_GROUP):
        oa_ref[:, g * LANES:(g + 1) * LANES] = out_a[g].astype(BF16)

    masks_t, masks_bf_t = _slot_masks(tt)
    tok_t = lax.broadcasted_iota(jnp.int32, (tt, B_G), 0)

    def stack_q(q):
        return jnp.concatenate([q * masks_bf_t[s] for s in range(B_SLOTS)], axis=0)

    def unstack(o, mx, l):
        acc = jnp.where(masks_t[0], o[0:tt], 0.0)
        mf = jnp.broadcast_to(mx[0:tt], (tt, B_G))
        lf = jnp.broadcast_to(l[0:tt], (tt, B_G))
        for s in range(1, B_SLOTS):
            acc = jnp.where(masks_t[s], o[s * tt:(s + 1) * tt], acc)
            mf = jnp.where(masks_t[s], mx[s * tt:(s + 1) * tt], mf)
            lf = jnp.where(masks_t[s], l[s * tt:(s + 1) * tt], lf)
        return acc, mf, lf

    zero = jnp.zeros((tt, B_G), F32)
    stats = []
    for (win, dil), q_ref, kn_ref, c_ref in zip(B_DIL, (q1_ref, q2_ref, q3_ref),
                                                (kn1_ref, kn2_ref, kn3_ref), (c1_ref, c2_ref, c3_ref)):
        assert c_ref.shape[2] == win
        group = [zero, zero, zero]

        def finish_b(n, o, mx, l, group=group):
            sel = (tok_t >> seq_shift) == n
            for k, new in enumerate(unstack(o, mx, l)):
                group[k] = jnp.where(sel, new, group[k])

        window_group(stack_q(q_ref[...]), kn_ref, c_ref, dil, None, finish_b)
        stats.append(group)

    (acc1, m1, l1), (acc2, m2, l2), (acc3, m3, l3) = stats
    mx = jnp.maximum(jnp.maximum(m1, m2), m3)
    w1, w2, w3 = jnp.exp(m1 - mx), jnp.exp(m2 - mx), jnp.exp(m3 - mx)
    ob_ref[...] = ((w1 * acc1 + w2 * acc2 + w3 * acc3) / (w1 * l1 + w2 * l2 + w3 * l3)).astype(BF16)


def _sample_attn_call(sink, qa, kna, cache_a, q1, kn1, cache_b1, q2, kn2, cache_b2, q3, kn3, cache_b3,
                      *, nb):
    n_seq = cache_a.shape[0]
    tt = nb * DEC_SEQ
    kv_w = 2 * B_G
    tok = lambda w: pl.BlockSpec((tt, w), lambda i: (i, 0))
    cache = lambda c: pl.BlockSpec((nb,) + c.shape[1:], lambda i: (i, 0, 0))
    return pl.pallas_call(
        functools.partial(_sample_attn_kernel, nb=nb),
        grid=(n_seq // nb,),
        in_specs=[pl.BlockSpec(memory_space=pltpu.SMEM),
                  tok(A_Q), tok(2 * A_KV), cache(cache_a),
                  tok(B_G), tok(kv_w), cache(cache_b1),
                  tok(B_G), tok(kv_w), cache(cache_b2),
                  tok(B_G), tok(kv_w), cache(cache_b3)],
        out_specs=[tok(A_Q), tok(B_G)],
        out_shape=[jax.ShapeDtypeStruct((n_seq * DEC_SEQ, A_Q), BF16),
                   jax.ShapeDtypeStruct((n_seq * DEC_SEQ, B_G), BF16)],
        compiler_params=pltpu.CompilerParams(
            dimension_semantics=("arbitrary",), vmem_limit_bytes=VMEM_LIMIT),
        name="attn_sample",
    )(sink, qa, kna, cache_a, q1, kn1, cache_b1, q2, kn2, cache_b2, q3, kn3, cache_b3)


def _layernorm(x, g, b):
    mu = jnp.mean(x, axis=-1, keepdims=True)
    xc = x - mu
    var = jnp.mean(xc * xc, axis=-1, keepdims=True)
    return xc * lax.rsqrt(var + LN_EPS) * g + b


def _gelu_exact(x):
    return 0.5 * x * (1.0 + lax.erf(x * (0.5 ** 0.5)))


_FF_CHUNK = 256
_CARRY = 8


def _ffn_kernel(*refs, tm, sample, tiles_per_seq):
    if sample:
        (x_ref, oa_ref, ob_ref, e1_ref, e2_ref, wg_ref, wa_ref, wb_ref, wo_ref, g1_ref, b1_ref,
         wup_ref, cw_ref, cb_ref, wdn_ref, g2_ref, b2_ref, y_ref, u_ref,
         xb_ref, m_ref, h_ref, hb_ref, gg_ref, ext_ref) = refs
    else:
        (x_ref, oa_ref, ob_ref, wg_ref, wa_ref, wb_ref, wo_ref, g1_ref, b1_ref,
         wup_ref, cw_ref, cb_ref, wdn_ref, g2_ref, b2_ref, y_ref, ulast_ref,
         xb_ref, m_ref, h_ref, hb_ref, gg_ref, ext_ref, carry_ref) = refs

    x = x_ref[...]
    xb_ref[...] = x.astype(BF16)
    half = D_MODEL // 2
    for c in range(2):
        cs = slice(c * half, (c + 1) * half)
        ga = _dot(xb_ref[...], wg_ref[:, c * half:(c + 1) * half])
        gb = _dot(xb_ref[...], wg_ref[:, D_MODEL + c * half:D_MODEL + (c + 1) * half])
        ta = _dot(oa_ref[...], wa_ref[:, cs])
        tb = _dot(ob_ref[...], wb_ref[:, cs])
        m_ref[:, cs] = (jax.nn.sigmoid(ga) * ta + jax.nn.sigmoid(gb) * tb).astype(BF16)
    mix = _dot(m_ref[...], wo_ref[...])
    h = _layernorm(ALPHA * x + mix, g1_ref[...], b1_ref[...])
    h_ref[...] = h
    hb_ref[...] = h.astype(BF16)

    if sample:
        ext_ref[0:_CARRY, :] = jnp.zeros((_CARRY, _FF_CHUNK), F32)
        t = lax.broadcasted_iota(jnp.int32, (tm, _FF_CHUNK), 0) & (DEC_SEQ - 1)
    else:
        @pl.when(pl.program_id(0) % tiles_per_seq == 0)
        def _():
            carry_ref[...] = jnp.zeros_like(carry_ref)

    for c in range(D_FF // _FF_CHUNK):
        cs = slice(c * _FF_CHUNK, (c + 1) * _FF_CHUNK)
        u = _dot(hb_ref[...], wup_ref[:, c * _FF_CHUNK:(c + 1) * _FF_CHUNK])
        v = _dot(hb_ref[...], wup_ref[:, D_FF + c * _FF_CHUNK:D_FF + (c + 1) * _FF_CHUNK])
        if not sample:
            ext_ref[0:_CARRY, :] = carry_ref[:, cs]
        ext_ref[_CARRY:_CARRY + tm, :] = u
        u1 = ext_ref[_CARRY - 1:_CARRY - 1 + tm, :]
        u2 = ext_ref[_CARRY - 2:_CARRY - 2 + tm, :]
        if sample:
            u1 = jnp.where(t >= 1, u1, e1_ref[:, cs])
            u2 = jnp.where(t >= 2, u2, e2_ref[:, cs])
            u_ref[:, cs] = u
        else:
            tail = u[tm - _CARRY:tm, :]
            carry_ref[:, cs] = tail
            ulast_ref[0, :, cs] = tail
        a = cb_ref[:, cs] + cw_ref[0:1, cs] * u2 + cw_ref[1:2, cs] * u1 + cw_ref[2:3, cs] * u
        gg_ref[:, cs] = (_gelu_exact(a) * v).astype(BF16)
    f = _dot(gg_ref[...], wdn_ref[...])
    y_ref[...] = _layernorm(ALPHA * h_ref[...] + f, g2_ref[...], b2_ref[...])


def _ffn_call(x2d, oa, ob, weights, *, tm, sample, tiles_per_seq, conv_fill=None, name):
    m = x2d.shape[0]
    n_tiles = m // tm
    row = lambda w: pl.BlockSpec((tm, w), lambda i: (i, 0))
    sds = jax.ShapeDtypeStruct
    in_specs = [row(D_MODEL), row(A_Q), row(B_G)]
    args = [x2d, oa, ob]
    if sample:
        in_specs += [row(D_FF), row(D_FF)]
        args += list(conv_fill)
    in_specs += [_const_spec(w.shape) for w in weights]
    args += list(weights)
    scratch = [pltpu.VMEM((tm, D_MODEL), BF16), pltpu.VMEM((tm, D_MODEL), BF16),
               pltpu.VMEM((tm, D_MODEL), F32), pltpu.VMEM((tm, D_MODEL), BF16),
               pltpu.VMEM((tm, D_FF), BF16), pltpu.VMEM((tm + _CARRY, _FF_CHUNK), F32)]
    if sample:
        out_shape = [sds((m, D_MODEL), F32), sds((m, D_FF), F32)]
        out_specs = [row(D_MODEL), row(D_FF)]
    else:
        out_shape = [sds((m, D_MODEL), F32), sds((n_tiles, _CARRY, D_FF), F32)]
        out_specs = [row(D_MODEL), pl.BlockSpec((1, _CARRY, D_FF), lambda i: (i, 0, 0))]
        scratch.append(pltpu.VMEM((_CARRY, D_FF), F32))
    return pl.pallas_call(
        functools.partial(_ffn_kernel, tm=tm, sample=sample, tiles_per_seq=tiles_per_seq),
        grid=(n_tiles,),
        in_specs=in_specs,
        out_specs=out_specs,
        out_shape=out_shape,
        scratch_shapes=scratch,
        compiler_params=pltpu.CompilerParams(
            dimension_semantics=("arbitrary",), vmem_limit_bytes=VMEM_LIMIT),
        name=name,
    )(*args)


def _rope_tables(pos):
    half = HEAD_DIM // 2
    inv = ROPE_THETA ** (-jnp.arange(half, dtype=F32) / half)
    ang = pos.astype(F32)[:, None] * inv[None, :]
    cos, sin = jnp.cos(ang), jnp.sin(ang)
    reps = LANES // HALF
    return jnp.tile(cos, (1, reps)), jnp.tile(sin, (1, reps))


def _rotary_order(w, heads):
    x1 = [w[:, h * HEAD_DIM:h * HEAD_DIM + HALF] for h in heads]
    x2 = [w[:, h * HEAD_DIM + HALF:(h + 1) * HEAD_DIM] for h in heads]
    return x1 + x2


def _prep_qkv_weight(w_in):
    qa = w_in[:, 0:A_Q]
    ka = w_in[:, A_Q:A_Q + A_KV]
    va = w_in[:, A_Q + A_KV:A_Q + 2 * A_KV]
    b0 = A_Q + 2 * A_KV
    qb = w_in[:, b0:b0 + 3 * B_G]
    kb = w_in[:, b0 + 3 * B_G:b0 + 6 * B_G]
    vb = w_in[:, b0 + 6 * B_G:b0 + 9 * B_G]
    cols = []
    for c in range(A_GROUP):
        cols += _rotary_order(qa, (c, A_GROUP + c))
    cols += _rotary_order(ka, range(A_KV_HEADS)) + [va]
    for g in range(len(B_DIL)):
        gs = slice(g * B_G, (g + 1) * B_G)
        cols += _rotary_order(qb[:, gs], range(B_SLOTS)) + _rotary_order(kb[:, gs], range(B_SLOTS)) + [vb[:, gs]]
    return jnp.concatenate(cols, axis=1).astype(BF16)


def _prep_br_a(w_br_a):
    rows = []
    for c in range(A_GROUP):
        rows += [w_br_a[c * HEAD_DIM:(c + 1) * HEAD_DIM],
                 w_br_a[(A_GROUP + c) * HEAD_DIM:(A_GROUP + c + 1) * HEAD_DIM]]
    return jnp.concatenate(rows, axis=0).astype(BF16)


_TM_PROMPT = 512
_SAMPLE_SEQS_PER_STEP = 4


def kernel(x_prompt, x_sample, cache_a, cache_b1, cache_b2, cache_b3, state_conv, w_in, sink_a, w_br_a, w_br_b, w_o, ln1_g, ln1_b, w_up, conv_w, conv_b, w_down, ln2_g, ln2_b):
    assert DEPTH == 1
    l = 0
    w_qkv = _prep_qkv_weight(w_in[l])
    w_gate = w_in[l][:, N_QKV:].astype(BF16)
    ffn_weights = (w_gate, _prep_br_a(w_br_a[l]), w_br_b[l].astype(BF16), w_o[l].astype(BF16),
                   ln1_g[l][None], ln1_b[l][None], w_up[l].astype(BF16), conv_w[l], conv_b[l][None],
                   w_down[l].astype(BF16), ln2_g[l][None], ln2_b[l][None])
    sink = sink_a[l].astype(F32)

    mp = BATCH * SEQ
    xp = x_prompt.reshape(mp, D_MODEL)
    cos_p, sin_p = _rope_tables(jnp.arange(SEQ, dtype=jnp.int32))
    tps = SEQ // _TM_PROMPT
    (qa, ca_p, kva_bf, qb1, cb1_p, kvb1_bf, qb2, cb2_p, kvb2_bf, qb3, cb3_p, kvb3_bf) = _qkv_call(
        xp, w_qkv, cos_p, sin_p, tm=_TM_PROMPT, prompt=True, tiles_per_seq=tps, name="qkv_prompt")
    oa = _attn_a_call(sink, qa, kva_bf)
    ob = _attn_b_call(qb1, kvb1_bf, qb2, kvb2_bf, qb3, kvb3_bf)
    y_p, ulast = _ffn_call(xp, oa, ob, ffn_weights, tm=_TM_PROMPT, sample=False, tiles_per_seq=tps,
                           name="ffn_prompt")

    def prompt_cache(c, heads):
        return c.reshape(BATCH, 2, heads, HEAD_DIM, c.shape[-1]).transpose(0, 4, 1, 2, 3)[None]

    y_prompt = y_p.reshape(BATCH, SEQ, D_MODEL)
    cache_a_prompt = prompt_cache(ca_p, A_KV_HEADS)
    kvb_p = [prompt_cache(c, B_SLOTS) for c in (cb1_p, cb2_p, cb3_p)]
    state_conv_prompt = ulast.reshape(BATCH, tps, _CARRY, D_FF)[None, :, tps - 1, _CARRY - (CONV_W - 1):]

    ms = DEC_BATCH * DEC_SEQ
    xs = x_sample.reshape(ms, D_MODEL)
    cos_s, sin_s = _rope_tables(PAST_LEN + (jnp.arange(ms, dtype=jnp.int32) % DEC_SEQ))
    (qa_s, ca_s, kva_s, qb1_s, cb1_s, kvb1_s, qb2_s, cb2_s, kvb2_s, qb3_s, cb3_s, kvb3_s) = _qkv_call(
        xs, w_qkv, cos_s, sin_s, tm=ms, prompt=False, tiles_per_seq=1, name="qkv_sample")

    def window_buffer(c):
        return c.transpose(0, 2, 3, 4, 1).reshape(c.shape[0], -1, c.shape[1])

    ca = window_buffer(cache_a[l])
    cb = [window_buffer(c[l]) for c in (cache_b1, cache_b2, cache_b3)]
    oa_s, ob_s = _sample_attn_call(sink, qa_s, kva_s, ca, qb1_s, kvb1_s, cb[0], qb2_s, kvb2_s, cb[1],
                                   qb3_s, kvb3_s, cb[2], nb=_SAMPLE_SEQS_PER_STEP)
    sc = state_conv[l]
    zeros = lambda k: jnp.zeros((DEC_BATCH, k, D_FF), F32)
    e1 = jnp.concatenate([sc[:, 1:2], zeros(DEC_SEQ - 1)], axis=1).reshape(ms, D_FF)
    e2 = jnp.concatenate([sc, zeros(DEC_SEQ - 2)], axis=1).reshape(ms, D_FF)
    y_s, u_s = _ffn_call(xs, oa_s, ob_s, ffn_weights, tm=ms, sample=True, tiles_per_seq=1,
                         conv_fill=(e1, e2), name="ffn_sample")

    def sample_cache(c, heads):
        return c.reshape(DEC_SEQ, 2, heads, HEAD_DIM, DEC_BATCH).transpose(4, 0, 1, 2, 3)[None]

    y_sample = y_s.reshape(DEC_BATCH, DEC_SEQ, D_MODEL)
    cache_a_sample = sample_cache(ca_s, A_KV_HEADS)
    kvb_s = [sample_cache(c, B_SLOTS) for c in (cb1_s, cb2_s, cb3_s)]
    state_conv_sample = u_s.reshape(DEC_BATCH, DEC_SEQ, D_FF)[None, :, DEC_SEQ - (CONV_W - 1):]

    return (y_prompt, y_sample, cache_a_prompt, cache_a_sample, kvb_p[0], kvb_s[0], kvb_p[1], kvb_s[1],
            kvb_p[2], kvb_s[2], state_conv_prompt, state_conv_sample)
```

```python
import functools

import jax
import jax.numpy as jnp
from jax import lax
from jax.experimental import pallas as pl
from jax.experimental.pallas import tpu as pltpu

D_MODEL = 1024
BATCH = 8
SEQ = 2048
DEPTH = 1
DEC_BATCH = 128
DEC_SEQ = 4
PAST_LEN = 16384
HEAD_DIM = 64
A_Q_HEADS = 8
A_KV_HEADS = 2
A_GROUP = A_Q_HEADS // A_KV_HEADS
A_WINDOW = 128
B_DIL = ((128, 1), (512, 4), (2048, 16))
B_SLOTS = 4
BLK = 128
ROPE_THETA = 10000.0
D_FF = ((8 * D_MODEL // 3 + 127) // 128) * 128
CONV_W = 3
ALPHA = (2 * DEPTH) ** 0.25
LN_EPS = 1e-5
NEG = -1e30
SCALE = HEAD_DIM ** -0.5
A_Q = A_Q_HEADS * HEAD_DIM
A_KV = A_KV_HEADS * HEAD_DIM
B_G = B_SLOTS * HEAD_DIM
N_QKV = A_Q + 2 * A_KV + 3 * 3 * B_G

LANES = 128
VMEM_LIMIT = 56 * 1024 * 1024

BF16 = jnp.bfloat16
F32 = jnp.float32

_C_QA = 0
_C_KVA = _C_QA + A_Q
_C_QB = (_C_KVA + 2 * A_KV, _C_KVA + 2 * A_KV + 3 * B_G, _C_KVA + 2 * A_KV + 6 * B_G)
_C_KVB = tuple(c + B_G for c in _C_QB)


def _const_spec(shape):
    nd = len(shape)
    return pl.BlockSpec(shape, lambda *_: (0,) * nd, pipeline_mode=pl.Buffered(1))


def _nt_dot(a, b):
    return lax.dot_general(a, b, (((1,), (1,)), ((), ())), preferred_element_type=F32)


def _dot(a, b):
    return jnp.dot(a, b, preferred_element_type=F32)


HALF = HEAD_DIM // 2


def _rope_split(y, cos4, sin4):
    x1, x2 = y[:, :LANES], y[:, LANES:]
    return jnp.concatenate([x1 * cos4 - x2 * sin4, x2 * cos4 + x1 * sin4], axis=1)


def _rope_rot64(y, cos4, sin_signed):
    outs = []
    for j in range(y.shape[1] // LANES):
        yj = y[:, j * LANES:(j + 1) * LANES]
        outs.append(yj * cos4 + pltpu.roll(yj, LANES // 2, 1) * sin_signed)
    return outs[0] if len(outs) == 1 else jnp.concatenate(outs, axis=1)


def _k_feature_rows(n_heads):
    return [(HEAD_DIM * h + HALF * part, n_heads * HALF * part + HALF * h)
            for part in range(2) for h in range(n_heads)]


def _qkv_kernel(x_ref, w_ref, cos_ref, sin_ref, *refs, tm, prompt, tiles_per_seq):
    n_groups = 1 + len(B_DIL)
    if prompt:
        rider_in, refs = refs[:_N_SAMPLE_ATTN_IN], refs[_N_SAMPLE_ATTN_IN:]
        out_refs, (oa_s_ref, ob_s_ref) = refs[:3 * n_groups], refs[3 * n_groups:3 * n_groups + 2]
    else:
        out_refs = refs[:3 * n_groups]
    xb_ref, y_ref, proj0_ref, proj1_ref = refs[-4:]
    q_refs, cache_refs, kv_refs = out_refs[0::3], out_refs[1::3], out_refs[2::3]
    groups = ((A_WINDOW, 1),) + B_DIL

    def rider(k):
        out_a, out_b = _sample_attn_block(*rider_in, tt=n_groups * DEC_SEQ, seqs=[(0, k)])
        seq_shift = DEC_SEQ.bit_length() - 1

        def merge(ref, cols, new):
            if k:
                mine = (lax.broadcasted_iota(jnp.int32, new.shape, 0) >> seq_shift) == k
                new = jnp.where(mine, new, ref[:, cols].astype(F32))
            ref[:, cols] = new.astype(BF16)

        for c in range(A_GROUP):
            merge(oa_s_ref, slice(c * LANES, (c + 1) * LANES), out_a[c])
        merge(ob_s_ref, slice(0, B_G), out_b)

    col_q = (_C_QA,) + _C_QB
    col_kv = (_C_KVA,) + _C_KVB
    width_q = (A_Q,) + (B_G,) * len(B_DIL)
    heads_kv = (A_KV_HEADS,) + (B_SLOTS,) * len(B_DIL)

    def project(proj_ref, c0, width):
        proj_ref[:, c0:c0 + width] = _dot(xb_ref[...], w_ref[:, c0:c0 + width])

    def put_cache_rows(cache_ref, lead, kv_t, n_heads):
        wk = n_heads * HEAD_DIM
        for ref_row, our_row in _k_feature_rows(n_heads):
            cache_ref[lead, ref_row:ref_row + HALF, :] = kv_t[our_row:our_row + HALF]
        cache_ref[lead, wk:2 * wk, :] = kv_t[wk:2 * wk]

    def to_planes(val):
        n_col = val.shape[1] // LANES
        for c in range(n_col):
            y_ref[c] = val[:, c * LANES:(c + 1) * LANES]
        return n_col

    def rope(y, n_heads):
        cos = cos_ref[...]
        sin = sin_ref[...]
        if n_heads == B_SLOTS:
            return _rope_split(y, cos, sin)
        lane = lax.broadcasted_iota(jnp.int32, (tm, LANES), 1)
        return _rope_rot64(y, cos, jnp.where(lane < LANES // 2, -sin, sin))

    def roped_q(proj_ref, g):
        return rope(proj_ref[:, col_q[g]:col_q[g] + width_q[g]], heads_kv[g]) * SCALE

    def roped_kv(proj_ref, g):
        wk = heads_kv[g] * HEAD_DIM
        return jnp.concatenate([rope(proj_ref[:, col_kv[g]:col_kv[g] + wk], heads_kv[g]),
                                proj_ref[:, col_kv[g] + wk:col_kv[g] + 2 * wk]], axis=1)

    def deinterleave(val, out_ref, dil):
        n_col = to_planes(val)
        for r in range(dil):
            out_ref[0, r] = jnp.concatenate(
                [y_ref[c, pl.ds(r, tm // dil, stride=dil), :] for c in range(n_col)], axis=1).astype(BF16)

    if not prompt:
        xb_ref[...] = x_ref[...].astype(BF16)
        project(proj0_ref, 0, N_QKV)
        n_seq = tm // DEC_SEQ
        for g in range(len(groups)):
            q_refs[g][...] = roped_q(proj0_ref, g).astype(BF16)
            kv = roped_kv(proj0_ref, g)
            kv_refs[g][...] = kv
            n_col = to_planes(kv)
            for t in range(DEC_SEQ):
                kv_t = jnp.concatenate(
                    [y_ref[c, pl.ds(t, n_seq, stride=DEC_SEQ), :].T for c in range(n_col)], axis=0)
                put_cache_rows(cache_refs[g], t, kv_t, heads_kv[g])
        return

    step, sub = pl.program_id(0), pl.program_id(1)
    tile = jnp.maximum(step - 1, 0)
    last_tile = (tile % tiles_per_seq) == tiles_per_seq - 1

    @pl.when((step == 0) & (sub == 0))
    def _():
        proj1_ref[...] = jnp.zeros_like(proj1_ref)

    def group_step(mine, other, g):
        win, dil = groups[g]
        if g == 0:
            xb_ref[...] = x_ref[...].astype(BF16)
        project(mine, col_q[g], width_q[g])
        q = roped_q(other, g)
        if dil == 1:
            q_refs[g][...] = q.astype(BF16)
        else:
            deinterleave(q, q_refs[g], dil)
        project(mine, col_kv[g], 2 * heads_kv[g] * HEAD_DIM)
        kv = roped_kv(other, g)
        if dil == 1:
            kv_refs[g][...] = kv.astype(BF16)
        else:
            deinterleave(kv, kv_refs[g], dil)
        keep = min(win, SEQ)
        if keep == SEQ:
            put_cache_rows(cache_refs[g], 0, kv.T, heads_kv[g])
        rider(g)
        if keep < SEQ:
            assert keep <= tm

            @pl.when(last_tile)
            def _():
                put_cache_rows(cache_refs[g], 0, roped_kv(other, g)[tm - keep:, :].T, heads_kv[g])

    for par, (mine, other) in enumerate(((proj0_ref, proj1_ref), (proj1_ref, proj0_ref))):
        for g in range(len(groups)):
            @pl.when((step % 2 == par) & (sub == g))
            def _(mine=mine, other=other, g=g):
                group_step(mine, other, g)


_N_SAMPLE_ATTN_IN = 13


def _qkv_call(x2d, w_qkv, cos_t, sin_t, *, tm, prompt, tiles_per_seq, name, sample_attn_args=()):
    m = x2d.shape[0]
    n_tiles = m // tm
    n_groups = 1 + len(B_DIL)
    if prompt:
        tile = lambda i, g: jnp.maximum(i - 1, 0)
        x_spec = pl.BlockSpec((tm, D_MODEL), lambda i, g: (jnp.minimum(i, n_tiles - 1), 0))
    else:
        assert n_tiles == 1
        tile = lambda i: i
        x_spec = pl.BlockSpec((tm, D_MODEL), lambda i: (i, 0))
    row = lambda w: pl.BlockSpec((tm, w), lambda *ij: (tile(*ij), 0))
    tab = pl.BlockSpec((tm, LANES), lambda *ij: (tile(*ij) % tiles_per_seq, 0))
    in_specs = [x_spec, _const_spec((D_MODEL, N_QKV)), tab, tab]
    sds = jax.ShapeDtypeStruct
    out_shape, out_specs = [], []
    for g, (win, dil) in enumerate(((A_WINDOW, 1),) + B_DIL):
        wq = A_Q if g == 0 else B_G
        wkv = 2 * (A_KV if g == 0 else B_G)
        if not prompt:
            out_shape += [sds((m, wq), BF16), sds((DEC_SEQ, wkv, m // DEC_SEQ), F32), sds((m, wkv), F32)]
            out_specs += [row(wq), pl.BlockSpec((DEC_SEQ, wkv, m // DEC_SEQ), lambda i: (0, 0, 0)),
                          row(wkv)]
            continue
        n_seq = m // SEQ
        keep = min(win, SEQ)
        if keep == SEQ:
            cache_spec = pl.BlockSpec(
                (1, wkv, tm), lambda i, g: (tile(i, g) // tiles_per_seq, 0, tile(i, g) % tiles_per_seq))
        else:
            cache_spec = pl.BlockSpec((1, wkv, keep), lambda i, g: (tile(i, g) // tiles_per_seq, 0, 0))
        if dil == 1:
            out_shape += [sds((m, wq), BF16), sds((n_seq, wkv, keep), F32), sds((m, wkv), BF16)]
            out_specs += [row(wq), cache_spec, row(wkv)]
        else:
            dspec = lambda w, dil=dil: pl.BlockSpec(
                (1, dil, tm // dil, w),
                lambda i, g: (tile(i, g) // tiles_per_seq, 0, tile(i, g) % tiles_per_seq, 0))
            out_shape += [sds((n_seq, dil, SEQ // dil, wq), BF16), sds((n_seq, wkv, keep), F32),
                          sds((n_seq, dil, SEQ // dil, wkv), BF16)]
            out_specs += [dspec(wq), cache_spec, dspec(wkv)]
    args = [x2d, w_qkv, cos_t, sin_t]
    if prompt:
        assert len(sample_attn_args) == _N_SAMPLE_ATTN_IN
        tt = n_groups * DEC_SEQ
        n_blocks = sample_attn_args[1].shape[0] // tt
        assert n_blocks <= n_tiles + 1
        blk = lambda i: jnp.minimum(i, n_blocks - 1)
        in_specs.append(pl.BlockSpec(memory_space=pltpu.SMEM))
        for k, a in enumerate(sample_attn_args[1:]):
            if k % 3 == 2:
                in_specs.append(pl.BlockSpec((1,) + a.shape[1:], lambda i, g: (n_groups * blk(i) + g, 0, 0)))
            else:
                in_specs.append(pl.BlockSpec((tt, a.shape[1]), lambda i, g: (blk(i), 0)))
        args += list(sample_attn_args)
        n_tok = sample_attn_args[1].shape[0]
        out_shape += [sds((n_tok, A_Q), BF16), sds((n_tok, B_G), BF16)]
        out_specs += [pl.BlockSpec((tt, A_Q), lambda i, g: (blk(i), 0)),
                      pl.BlockSpec((tt, B_G), lambda i, g: (blk(i), 0))]
    proj_bufs = 2 if prompt else 1
    return pl.pallas_call(
        functools.partial(_qkv_kernel, tm=tm, prompt=prompt, tiles_per_seq=tiles_per_seq),
        grid=(n_tiles + 1, n_groups) if prompt else (n_tiles,),
        in_specs=in_specs,
        out_specs=out_specs,
        out_shape=out_shape,
        scratch_shapes=[pltpu.VMEM((tm, D_MODEL), BF16),
                        pltpu.VMEM((2 * B_G // LANES, tm, LANES), F32)]
        + [pltpu.VMEM((tm, N_QKV) if k < proj_bufs else (8, LANES), F32) for k in range(2)],
        compiler_params=pltpu.CompilerParams(
            dimension_semantics=("arbitrary",) * (2 if prompt else 1), vmem_limit_bytes=VMEM_LIMIT),
        name=name,
    )(*args)


def _a_query_masks(rows):
    lane = lax.broadcasted_iota(jnp.int32, (rows, LANES), 1)
    return [jnp.where(((lane >> 5) & (A_KV_HEADS - 1)) == j, 1.0, 0.0).astype(BF16)
            for j in range(A_KV_HEADS)]


def _fold_masks():
    row = lax.broadcasted_iota(jnp.int32, (BLK, BLK), 0)
    col = lax.broadcasted_iota(jnp.int32, (BLK, BLK), 1)
    return col > row


def _attn_scratch(heads):
    return [pltpu.VMEM((2, heads * BLK, 2 * BLK), F32), pltpu.VMEM((2, heads * BLK, 2 * BLK), BF16),
            pltpu.VMEM((2, heads * BLK, LANES), F32), pltpu.VMEM((2, heads * BLK, LANES), F32)]


def _attn_pipeline(n_units, bufs, *, heads, load_q, load_k, load_v, has_prev, sink_of_head, finish,
                   mxu_row_sum=False):
    sbuf, pbuf, mbuf, lbuf = bufs
    upper = _fold_masks()
    upper_bf = jnp.where(upper, 1.0, 0.0).astype(BF16)
    width = BLK if has_prev is None else 2 * BLK
    assert n_units % 2 == 0

    sbuf[1] = jnp.zeros(sbuf.shape[1:], F32)
    pbuf[0] = jnp.zeros(pbuf.shape[1:], BF16)
    mbuf[0] = jnp.zeros(mbuf.shape[1:], F32)
    lbuf[0] = jnp.ones(lbuf.shape[1:], F32)

    def scores(u, par):
        sbuf[par, :, 0:width] = _nt_dot(load_q(u), load_k(u))

    def softmax(u, par):
        prev_ok = None if has_prev is None else has_prev(u)
        for h in range(heads):
            rows = slice(h * BLK, (h + 1) * BLK)
            if has_prev is None:
                sf = jnp.where(upper, NEG, sbuf[par, rows, 0:BLK])
            else:
                sf = jnp.where(upper, jnp.where(prev_ok, sbuf[par, rows, 0:BLK], NEG),
                               sbuf[par, rows, BLK:2 * BLK])
            mx = jnp.max(sf, axis=-1, keepdims=True)
            if sink_of_head is not None:
                mx = jnp.maximum(mx, sink_of_head(h))
            e = jnp.exp(sf - mx)
            if not mxu_row_sum:
                lbuf[par, rows, :] = jnp.broadcast_to(jnp.sum(e, axis=-1, keepdims=True), (BLK, LANES))
            e = e.astype(BF16)
            if has_prev is None:
                pbuf[par, rows, 0:BLK] = e
            else:
                p_prev = e * upper_bf
                pbuf[par, rows, 0:BLK] = p_prev
                pbuf[par, rows, BLK:2 * BLK] = e - p_prev
            mbuf[par, rows, :] = jnp.broadcast_to(mx, (BLK, LANES))

    def values(u, par):
        finish(u, _dot(pbuf[par, :, 0:width], load_v(u)), mbuf[par], lbuf[par])

    last = n_units - 1

    def trip_pair(t, carry):
        for par in (0, 1):
            i = 2 * t + par
            scores(jnp.minimum(i, last), par)
            softmax(jnp.clip(i - 1, 0, last), 1 - par)
            values(jnp.clip(i - 2, 0, last), par)
        return carry

    lax.fori_loop(0, n_units // 2 + 1, trip_pair, 0)


def _attn_a_kernel(sink_ref, q_ref, kv_ref, o_ref, *bufs):
    n_blk = q_ref.shape[0] // BLK
    lane = lax.broadcasted_iota(jnp.int32, (BLK, LANES), 1)
    hi = lane >= HEAD_DIM
    mask_bf = _a_query_masks(BLK)

    def rows_of(b):
        return pl.ds(pl.multiple_of(b * BLK, BLK), BLK)

    def prev_cur(b, cols):
        return jnp.concatenate([kv_ref[rows_of(jnp.maximum(b - 1, 0)), cols], kv_ref[rows_of(b), cols]],
                               axis=0)

    def load_q(b):
        qblk = q_ref[rows_of(b), :]
        return jnp.concatenate([qblk[:, g * LANES:(g + 1) * LANES] * mask_bf[j]
                                for j in range(A_KV_HEADS) for g in range(A_GROUP)], axis=0)

    ones = jnp.ones((2 * BLK, LANES), BF16)

    def load_v(b):
        return jnp.concatenate([prev_cur(b, slice(A_KV, 2 * A_KV)), ones], axis=1)

    def finish(b, o, m, _):
        def normalised(h):
            rows = slice(h * BLK, (h + 1) * BLK)
            den = o[rows, LANES:] + jnp.exp(sink_ref[h] - m[rows])
            return o[rows, :LANES] / den

        for g in range(A_GROUP):
            o_ref[rows_of(b), g * LANES:(g + 1) * LANES] = jnp.where(
                hi, normalised(A_GROUP + g), normalised(g)).astype(BF16)

    _attn_pipeline(n_blk, bufs, heads=A_Q_HEADS, load_q=load_q,
                   load_k=lambda b: prev_cur(b, slice(0, A_KV)), load_v=load_v,
                   has_prev=lambda b: b > 0, sink_of_head=lambda h: sink_ref[h], finish=finish,
                   mxu_row_sum=True)


def _attn_a_call(sink, q_a, kva_bf):
    n_seq = q_a.shape[0] // SEQ
    return pl.pallas_call(
        _attn_a_kernel,
        grid=(n_seq,),
        in_specs=[pl.BlockSpec(memory_space=pltpu.SMEM),
                  pl.BlockSpec((SEQ, A_Q), lambda n: (n, 0)),
                  pl.BlockSpec((SEQ, 2 * A_KV), lambda n: (n, 0))],
        out_specs=pl.BlockSpec((SEQ, A_Q), lambda n: (n, 0)),
        out_shape=jax.ShapeDtypeStruct(q_a.shape, BF16),
        scratch_shapes=_attn_scratch(A_Q_HEADS),
        compiler_params=pltpu.CompilerParams(
            dimension_semantics=("arbitrary",), vmem_limit_bytes=VMEM_LIMIT),
        name="attn_a_prompt",
    )(sink, q_a, kva_bf)


def _slot_masks(rows):
    lane = lax.broadcasted_iota(jnp.int32, (rows, B_G), 1)
    masks = [(lane >> 6) == s for s in range(B_SLOTS)]
    masks_bf = [jnp.where(((lane >> 5) & (B_SLOTS - 1)) == s, 1.0, 0.0).astype(BF16) for s in range(B_SLOTS)]
    return masks, masks_bf


def _attn_b_kernel(q1_ref, kv1_ref, q2_ref, kv2_ref, q3_ref, kv3_ref, o_ref,
                   acc2_ref, m2_ref, l2_ref, acc3_ref, m3_ref, l3_ref, *bufs):
    masks, masks_bf = _slot_masks(BLK)
    d2, d3 = B_DIL[1][1], B_DIL[2][1]
    nb2 = SEQ // d2 // BLK
    assert SEQ // d3 == BLK
    n_plane = B_G // LANES

    def put(ref, rows, val):
        for c in range(n_plane):
            ref[c, rows, :] = val[:, c * LANES:(c + 1) * LANES]

    def get(ref, rows):
        return jnp.concatenate([ref[c, rows, :] for c in range(n_plane)], axis=1)

    def stack_q(qblk):
        return jnp.concatenate([qblk * masks_bf[s] for s in range(B_SLOTS)], axis=0)

    def unstack(o, m, l):
        acc = jnp.where(masks[0], o[0:BLK], 0.0)
        rep = lambda x, s: jnp.concatenate([x[s * BLK:(s + 1) * BLK]] * n_plane, axis=1)
        mf, lf = rep(m, 0), rep(l, 0)
        for s in range(1, B_SLOTS):
            acc = jnp.where(masks[s], o[s * BLK:(s + 1) * BLK], acc)
            mf = jnp.where(masks[s], rep(m, s), mf)
            lf = jnp.where(masks[s], rep(l, s), lf)
        return acc, mf, lf

    def rows_of(b):
        return pl.ds(pl.multiple_of(b * BLK, BLK), BLK)

    run = functools.partial(_attn_pipeline, bufs=bufs, heads=B_SLOTS, sink_of_head=None)

    def finish3(r, o, m, l):
        rows = pl.ds(r, BLK, stride=d3)
        for ref, val in zip((acc3_ref, m3_ref, l3_ref), unstack(o, m, l)):
            put(ref, rows, val)

    run(d3, load_q=lambda r: stack_q(q3_ref[r]), load_k=lambda r: kv3_ref[r, :, 0:B_G],
        load_v=lambda r: kv3_ref[r, :, B_G:2 * B_G], has_prev=None, finish=finish3)

    def prev_cur2(u, cols):
        r, b = u // nb2, u % nb2
        return jnp.concatenate([kv2_ref[r, rows_of(jnp.maximum(b - 1, 0)), cols], kv2_ref[r, rows_of(b), cols]],
                               axis=0)

    def finish2(u, o, m, l):
        r, b = u // nb2, u % nb2
        rows = pl.ds(r + b * (BLK * d2), BLK, stride=d2)
        for ref, val in zip((acc2_ref, m2_ref, l2_ref), unstack(o, m, l)):
            put(ref, rows, val)

    run(d2 * nb2, load_q=lambda u: stack_q(q2_ref[u // nb2, rows_of(u % nb2), :]),
        load_k=lambda u: prev_cur2(u, slice(0, B_G)), load_v=lambda u: prev_cur2(u, slice(B_G, 2 * B_G)),
        has_prev=lambda u: (u % nb2) > 0, finish=finish2)

    def prev_cur1(b, cols):
        return jnp.concatenate([kv1_ref[rows_of(jnp.maximum(b - 1, 0)), cols], kv1_ref[rows_of(b), cols]],
                               axis=0)

    def finish1(b, o, m, l):
        acc1, m1, l1 = unstack(o, m, l)
        rows = rows_of(b)
        m2, m3 = get(m2_ref, rows), get(m3_ref, rows)
        mx = jnp.maximum(jnp.maximum(m1, m2), m3)
        w1, w2, w3 = jnp.exp(m1 - mx), jnp.exp(m2 - mx), jnp.exp(m3 - mx)
        num = w1 * acc1 + w2 * get(acc2_ref, rows) + w3 * get(acc3_ref, rows)
        den = w1 * l1 + w2 * get(l2_ref, rows) + w3 * get(l3_ref, rows)
        o_ref[rows, :] = (num / den).astype(BF16)

    run(SEQ // BLK, load_q=lambda b: stack_q(q1_ref[rows_of(b), :]),
        load_k=lambda b: prev_cur1(b, slice(0, B_G)), load_v=lambda b: prev_cur1(b, slice(B_G, 2 * B_G)),
        has_prev=lambda b: b > 0, finish=finish1)


def _attn_b_call(q1, kv1, q2, kv2, q3, kv3):
    n_seq = q1.shape[0] // SEQ
    d2, d3 = B_DIL[1][1], B_DIL[2][1]
    rows = lambda w: pl.BlockSpec((SEQ, w), lambda n: (n, 0))
    dsp = lambda d, w: pl.BlockSpec((None, d, SEQ // d, w), lambda n: (n, 0, 0, 0))
    return pl.pallas_call(
        _attn_b_kernel,
        grid=(n_seq,),
        in_specs=[rows(B_G), rows(2 * B_G), dsp(d2, B_G), dsp(d2, 2 * B_G), dsp(d3, B_G),
                  dsp(d3, 2 * B_G)],
        out_specs=rows(B_G),
        out_shape=jax.ShapeDtypeStruct((q1.shape[0], B_G), BF16),
        scratch_shapes=[pltpu.VMEM((B_G // LANES, SEQ, LANES), F32) for _ in range(6)]
        + _attn_scratch(B_SLOTS),
        compiler_params=pltpu.CompilerParams(
            dimension_semantics=("arbitrary",), vmem_limit_bytes=VMEM_LIMIT),
        name="attn_b_prompt",
    )(q1, kv1, q2, kv2, q3, kv3)


def _sample_attn_block(sink_ref, qa_ref, kna_ref, ca_ref, q1_ref, kn1_ref, c1_ref,
                       q2_ref, kn2_ref, c2_ref, q3_ref, kn3_ref, c3_ref, *, tt, seqs):
    pad = BLK - tt
    tt_shift, seq_shift = tt.bit_length() - 1, DEC_SEQ.bit_length() - 1
    assert tt == 1 << tt_shift and DEC_SEQ == 1 << seq_shift

    def pad_rows(x):
        return jnp.concatenate([x.astype(BF16), jnp.zeros((pad, x.shape[1]), BF16)], axis=0)

    def window_group(qm, kn_ref, c_ref, dil, sinkv, finish):
        rows, kd = qm.shape
        lc = c_ref.shape[2]
        kn = kn_ref[...]
        knpad = pad_rows(kn[:, :kd])
        vnpad = pad_rows(kn[:, kd:])
        s_new = _nt_dot(qm, knpad)
        def masks(width):
            col = lax.broadcasted_iota(jnp.int32, (rows, width), 1)
            t = lax.broadcasted_iota(jnp.int32, (rows, width), 0) & (DEC_SEQ - 1)
            same_res = (col & (dil - 1)) == (t & (dil - 1))
            return (col > t) & same_res, (col <= t) & same_res

        valid, _ = masks(lc)
        _, is_new = masks(BLK)
        k_rows = sorted(_k_feature_rows(kd // HEAD_DIM), key=lambda rows: rows[1])
        scores = []
        for pos, _ in seqs:
            k_t = jnp.concatenate([c_ref[pos, ref_row:ref_row + HALF, :] for ref_row, _ in k_rows],
                                  axis=0).astype(BF16)
            scores.append(_dot(qm, k_t))
        yield
        probs = []
        for score, (_, n) in zip(scores, seqs):
            s_c = jnp.where(valid, score, NEG)
            s_n = pltpu.roll(s_new, BLK - DEC_SEQ * n, 1) if n else s_new
            first = jnp.where(is_new, s_n, s_c[:, :BLK])
            s = first if lc == BLK else jnp.concatenate([first, s_c[:, BLK:]], axis=1)
            mx = jnp.max(s, axis=-1, keepdims=True)
            if sinkv is not None:
                mx = jnp.maximum(mx, sinkv)
            e = jnp.exp(s - mx)
            l = jnp.sum(e, axis=-1, keepdims=True)
            if sinkv is not None:
                l = l + jnp.exp(sinkv - mx)
            p_new = jnp.where(is_new, e[:, :BLK], 0.0)
            if n:
                p_new = pltpu.roll(p_new, DEC_SEQ * n, 1)
            probs.append((jnp.where(valid, e, 0.0).astype(BF16), p_new.astype(BF16), mx, l))
        yield
        for (p_c, p_new, mx, l), (pos, n) in zip(probs, seqs):
            v_t = c_ref[pos, kd:2 * kd, :].astype(BF16)
            finish(n, _nt_dot(p_c, v_t) + _dot(p_new, vnpad), mx, l)

    rows_a = A_Q_HEADS * tt
    lane_t = lax.broadcasted_iota(jnp.int32, (tt, LANES), 1)
    hi_t = lane_t >= HEAD_DIM
    mask_bf = _a_query_masks(tt)
    qa = qa_ref[...]
    qm = jnp.concatenate([qa[:, g * LANES:(g + 1) * LANES] * mask_bf[j]
                          for j in range(A_KV_HEADS) for g in range(A_GROUP)], axis=0)
    head = lax.broadcasted_iota(jnp.int32, (rows_a, 1), 0) >> tt_shift
    sinkv = jnp.zeros((rows_a, 1), F32)
    for h in range(A_Q_HEADS):
        sinkv = jnp.where(head == h, sink_ref[h], sinkv)
    seq_t = lax.broadcasted_iota(jnp.int32, (tt, LANES), 0) >> seq_shift
    out_a = [jnp.zeros((tt, LANES), F32) for _ in range(A_GROUP)]

    def finish_a(n, o, mx, l):
        o = o / l
        for g in range(A_GROUP):
            og = jnp.where(hi_t, o[(A_GROUP + g) * tt:(A_GROUP + g + 1) * tt], o[g * tt:(g + 1) * tt])
            out_a[g] = jnp.where(seq_t == n, og, out_a[g])

    stages = [window_group(qm, kna_ref, ca_ref, 1, sinkv, finish_a)]

    masks_t, masks_bf_t = _slot_masks(tt)
    tok_t = lax.broadcasted_iota(jnp.int32, (tt, B_G), 0)

    def stack_q(q):
        return jnp.concatenate([q * masks_bf_t[s] for s in range(B_SLOTS)], axis=0)

    def unstack(o, mx, l):
        acc = jnp.where(masks_t[0], o[0:tt], 0.0)
        mf = jnp.broadcast_to(mx[0:tt], (tt, B_G))
        lf = jnp.broadcast_to(l[0:tt], (tt, B_G))
        for s in range(1, B_SLOTS):
            acc = jnp.where(masks_t[s], o[s * tt:(s + 1) * tt], acc)
            mf = jnp.where(masks_t[s], mx[s * tt:(s + 1) * tt], mf)
            lf = jnp.where(masks_t[s], l[s * tt:(s + 1) * tt], lf)
        return acc, mf, lf

    zero, one = jnp.zeros((tt, B_G), F32), jnp.ones((tt, B_G), F32)
    stats = []
    for (win, dil), q_ref, kn_ref, c_ref in zip(B_DIL, (q1_ref, q2_ref, q3_ref),
                                                (kn1_ref, kn2_ref, kn3_ref), (c1_ref, c2_ref, c3_ref)):
        assert c_ref.shape[2] == win
        group = [zero, zero, one]

        def finish_b(n, o, mx, l, group=group):
            sel = (tok_t >> seq_shift) == n
            for k, new in enumerate(unstack(o, mx, l)):
                group[k] = jnp.where(sel, new, group[k])

        stages.append(window_group(stack_q(q_ref[...]), kn_ref, c_ref, dil, None, finish_b))
        stats.append(group)

    for _ in range(3):
        for stage in stages:
            next(stage, None)
    (acc1, m1, l1), (acc2, m2, l2), (acc3, m3, l3) = stats
    mx = jnp.maximum(jnp.maximum(m1, m2), m3)
    w1, w2, w3 = jnp.exp(m1 - mx), jnp.exp(m2 - mx), jnp.exp(m3 - mx)
    return out_a, (w1 * acc1 + w2 * acc2 + w3 * acc3) / (w1 * l1 + w2 * l2 + w3 * l3)


def _layernorm(x, g, b):
    mu = jnp.mean(x, axis=-1, keepdims=True)
    xc = x - mu
    var = jnp.mean(xc * xc, axis=-1, keepdims=True)
    return xc * lax.rsqrt(var + LN_EPS) * g + b


def _gelu_exact(x):
    return 0.5 * x * (1.0 + lax.erf(x * (0.5 ** 0.5)))


_FF_CHUNK = 256
_CARRY = 8


def _ffn_kernel(*refs, tm, sample, tiles_per_seq):
    if sample:
        (x_ref, oa_ref, ob_ref, fill_ref, wg_ref, wa_ref, wb_ref, wo_ref, g1_ref, b1_ref,
         wup_ref, cw_ref, cb_ref, wdn_ref, g2_ref, b2_ref, y_ref, u_ref,
         xb_ref, m_ref, gg_ref, ext_ref, h0_ref, hb0_ref) = refs
    else:
        (x_ref, oa_ref, ob_ref, wg_ref, wa_ref, wb_ref, wo_ref, g1_ref, b1_ref,
         wup_ref, cw_ref, cb_ref, wdn_ref, g2_ref, b2_ref, y_ref, ulast_ref,
         xb_ref, m_ref, gg_ref, ext_ref, h0_ref, hb0_ref, carry_ref) = refs
    half = D_MODEL // 2

    def merge_pieces(h_ref, hb_ref):
        def gate_half(c):
            if c == 0:
                xb_ref[...] = x_ref[...].astype(BF16)
            cs = slice(c * half, (c + 1) * half)
            ga = _dot(xb_ref[...], wg_ref[:, c * half:(c + 1) * half])
            gb = _dot(xb_ref[...], wg_ref[:, D_MODEL + c * half:D_MODEL + (c + 1) * half])
            ta = _dot(oa_ref[...], wa_ref[:, cs])
            tb = _dot(ob_ref[...], wb_ref[:, cs])
            m_ref[:, cs] = (jax.nn.sigmoid(ga) * ta + jax.nn.sigmoid(gb) * tb).astype(BF16)

        def out_proj():
            mix = _dot(m_ref[...], wo_ref[...])
            h = _layernorm(ALPHA * x_ref[...] + mix, g1_ref[...], b1_ref[...])
            h_ref[...] = h
            hb_ref[...] = h.astype(BF16)

        return [functools.partial(gate_half, 0), functools.partial(gate_half, 1), out_proj]

    def ffn_pieces(h_ref, hb_ref):
        def chunk(c):
            cs = slice(c * _FF_CHUNK, (c + 1) * _FF_CHUNK)
            u = _dot(hb_ref[...], wup_ref[:, c * _FF_CHUNK:(c + 1) * _FF_CHUNK])
            v = _dot(hb_ref[...], wup_ref[:, D_FF + c * _FF_CHUNK:D_FF + (c + 1) * _FF_CHUNK])
            if sample:
                ext_ref[0:_CARRY, :] = jnp.zeros((_CARRY, _FF_CHUNK), F32)
            else:
                ext_ref[0:_CARRY, :] = carry_ref[:, cs]
            ext_ref[_CARRY:_CARRY + tm, :] = u
            u1 = ext_ref[_CARRY - 1:_CARRY - 1 + tm, :]
            u2 = ext_ref[_CARRY - 2:_CARRY - 2 + tm, :]
            if sample:
                t = lax.broadcasted_iota(jnp.int32, (tm, _FF_CHUNK), 0) & (DEC_SEQ - 1)
                fill = fill_ref[:, cs]
                u1 = jnp.where(t >= 1, u1, pltpu.roll(fill, tm - 1, 0))
                u2 = jnp.where(t >= 2, u2, fill)
                u_ref[:, cs] = u
            else:
                tail = u[tm - _CARRY:tm, :]
                carry_ref[:, cs] = tail
                ulast_ref[0, :, cs] = tail
            a = cb_ref[:, cs] + cw_ref[0:1, cs] * u2 + cw_ref[1:2, cs] * u1 + cw_ref[2:3, cs] * u
            gg_ref[:, cs] = (_gelu_exact(a) * v).astype(BF16)

        def down():
            f = _dot(gg_ref[...], wdn_ref[...])
            y_ref[...] = _layernorm(ALPHA * h_ref[...] + f, g2_ref[...], b2_ref[...])

        return [functools.partial(chunk, c) for c in range(D_FF // _FF_CHUNK)] + [down]

    if not sample:
        @pl.when(pl.program_id(0) % tiles_per_seq == 0)
        def _():
            carry_ref[...] = jnp.zeros_like(carry_ref)

    for piece in merge_pieces(h0_ref, hb0_ref) + ffn_pieces(h0_ref, hb0_ref):
        piece()


def _ffn_call(x2d, oa, ob, weights, *, tm, sample, tiles_per_seq, conv_fill=None, name):
    m = x2d.shape[0]
    n_tiles = m // tm
    sds = jax.ShapeDtypeStruct
    row_in = row_out = lambda w: pl.BlockSpec((tm, w), lambda i: (i, 0))
    in_specs = [row_in(D_MODEL), row_in(A_Q), row_in(B_G)]
    args = [x2d, oa, ob]
    if sample:
        in_specs += [row_in(D_FF)]
        args += [conv_fill]
    in_specs += [_const_spec(w.shape) for w in weights]
    args += list(weights)
    h_bufs = [pltpu.VMEM((tm, D_MODEL), F32), pltpu.VMEM((tm, D_MODEL), BF16)]
    scratch = [pltpu.VMEM((tm, D_MODEL), BF16), pltpu.VMEM((tm, D_MODEL), BF16),
               pltpu.VMEM((tm, D_FF), BF16), pltpu.VMEM((tm + _CARRY, _FF_CHUNK), F32)] + h_bufs
    if sample:
        out_shape = [sds((m, D_MODEL), F32), sds((m, D_FF), F32)]
        out_specs = [row_out(D_MODEL), row_out(D_FF)]
    else:
        out_shape = [sds((m, D_MODEL), F32), sds((n_tiles, _CARRY, D_FF), F32)]
        out_specs = [row_out(D_MODEL), pl.BlockSpec((1, _CARRY, D_FF), lambda i: (i, 0, 0))]
        scratch += [pltpu.VMEM((_CARRY, D_FF), F32)]
    return pl.pallas_call(
        functools.partial(_ffn_kernel, tm=tm, sample=sample, tiles_per_seq=tiles_per_seq),
        grid=(n_tiles,),
        in_specs=in_specs,
        out_specs=out_specs,
        out_shape=out_shape,
        scratch_shapes=scratch,
        compiler_params=pltpu.CompilerParams(
            dimension_semantics=("arbitrary",), vmem_limit_bytes=VMEM_LIMIT),
        name=name,
    )(*args)


def _rope_tables(pos):
    half = HEAD_DIM // 2
    inv = ROPE_THETA ** (-jnp.arange(half, dtype=F32) / half)
    ang = pos.astype(F32)[:, None] * inv[None, :]
    cos, sin = jnp.cos(ang), jnp.sin(ang)
    reps = LANES // HALF
    return jnp.tile(cos, (1, reps)), jnp.tile(sin, (1, reps))


def _rotary_order(w, heads):
    x1 = [w[:, h * HEAD_DIM:h * HEAD_DIM + HALF] for h in heads]
    x2 = [w[:, h * HEAD_DIM + HALF:(h + 1) * HEAD_DIM] for h in heads]
    return x1 + x2


def _prep_qkv_weight(w_in):
    qa = w_in[:, 0:A_Q]
    ka = w_in[:, A_Q:A_Q + A_KV]
    va = w_in[:, A_Q + A_KV:A_Q + 2 * A_KV]
    b0 = A_Q + 2 * A_KV
    qb = w_in[:, b0:b0 + 3 * B_G]
    kb = w_in[:, b0 + 3 * B_G:b0 + 6 * B_G]
    vb = w_in[:, b0 + 6 * B_G:b0 + 9 * B_G]
    cols = []
    for c in range(A_GROUP):
        cols += _rotary_order(qa, (c, A_GROUP + c))
    cols += _rotary_order(ka, range(A_KV_HEADS)) + [va]
    for g in range(len(B_DIL)):
        gs = slice(g * B_G, (g + 1) * B_G)
        cols += _rotary_order(qb[:, gs], range(B_SLOTS)) + _rotary_order(kb[:, gs], range(B_SLOTS)) + [vb[:, gs]]
    return jnp.concatenate(cols, axis=1).astype(BF16)


def _prep_br_a(w_br_a):
    rows = []
    for c in range(A_GROUP):
        rows += [w_br_a[c * HEAD_DIM:(c + 1) * HEAD_DIM],
                 w_br_a[(A_GROUP + c) * HEAD_DIM:(A_GROUP + c + 1) * HEAD_DIM]]
    return jnp.concatenate(rows, axis=0).astype(BF16)


_TM_PROMPT = 512


def kernel(x_prompt, x_sample, cache_a, cache_b1, cache_b2, cache_b3, state_conv, w_in, sink_a, w_br_a, w_br_b, w_o, ln1_g, ln1_b, w_up, conv_w, conv_b, w_down, ln2_g, ln2_b):
    assert DEPTH == 1
    l = 0
    w_qkv = _prep_qkv_weight(w_in[l])
    w_gate = w_in[l][:, N_QKV:].astype(BF16)
    ffn_weights = (w_gate, _prep_br_a(w_br_a[l]), w_br_b[l].astype(BF16), w_o[l].astype(BF16),
                   ln1_g[l][None], ln1_b[l][None], w_up[l].astype(BF16), conv_w[l], conv_b[l][None],
                   w_down[l].astype(BF16), ln2_g[l][None], ln2_b[l][None])
    sink = sink_a[l].astype(F32)

    ms = DEC_BATCH * DEC_SEQ
    xs = x_sample.reshape(ms, D_MODEL)
    cos_s, sin_s = _rope_tables(PAST_LEN + (jnp.arange(ms, dtype=jnp.int32) % DEC_SEQ))
    (qa_s, ca_s, kva_s, qb1_s, cb1_s, kvb1_s, qb2_s, cb2_s, kvb2_s, qb3_s, cb3_s, kvb3_s) = _qkv_call(
        xs, w_qkv, cos_s, sin_s, tm=ms, prompt=False, tiles_per_seq=1, name="qkv_sample")

    def window_buffer(c):
        return c.transpose(0, 2, 3, 4, 1).reshape(c.shape[0], -1, c.shape[1])

    ca = window_buffer(cache_a[l])
    cb = [window_buffer(c[l]) for c in (cache_b1, cache_b2, cache_b3)]
    sample_attn_args = (sink, qa_s, kva_s, ca, qb1_s, kvb1_s, cb[0], qb2_s, kvb2_s, cb[1], qb3_s, kvb3_s, cb[2])

    mp = BATCH * SEQ
    xp = x_prompt.reshape(mp, D_MODEL)
    cos_p, sin_p = _rope_tables(jnp.arange(SEQ, dtype=jnp.int32))
    tps = SEQ // _TM_PROMPT
    (qa, ca_p, kva_bf, qb1, cb1_p, kvb1_bf, qb2, cb2_p, kvb2_bf, qb3, cb3_p, kvb3_bf, oa_s, ob_s) = _qkv_call(
        xp, w_qkv, cos_p, sin_p, tm=_TM_PROMPT, prompt=True, tiles_per_seq=tps, name="qkv_prompt",
        sample_attn_args=sample_attn_args)
    oa = _attn_a_call(sink, qa, kva_bf)
    ob = _attn_b_call(qb1, kvb1_bf, qb2, kvb2_bf, qb3, kvb3_bf)
    y_p, ulast = _ffn_call(xp, oa, ob, ffn_weights, tm=_TM_PROMPT, sample=False, tiles_per_seq=tps,
                           name="ffn_prompt")

    def prompt_cache(c, heads):
        return c.reshape(BATCH, 2, heads, HEAD_DIM, c.shape[-1]).transpose(0, 4, 1, 2, 3)[None]

    y_prompt = y_p.reshape(BATCH, SEQ, D_MODEL)
    cache_a_prompt = prompt_cache(ca_p, A_KV_HEADS)
    kvb_p = [prompt_cache(c, B_SLOTS) for c in (cb1_p, cb2_p, cb3_p)]
    state_conv_prompt = ulast.reshape(BATCH, tps, _CARRY, D_FF)[None, :, tps - 1, _CARRY - (CONV_W - 1):]

    fill = jnp.pad(state_conv[l], ((0, 0), (0, DEC_SEQ - (CONV_W - 1)), (0, 0))).reshape(ms, D_FF)
    y_s, u_s = _ffn_call(xs, oa_s, ob_s, ffn_weights, tm=ms, sample=True, tiles_per_seq=1,
                         conv_fill=fill, name="ffn_sample")

    def sample_cache(c, heads):
        return c.reshape(DEC_SEQ, 2, heads, HEAD_DIM, DEC_BATCH).transpose(4, 0, 1, 2, 3)[None]

    y_sample = y_s.reshape(DEC_BATCH, DEC_SEQ, D_MODEL)
    cache_a_sample = sample_cache(ca_s, A_KV_HEADS)
    kvb_s = [sample_cache(c, B_SLOTS) for c in (cb1_s, cb2_s, cb3_s)]
    state_conv_sample = u_s.reshape(DEC_BATCH, DEC_SEQ, D_FF)[None, :, DEC_SEQ - (CONV_W - 1):]

    return (y_prompt, y_sample, cache_a_prompt, cache_a_sample, kvb_p[0], kvb_s[0], kvb_p[1], kvb_s[1],
            kvb_p[2], kvb_s[2], state_conv_prompt, state_conv_sample)
```

```python
import functools

import jax
import jax.numpy as jnp
from jax import lax
from jax.experimental import pallas as pl
from jax.experimental.pallas import tpu as pltpu

D_MODEL = 1024
BATCH = 8
SEQ = 2048
DEPTH = 1
DEC_BATCH = 128
DEC_SEQ = 4
PAST_LEN = 16384
HEAD_DIM = 64
A_Q_HEADS = 8
A_KV_HEADS = 2
A_GROUP = A_Q_HEADS // A_KV_HEADS
A_WINDOW = 128
B_DIL = ((128, 1), (512, 4), (2048, 16))
B_SLOTS = 4
BLK = 128
ROPE_THETA = 10000.0
D_FF = ((8 * D_MODEL // 3 + 127) // 128) * 128
CONV_W = 3
ALPHA = (2 * DEPTH) ** 0.25
LN_EPS = 1e-5
NEG = -1e30
SCALE = HEAD_DIM ** -0.5
A_Q = A_Q_HEADS * HEAD_DIM
A_KV = A_KV_HEADS * HEAD_DIM
B_G = B_SLOTS * HEAD_DIM
N_QKV = A_Q + 2 * A_KV + 3 * 3 * B_G

LANES = 128
VMEM_LIMIT = 56 * 1024 * 1024

BF16 = jnp.bfloat16
F32 = jnp.float32

_C_QA = 0
_C_KVA = _C_QA + A_Q
_C_QB = (_C_KVA + 2 * A_KV, _C_KVA + 2 * A_KV + 3 * B_G, _C_KVA + 2 * A_KV + 6 * B_G)
_C_KVB = tuple(c + B_G for c in _C_QB)


def _const_spec(shape):
    nd = len(shape)
    return pl.BlockSpec(shape, lambda *_: (0,) * nd, pipeline_mode=pl.Buffered(1))


def _nt_dot(a, b):
    return lax.dot_general(a, b, (((1,), (1,)), ((), ())), preferred_element_type=F32)


def _dot(a, b):
    return jnp.dot(a, b, preferred_element_type=F32)


HALF = HEAD_DIM // 2


def _rope_split(y, cos4, sin4):
    x1, x2 = y[:, :LANES], y[:, LANES:]
    return jnp.concatenate([x1 * cos4 - x2 * sin4, x2 * cos4 + x1 * sin4], axis=1)


def _rope_rot64(y, cos4, sin_signed):
    outs = []
    for j in range(y.shape[1] // LANES):
        yj = y[:, j * LANES:(j + 1) * LANES]
        outs.append(yj * cos4 + pltpu.roll(yj, LANES // 2, 1) * sin_signed)
    return outs[0] if len(outs) == 1 else jnp.concatenate(outs, axis=1)


def _k_feature_rows(n_heads):
    return [(HEAD_DIM * h + HALF * part, n_heads * HALF * part + HALF * h)
            for part in range(2) for h in range(n_heads)]


def _qkv_kernel(x_ref, w_ref, cos_ref, sin_ref, *refs, tm, prompt, tiles_per_seq):
    out_refs, (xb_ref, y_ref, proj0_ref, proj1_ref) = refs[:-4], refs[-4:]
    q_refs, cache_refs, kv_refs = out_refs[0::3], out_refs[1::3], out_refs[2::3]
    groups = ((A_WINDOW, 1),) + B_DIL

    col_q = (_C_QA,) + _C_QB
    col_kv = (_C_KVA,) + _C_KVB
    width_q = (A_Q,) + (B_G,) * len(B_DIL)
    heads_kv = (A_KV_HEADS,) + (B_SLOTS,) * len(B_DIL)

    def project(proj_ref, c0, width):
        proj_ref[:, c0:c0 + width] = _dot(xb_ref[...], w_ref[:, c0:c0 + width])

    def put_cache_rows(cache_ref, lead, kv_t, n_heads):
        wk = n_heads * HEAD_DIM
        for ref_row, our_row in _k_feature_rows(n_heads):
            cache_ref[lead, ref_row:ref_row + HALF, :] = kv_t[our_row:our_row + HALF]
        cache_ref[lead, wk:2 * wk, :] = kv_t[wk:2 * wk]

    def to_planes(val):
        n_col = val.shape[1] // LANES
        for c in range(n_col):
            y_ref[c] = val[:, c * LANES:(c + 1) * LANES]
        return n_col

    def rope(y, n_heads):
        cos = cos_ref[...]
        sin = sin_ref[...]
        if n_heads == B_SLOTS:
            return _rope_split(y, cos, sin)
        lane = lax.broadcasted_iota(jnp.int32, (tm, LANES), 1)
        return _rope_rot64(y, cos, jnp.where(lane < LANES // 2, -sin, sin))

    def roped_q(proj_ref, g):
        return rope(proj_ref[:, col_q[g]:col_q[g] + width_q[g]], heads_kv[g]) * SCALE

    def roped_kv(proj_ref, g):
        wk = heads_kv[g] * HEAD_DIM
        return jnp.concatenate([rope(proj_ref[:, col_kv[g]:col_kv[g] + wk], heads_kv[g]),
                                proj_ref[:, col_kv[g] + wk:col_kv[g] + 2 * wk]], axis=1)

    def deinterleave(val, out_ref, dil):
        n_col = to_planes(val)
        for r in range(dil):
            out_ref[0, r] = jnp.concatenate(
                [y_ref[c, pl.ds(r, tm // dil, stride=dil), :] for c in range(n_col)], axis=1).astype(BF16)

    if not prompt:
        xb_ref[...] = x_ref[...].astype(BF16)
        project(proj0_ref, 0, N_QKV)
        n_seq = tm // DEC_SEQ
        for g in range(len(groups)):
            q_refs[g][...] = roped_q(proj0_ref, g).astype(BF16)
            kv = roped_kv(proj0_ref, g)
            kv_refs[g][...] = kv
            n_col = to_planes(kv)
            for t in range(DEC_SEQ):
                kv_t = jnp.concatenate(
                    [y_ref[c, pl.ds(t, n_seq, stride=DEC_SEQ), :].T for c in range(n_col)], axis=0)
                put_cache_rows(cache_refs[g], t, kv_t, heads_kv[g])
        return

    step = pl.program_id(0)
    tile = jnp.maximum(step - 1, 0)
    last_tile = (tile % tiles_per_seq) == tiles_per_seq - 1

    @pl.when(step == 0)
    def _():
        proj1_ref[...] = jnp.zeros_like(proj1_ref)

    def step_body(mine, other):
        xb_ref[...] = x_ref[...].astype(BF16)
        for g, (win, dil) in enumerate(groups):
            project(mine, col_q[g], width_q[g])
            q = roped_q(other, g)
            if dil == 1:
                q_refs[g][...] = q.astype(BF16)
            else:
                deinterleave(q, q_refs[g], dil)
            project(mine, col_kv[g], 2 * heads_kv[g] * HEAD_DIM)
            kv = roped_kv(other, g)
            if dil == 1:
                kv_refs[g][...] = kv.astype(BF16)
            else:
                deinterleave(kv, kv_refs[g], dil)
            if min(win, SEQ) == SEQ:
                put_cache_rows(cache_refs[g], 0, kv.T, heads_kv[g])

        @pl.when(last_tile)
        def _():
            for g, (win, _) in enumerate(groups):
                keep = min(win, SEQ)
                if keep < SEQ:
                    assert keep <= tm
                    put_cache_rows(cache_refs[g], 0, roped_kv(other, g)[tm - keep:, :].T, heads_kv[g])

    for par, (mine, other) in enumerate(((proj0_ref, proj1_ref), (proj1_ref, proj0_ref))):
        @pl.when(step % 2 == par)
        def _(mine=mine, other=other):
            step_body(mine, other)


def _qkv_call(x2d, w_qkv, cos_t, sin_t, *, tm, prompt, tiles_per_seq, name):
    m = x2d.shape[0]
    n_tiles = m // tm
    if prompt:
        tile = lambda i: jnp.maximum(i - 1, 0)
        x_spec = pl.BlockSpec((tm, D_MODEL), lambda i: (jnp.minimum(i, n_tiles - 1), 0))
    else:
        assert n_tiles == 1
        tile = lambda i: i
        x_spec = pl.BlockSpec((tm, D_MODEL), lambda i: (i, 0))
    row = lambda w: pl.BlockSpec((tm, w), lambda i: (tile(i), 0))
    tab = pl.BlockSpec((tm, LANES), lambda i: (tile(i) % tiles_per_seq, 0))
    in_specs = [x_spec, _const_spec((D_MODEL, N_QKV)), tab, tab]
    sds = jax.ShapeDtypeStruct
    out_shape, out_specs = [], []
    for g, (win, dil) in enumerate(((A_WINDOW, 1),) + B_DIL):
        wq = A_Q if g == 0 else B_G
        wkv = 2 * (A_KV if g == 0 else B_G)
        if not prompt:
            out_shape += [sds((m, wq), BF16), sds((DEC_SEQ, wkv, m // DEC_SEQ), F32), sds((m, wkv), F32)]
            out_specs += [row(wq), pl.BlockSpec((DEC_SEQ, wkv, m // DEC_SEQ), lambda i: (0, 0, 0)),
                          row(wkv)]
            continue
        n_seq = m // SEQ
        keep = min(win, SEQ)
        if keep == SEQ:
            cache_spec = pl.BlockSpec(
                (1, wkv, tm), lambda i: (tile(i) // tiles_per_seq, 0, tile(i) % tiles_per_seq))
        else:
            cache_spec = pl.BlockSpec((1, wkv, keep), lambda i: (tile(i) // tiles_per_seq, 0, 0))
        if dil == 1:
            out_shape += [sds((m, wq), BF16), sds((n_seq, wkv, keep), F32), sds((m, wkv), BF16)]
            out_specs += [row(wq), cache_spec, row(wkv)]
        else:
            dspec = lambda w, dil=dil: pl.BlockSpec(
                (1, dil, tm // dil, w),
                lambda i: (tile(i) // tiles_per_seq, 0, tile(i) % tiles_per_seq, 0))
            out_shape += [sds((n_seq, dil, SEQ // dil, wq), BF16), sds((n_seq, wkv, keep), F32),
                          sds((n_seq, dil, SEQ // dil, wkv), BF16)]
            out_specs += [dspec(wq), cache_spec, dspec(wkv)]
    args = [x2d, w_qkv, cos_t, sin_t]
    proj_bufs = 2 if prompt else 1
    return pl.pallas_call(
        functools.partial(_qkv_kernel, tm=tm, prompt=prompt, tiles_per_seq=tiles_per_seq),
        grid=(n_tiles + 1,) if prompt else (n_tiles,),
        in_specs=in_specs,
        out_specs=out_specs,
        out_shape=out_shape,
        scratch_shapes=[pltpu.VMEM((tm, D_MODEL), BF16),
                        pltpu.VMEM((2 * B_G // LANES, tm, LANES), F32)]
        + [pltpu.VMEM((tm, N_QKV) if k < proj_bufs else (8, LANES), F32) for k in range(2)],
        compiler_params=pltpu.CompilerParams(
            dimension_semantics=("arbitrary",), vmem_limit_bytes=VMEM_LIMIT),
        name=name,
    )(*args)


def _a_query_masks(rows):
    lane = lax.broadcasted_iota(jnp.int32, (rows, LANES), 1)
    return [jnp.where(((lane >> 5) & (A_KV_HEADS - 1)) == j, 1.0, 0.0).astype(BF16)
            for j in range(A_KV_HEADS)]


def _fold_masks():
    row = lax.broadcasted_iota(jnp.int32, (BLK, BLK), 0)
    col = lax.broadcasted_iota(jnp.int32, (BLK, BLK), 1)
    return col > row


def _attn_scratch(heads):
    return [pltpu.VMEM((2, heads * BLK, 2 * BLK), F32), pltpu.VMEM((2, heads * BLK, 2 * BLK), BF16),
            pltpu.VMEM((2, heads * BLK, LANES), F32), pltpu.VMEM((2, heads * BLK, LANES), F32)]


def _attn_pipeline(n_units, bufs, *, heads, load_q, load_k, load_v, has_prev, sink_of_head, finish,
                   mxu_row_sum=False):
    sbuf, pbuf, mbuf, lbuf = bufs
    upper = _fold_masks()
    upper_bf = jnp.where(upper, 1.0, 0.0).astype(BF16)
    width = BLK if has_prev is None else 2 * BLK
    assert n_units % 2 == 0

    sbuf[1] = jnp.zeros(sbuf.shape[1:], F32)
    pbuf[0] = jnp.zeros(pbuf.shape[1:], BF16)
    mbuf[0] = jnp.zeros(mbuf.shape[1:], F32)
    lbuf[0] = jnp.ones(lbuf.shape[1:], F32)

    def scores(u, par):
        sbuf[par, :, 0:width] = _nt_dot(load_q(u), load_k(u))

    def softmax(u, par):
        prev_ok = None if has_prev is None else has_prev(u)
        for h in range(heads):
            rows = slice(h * BLK, (h + 1) * BLK)
            if has_prev is None:
                sf = jnp.where(upper, NEG, sbuf[par, rows, 0:BLK])
            else:
                sf = jnp.where(upper, jnp.where(prev_ok, sbuf[par, rows, 0:BLK], NEG),
                               sbuf[par, rows, BLK:2 * BLK])
            mx = jnp.max(sf, axis=-1, keepdims=True)
            if sink_of_head is not None:
                mx = jnp.maximum(mx, sink_of_head(h))
            e = jnp.exp(sf - mx)
            if not mxu_row_sum:
                lbuf[par, rows, :] = jnp.broadcast_to(jnp.sum(e, axis=-1, keepdims=True), (BLK, LANES))
            e = e.astype(BF16)
            if has_prev is None:
                pbuf[par, rows, 0:BLK] = e
            else:
                p_prev = e * upper_bf
                pbuf[par, rows, 0:BLK] = p_prev
                pbuf[par, rows, BLK:2 * BLK] = e - p_prev
            mbuf[par, rows, :] = jnp.broadcast_to(mx, (BLK, LANES))

    def values(u, par):
        finish(u, _dot(pbuf[par, :, 0:width], load_v(u)), mbuf[par], lbuf[par])

    last = n_units - 1

    def trip_pair(t, carry):
        for par in (0, 1):
            i = 2 * t + par
            scores(jnp.minimum(i, last), par)
            softmax(jnp.clip(i - 1, 0, last), 1 - par)
            values(jnp.clip(i - 2, 0, last), par)
        return carry

    lax.fori_loop(0, n_units // 2 + 1, trip_pair, 0)


def _attn_a_kernel(sink_ref, q_ref, kv_ref, o_ref, *bufs):
    n_blk = q_ref.shape[0] // BLK
    lane = lax.broadcasted_iota(jnp.int32, (BLK, LANES), 1)
    hi = lane >= HEAD_DIM
    mask_bf = _a_query_masks(BLK)

    def rows_of(b):
        return pl.ds(pl.multiple_of(b * BLK, BLK), BLK)

    def prev_cur(b, cols):
        return jnp.concatenate([kv_ref[rows_of(jnp.maximum(b - 1, 0)), cols], kv_ref[rows_of(b), cols]],
                               axis=0)

    def load_q(b):
        qblk = q_ref[rows_of(b), :]
        return jnp.concatenate([qblk[:, g * LANES:(g + 1) * LANES] * mask_bf[j]
                                for j in range(A_KV_HEADS) for g in range(A_GROUP)], axis=0)

    ones = jnp.ones((2 * BLK, LANES), BF16)

    def load_v(b):
        return jnp.concatenate([prev_cur(b, slice(A_KV, 2 * A_KV)), ones], axis=1)

    def finish(b, o, m, _):
        def normalised(h):
            rows = slice(h * BLK, (h + 1) * BLK)
            den = o[rows, LANES:] + jnp.exp(sink_ref[h] - m[rows])
            return o[rows, :LANES] / den

        for g in range(A_GROUP):
            o_ref[rows_of(b), g * LANES:(g + 1) * LANES] = jnp.where(
                hi, normalised(A_GROUP + g), normalised(g)).astype(BF16)

    _attn_pipeline(n_blk, bufs, heads=A_Q_HEADS, load_q=load_q,
                   load_k=lambda b: prev_cur(b, slice(0, A_KV)), load_v=load_v,
                   has_prev=lambda b: b > 0, sink_of_head=lambda h: sink_ref[h], finish=finish,
                   mxu_row_sum=True)


def _attn_a_call(sink, q_a, kva_bf):
    n_seq = q_a.shape[0] // SEQ
    return pl.pallas_call(
        _attn_a_kernel,
        grid=(n_seq,),
        in_specs=[pl.BlockSpec(memory_space=pltpu.SMEM),
                  pl.BlockSpec((SEQ, A_Q), lambda n: (n, 0)),
                  pl.BlockSpec((SEQ, 2 * A_KV), lambda n: (n, 0))],
        out_specs=pl.BlockSpec((SEQ, A_Q), lambda n: (n, 0)),
        out_shape=jax.ShapeDtypeStruct(q_a.shape, BF16),
        scratch_shapes=_attn_scratch(A_Q_HEADS),
        compiler_params=pltpu.CompilerParams(
            dimension_semantics=("arbitrary",), vmem_limit_bytes=VMEM_LIMIT),
        name="attn_a_prompt",
    )(sink, q_a, kva_bf)


def _slot_masks(rows):
    lane = lax.broadcasted_iota(jnp.int32, (rows, B_G), 1)
    masks = [(lane >> 6) == s for s in range(B_SLOTS)]
    masks_bf = [jnp.where(((lane >> 5) & (B_SLOTS - 1)) == s, 1.0, 0.0).astype(BF16) for s in range(B_SLOTS)]
    return masks, masks_bf


def _attn_b_kernel(q1_ref, kv1_ref, q2_ref, kv2_ref, q3_ref, kv3_ref, o_ref,
                   acc2_ref, m2_ref, l2_ref, acc3_ref, m3_ref, l3_ref, *bufs):
    masks, masks_bf = _slot_masks(BLK)
    d2, d3 = B_DIL[1][1], B_DIL[2][1]
    nb2 = SEQ // d2 // BLK
    assert SEQ // d3 == BLK
    n_plane = B_G // LANES

    def put(ref, rows, val):
        for c in range(n_plane):
            ref[c, rows, :] = val[:, c * LANES:(c + 1) * LANES]

    def get(ref, rows):
        return jnp.concatenate([ref[c, rows, :] for c in range(n_plane)], axis=1)

    def stack_q(qblk):
        return jnp.concatenate([qblk * masks_bf[s] for s in range(B_SLOTS)], axis=0)

    def unstack(o, m, l):
        acc = jnp.where(masks[0], o[0:BLK], 0.0)
        rep = lambda x, s: jnp.concatenate([x[s * BLK:(s + 1) * BLK]] * n_plane, axis=1)
        mf, lf = rep(m, 0), rep(l, 0)
        for s in range(1, B_SLOTS):
            acc = jnp.where(masks[s], o[s * BLK:(s + 1) * BLK], acc)
            mf = jnp.where(masks[s], rep(m, s), mf)
            lf = jnp.where(masks[s], rep(l, s), lf)
        return acc, mf, lf

    def rows_of(b):
        return pl.ds(pl.multiple_of(b * BLK, BLK), BLK)

    run = functools.partial(_attn_pipeline, bufs=bufs, heads=B_SLOTS, sink_of_head=None)

    def finish3(r, o, m, l):
        rows = pl.ds(r, BLK, stride=d3)
        for ref, val in zip((acc3_ref, m3_ref, l3_ref), unstack(o, m, l)):
            put(ref, rows, val)

    run(d3, load_q=lambda r: stack_q(q3_ref[r]), load_k=lambda r: kv3_ref[r, :, 0:B_G],
        load_v=lambda r: kv3_ref[r, :, B_G:2 * B_G], has_prev=None, finish=finish3)

    def prev_cur2(u, cols):
        r, b = u // nb2, u % nb2
        return jnp.concatenate([kv2_ref[r, rows_of(jnp.maximum(b - 1, 0)), cols], kv2_ref[r, rows_of(b), cols]],
                               axis=0)

    def finish2(u, o, m, l):
        r, b = u // nb2, u % nb2
        rows = pl.ds(r + b * (BLK * d2), BLK, stride=d2)
        for ref, val in zip((acc2_ref, m2_ref, l2_ref), unstack(o, m, l)):
            put(ref, rows, val)

    run(d2 * nb2, load_q=lambda u: stack_q(q2_ref[u // nb2, rows_of(u % nb2), :]),
        load_k=lambda u: prev_cur2(u, slice(0, B_G)), load_v=lambda u: prev_cur2(u, slice(B_G, 2 * B_G)),
        has_prev=lambda u: (u % nb2) > 0, finish=finish2)

    def prev_cur1(b, cols):
        return jnp.concatenate([kv1_ref[rows_of(jnp.maximum(b - 1, 0)), cols], kv1_ref[rows_of(b), cols]],
                               axis=0)

    def finish1(b, o, m, l):
        acc1, m1, l1 = unstack(o, m, l)
        rows = rows_of(b)
        m2, m3 = get(m2_ref, rows), get(m3_ref, rows)
        mx = jnp.maximum(jnp.maximum(m1, m2), m3)
        w1, w2, w3 = jnp.exp(m1 - mx), jnp.exp(m2 - mx), jnp.exp(m3 - mx)
        num = w1 * acc1 + w2 * get(acc2_ref, rows) + w3 * get(acc3_ref, rows)
        den = w1 * l1 + w2 * get(l2_ref, rows) + w3 * get(l3_ref, rows)
        o_ref[rows, :] = (num / den).astype(BF16)

    run(SEQ // BLK, load_q=lambda b: stack_q(q1_ref[rows_of(b), :]),
        load_k=lambda b: prev_cur1(b, slice(0, B_G)), load_v=lambda b: prev_cur1(b, slice(B_G, 2 * B_G)),
        has_prev=lambda b: b > 0, finish=finish1)


def _attn_b_call(q1, kv1, q2, kv2, q3, kv3):
    n_seq = q1.shape[0] // SEQ
    d2, d3 = B_DIL[1][1], B_DIL[2][1]
    rows = lambda w: pl.BlockSpec((SEQ, w), lambda n: (n, 0))
    dsp = lambda d, w: pl.BlockSpec((None, d, SEQ // d, w), lambda n: (n, 0, 0, 0))
    return pl.pallas_call(
        _attn_b_kernel,
        grid=(n_seq,),
        in_specs=[rows(B_G), rows(2 * B_G), dsp(d2, B_G), dsp(d2, 2 * B_G), dsp(d3, B_G),
                  dsp(d3, 2 * B_G)],
        out_specs=rows(B_G),
        out_shape=jax.ShapeDtypeStruct((q1.shape[0], B_G), BF16),
        scratch_shapes=[pltpu.VMEM((B_G // LANES, SEQ, LANES), F32) for _ in range(6)]
        + _attn_scratch(B_SLOTS),
        compiler_params=pltpu.CompilerParams(
            dimension_semantics=("arbitrary",), vmem_limit_bytes=VMEM_LIMIT),
        name="attn_b_prompt",
    )(q1, kv1, q2, kv2, q3, kv3)


def _sample_attn_block(sink_ref, qa_ref, kna_ref, ca_ref, q1_ref, kn1_ref, c1_ref,
                       q2_ref, kn2_ref, c2_ref, q3_ref, kn3_ref, c3_ref, *, tt, seqs, emit):
    pad = BLK - tt
    tt_shift, seq_shift = tt.bit_length() - 1, DEC_SEQ.bit_length() - 1
    assert tt == 1 << tt_shift and DEC_SEQ == 1 << seq_shift

    def pad_rows(x):
        return jnp.concatenate([x.astype(BF16), jnp.zeros((pad, x.shape[1]), BF16)], axis=0)

    def window_group(qm, kn_ref, c_ref, dil, sinkv, finish):
        rows, kd = qm.shape
        lc = c_ref.shape[2]
        kn = kn_ref[...]
        knpad = pad_rows(kn[:, :kd])
        vnpad = pad_rows(kn[:, kd:])
        s_new = _nt_dot(qm, knpad)
        def masks(width):
            col = lax.broadcasted_iota(jnp.int32, (rows, width), 1)
            t = lax.broadcasted_iota(jnp.int32, (rows, width), 0) & (DEC_SEQ - 1)
            same_res = (col & (dil - 1)) == (t & (dil - 1))
            return (col > t) & same_res, (col <= t) & same_res

        valid, _ = masks(lc)
        _, is_new = masks(BLK)
        k_rows = sorted(_k_feature_rows(kd // HEAD_DIM), key=lambda rows: rows[1])
        scores = []
        for pos, _ in seqs:
            k_t = jnp.concatenate([c_ref[pos, ref_row:ref_row + HALF, :] for ref_row, _ in k_rows],
                                  axis=0).astype(BF16)
            scores.append(_dot(qm, k_t))
        yield
        probs = []
        for score, (_, n) in zip(scores, seqs):
            s_c = jnp.where(valid, score, NEG)
            s_n = pltpu.roll(s_new, BLK - DEC_SEQ * n, 1) if n else s_new
            first = jnp.where(is_new, s_n, s_c[:, :BLK])
            s = first if lc == BLK else jnp.concatenate([first, s_c[:, BLK:]], axis=1)
            mx = jnp.max(s, axis=-1, keepdims=True)
            if sinkv is not None:
                mx = jnp.maximum(mx, sinkv)
            e = jnp.exp(s - mx)
            l = jnp.sum(e, axis=-1, keepdims=True)
            if sinkv is not None:
                l = l + jnp.exp(sinkv - mx)
            p_new = jnp.where(is_new, e[:, :BLK], 0.0)
            if n:
                p_new = pltpu.roll(p_new, DEC_SEQ * n, 1)
            probs.append((jnp.where(valid, e, 0.0).astype(BF16), p_new.astype(BF16), mx, l))
        yield
        for (p_c, p_new, mx, l), (pos, n) in zip(probs, seqs):
            v_t = c_ref[pos, kd:2 * kd, :].astype(BF16)
            finish(n, _nt_dot(p_c, v_t) + _dot(p_new, vnpad), mx, l)

    rows_a = A_Q_HEADS * tt
    lane_t = lax.broadcasted_iota(jnp.int32, (tt, LANES), 1)
    hi_t = lane_t >= HEAD_DIM
    mask_bf = _a_query_masks(tt)
    qa = qa_ref[...]
    qm = jnp.concatenate([qa[:, g * LANES:(g + 1) * LANES] * mask_bf[j]
                          for j in range(A_KV_HEADS) for g in range(A_GROUP)], axis=0)
    head = lax.broadcasted_iota(jnp.int32, (rows_a, 1), 0) >> tt_shift
    sinkv = jnp.zeros((rows_a, 1), F32)
    for h in range(A_Q_HEADS):
        sinkv = jnp.where(head == h, sink_ref[h], sinkv)
    seq_t = lax.broadcasted_iota(jnp.int32, (tt, LANES), 0) >> seq_shift
    out_a = [jnp.zeros((tt, LANES), F32) for _ in range(A_GROUP)]

    def finish_a(n, o, mx, l):
        o = o / l
        for g in range(A_GROUP):
            og = jnp.where(hi_t, o[(A_GROUP + g) * tt:(A_GROUP + g + 1) * tt], o[g * tt:(g + 1) * tt])
            out_a[g] = jnp.where(seq_t == n, og, out_a[g])

    stages = [window_group(qm, kna_ref, ca_ref, 1, sinkv, finish_a)]

    masks_t, masks_bf_t = _slot_masks(tt)
    tok_t = lax.broadcasted_iota(jnp.int32, (tt, B_G), 0)

    def stack_q(q):
        return jnp.concatenate([q * masks_bf_t[s] for s in range(B_SLOTS)], axis=0)

    def unstack(o, mx, l):
        acc = jnp.where(masks_t[0], o[0:tt], 0.0)
        mf = jnp.broadcast_to(mx[0:tt], (tt, B_G))
        lf = jnp.broadcast_to(l[0:tt], (tt, B_G))
        for s in range(1, B_SLOTS):
            acc = jnp.where(masks_t[s], o[s * tt:(s + 1) * tt], acc)
            mf = jnp.where(masks_t[s], mx[s * tt:(s + 1) * tt], mf)
            lf = jnp.where(masks_t[s], l[s * tt:(s + 1) * tt], lf)
        return acc, mf, lf

    zero, one = jnp.zeros((tt, B_G), F32), jnp.ones((tt, B_G), F32)
    stats = []
    for (win, dil), q_ref, kn_ref, c_ref in zip(B_DIL, (q1_ref, q2_ref, q3_ref),
                                                (kn1_ref, kn2_ref, kn3_ref), (c1_ref, c2_ref, c3_ref)):
        assert c_ref.shape[2] == win
        group = [zero, zero, one]

        def finish_b(n, o, mx, l, group=group):
            sel = (tok_t >> seq_shift) == n
            for k, new in enumerate(unstack(o, mx, l)):
                group[k] = jnp.where(sel, new, group[k])

        stages.append(window_group(stack_q(q_ref[...]), kn_ref, c_ref, dil, None, finish_b))
        stats.append(group)

    for phase in range(3):
        for stage in stages:
            next(stage, None)
        if phase < 2:
            yield
    (acc1, m1, l1), (acc2, m2, l2), (acc3, m3, l3) = stats
    mx = jnp.maximum(jnp.maximum(m1, m2), m3)
    w1, w2, w3 = jnp.exp(m1 - mx), jnp.exp(m2 - mx), jnp.exp(m3 - mx)
    emit(out_a, (w1 * acc1 + w2 * acc2 + w3 * acc3) / (w1 * l1 + w2 * l2 + w3 * l3))


def _sample_attn_rider(in_refs, oa_ref, ob_ref, k, n_seqs):
    seq_shift = DEC_SEQ.bit_length() - 1

    def merge(ref, cols, new):
        if k:
            mine = (lax.broadcasted_iota(jnp.int32, new.shape, 0) >> seq_shift) == k
            new = jnp.where(mine, new, ref[:, cols].astype(F32))
        ref[:, cols] = new.astype(BF16)

    def emit(out_a, out_b):
        for c in range(A_GROUP):
            merge(oa_ref, slice(c * LANES, (c + 1) * LANES), out_a[c])
        merge(ob_ref, slice(0, B_G), out_b)

    yield from _sample_attn_block(*in_refs, tt=n_seqs * DEC_SEQ, seqs=[(0, k)], emit=emit)


def _layernorm(x, g, b):
    mu = jnp.mean(x, axis=-1, keepdims=True)
    xc = x - mu
    var = jnp.mean(xc * xc, axis=-1, keepdims=True)
    return xc * lax.rsqrt(var + LN_EPS) * g + b


def _gelu_exact(x):
    return 0.5 * x * (1.0 + lax.erf(x * (0.5 ** 0.5)))


_FF_CHUNK = 256
_CARRY = 8


def _ffn_kernel(*refs, tm, sample, tiles_per_seq):
    if sample:
        (x_ref, oa_ref, ob_ref, fill_ref, wg_ref, wa_ref, wb_ref, wo_ref, g1_ref, b1_ref,
         wup_ref, cw_ref, cb_ref, wdn_ref, g2_ref, b2_ref, y_ref, u_ref,
         xb_ref, m_ref, gg_ref, ext_ref, h0_ref, hb0_ref) = refs
    else:
        x_ref, oa_ref, ob_ref = refs[:3]
        rider_in, refs = refs[3:3 + _N_SAMPLE_ATTN_IN], refs[3 + _N_SAMPLE_ATTN_IN:]
        (wg_ref, wa_ref, wb_ref, wo_ref, g1_ref, b1_ref,
         wup_ref, cw_ref, cb_ref, wdn_ref, g2_ref, b2_ref, y_ref, ulast_ref, oa_s_ref, ob_s_ref,
         xb_ref, m_ref, gg_ref, ext_ref, h0_ref, hb0_ref, carry_ref) = refs
    half = D_MODEL // 2

    def merge_pieces(h_ref, hb_ref):
        def gate_half(c):
            if c == 0:
                xb_ref[...] = x_ref[...].astype(BF16)
            cs = slice(c * half, (c + 1) * half)
            ga = _dot(xb_ref[...], wg_ref[:, c * half:(c + 1) * half])
            gb = _dot(xb_ref[...], wg_ref[:, D_MODEL + c * half:D_MODEL + (c + 1) * half])
            ta = _dot(oa_ref[...], wa_ref[:, cs])
            tb = _dot(ob_ref[...], wb_ref[:, cs])
            m_ref[:, cs] = (jax.nn.sigmoid(ga) * ta + jax.nn.sigmoid(gb) * tb).astype(BF16)

        def out_proj():
            mix = _dot(m_ref[...], wo_ref[...])
            h = _layernorm(ALPHA * x_ref[...] + mix, g1_ref[...], b1_ref[...])
            h_ref[...] = h
            hb_ref[...] = h.astype(BF16)

        return [functools.partial(gate_half, 0), functools.partial(gate_half, 1), out_proj]

    def ffn_pieces(h_ref, hb_ref):
        def chunk(c):
            cs = slice(c * _FF_CHUNK, (c + 1) * _FF_CHUNK)
            u = _dot(hb_ref[...], wup_ref[:, c * _FF_CHUNK:(c + 1) * _FF_CHUNK])
            v = _dot(hb_ref[...], wup_ref[:, D_FF + c * _FF_CHUNK:D_FF + (c + 1) * _FF_CHUNK])
            if sample:
                ext_ref[0:_CARRY, :] = jnp.zeros((_CARRY, _FF_CHUNK), F32)
            else:
                ext_ref[0:_CARRY, :] = carry_ref[:, cs]
            ext_ref[_CARRY:_CARRY + tm, :] = u
            u1 = ext_ref[_CARRY - 1:_CARRY - 1 + tm, :]
            u2 = ext_ref[_CARRY - 2:_CARRY - 2 + tm, :]
            if sample:
                t = lax.broadcasted_iota(jnp.int32, (tm, _FF_CHUNK), 0) & (DEC_SEQ - 1)
                fill = fill_ref[:, cs]
                u1 = jnp.where(t >= 1, u1, pltpu.roll(fill, tm - 1, 0))
                u2 = jnp.where(t >= 2, u2, fill)
                u_ref[:, cs] = u
            else:
                tail = u[tm - _CARRY:tm, :]
                carry_ref[:, cs] = tail
                ulast_ref[0, :, cs] = tail
            a = cb_ref[:, cs] + cw_ref[0:1, cs] * u2 + cw_ref[1:2, cs] * u1 + cw_ref[2:3, cs] * u
            gg_ref[:, cs] = (_gelu_exact(a) * v).astype(BF16)

        def down():
            f = _dot(gg_ref[...], wdn_ref[...])
            y_ref[...] = _layernorm(ALPHA * h_ref[...] + f, g2_ref[...], b2_ref[...])

        return [functools.partial(chunk, c) for c in range(D_FF // _FF_CHUNK)] + [down]

    pieces = merge_pieces(h0_ref, hb0_ref) + ffn_pieces(h0_ref, hb0_ref)
    if sample:
        for piece in pieces:
            piece()
        return

    step, sub = pl.program_id(0), pl.program_id(1)

    @pl.when((step % tiles_per_seq == 0) & (sub == 0))
    def _():
        carry_ref[...] = jnp.zeros_like(carry_ref)

    n_chunk = D_FF // _FF_CHUNK
    phases = (pieces[:3], pieces[3:3 + n_chunk // 2], pieces[3 + n_chunk // 2:3 + n_chunk], pieces[3 + n_chunk:])
    assert len(phases) == _FFN_PHASES
    for g, phase in enumerate(phases):
        @pl.when(sub == g)
        def _(g=g, phase=phase):
            rider = _sample_attn_rider(rider_in, oa_s_ref, ob_s_ref, g, _FFN_PHASES)
            for piece in phase:
                next(rider, None)
                piece()
            for _ in rider:
                pass


_FFN_PHASES = 4
_N_SAMPLE_ATTN_IN = 13


def _ffn_call(x2d, oa, ob, weights, *, tm, sample, tiles_per_seq, conv_fill=None, sample_attn_args=(), name):
    m = x2d.shape[0]
    n_tiles = m // tm
    sds = jax.ShapeDtypeStruct
    row_in = row_out = lambda w: pl.BlockSpec((tm, w), lambda i, *_: (i, 0))
    in_specs = [row_in(D_MODEL), row_in(A_Q), row_in(B_G)]
    args = [x2d, oa, ob]
    if sample:
        in_specs += [row_in(D_FF)]
        args += [conv_fill]
    else:
        assert len(sample_attn_args) == _N_SAMPLE_ATTN_IN
        tt = _FFN_PHASES * DEC_SEQ
        n_tok = sample_attn_args[1].shape[0]
        assert n_tok == n_tiles * tt
        in_specs.append(pl.BlockSpec(memory_space=pltpu.SMEM))
        for k, a in enumerate(sample_attn_args[1:]):
            if k % 3 == 2:
                in_specs.append(pl.BlockSpec((1,) + a.shape[1:], lambda i, g: (_FFN_PHASES * i + g, 0, 0)))
            else:
                in_specs.append(pl.BlockSpec((tt, a.shape[1]), lambda i, g: (i, 0)))
        args += list(sample_attn_args)
    in_specs += [_const_spec(w.shape) for w in weights]
    args += list(weights)
    h_bufs = [pltpu.VMEM((tm, D_MODEL), F32), pltpu.VMEM((tm, D_MODEL), BF16)]
    scratch = [pltpu.VMEM((tm, D_MODEL), BF16), pltpu.VMEM((tm, D_MODEL), BF16),
               pltpu.VMEM((tm, D_FF), BF16), pltpu.VMEM((tm + _CARRY, _FF_CHUNK), F32)] + h_bufs
    if sample:
        out_shape = [sds((m, D_MODEL), F32), sds((m, D_FF), F32)]
        out_specs = [row_out(D_MODEL), row_out(D_FF)]
    else:
        out_shape = [sds((m, D_MODEL), F32), sds((n_tiles, _CARRY, D_FF), F32),
                     sds((n_tok, A_Q), BF16), sds((n_tok, B_G), BF16)]
        out_specs = [row_out(D_MODEL), pl.BlockSpec((1, _CARRY, D_FF), lambda i, g: (i, 0, 0)),
                     pl.BlockSpec((tt, A_Q), lambda i, g: (i, 0)), pl.BlockSpec((tt, B_G), lambda i, g: (i, 0))]
        scratch += [pltpu.VMEM((_CARRY, D_FF), F32)]
    grid = (n_tiles,) if sample else (n_tiles, _FFN_PHASES)
    return pl.pallas_call(
        functools.partial(_ffn_kernel, tm=tm, sample=sample, tiles_per_seq=tiles_per_seq),
        grid=grid,
        in_specs=in_specs,
        out_specs=out_specs,
        out_shape=out_shape,
        scratch_shapes=scratch,
        compiler_params=pltpu.CompilerParams(
            dimension_semantics=("arbitrary",) * len(grid), vmem_limit_bytes=VMEM_LIMIT),
        name=name,
    )(*args)


def _rope_tables(pos):
    half = HEAD_DIM // 2
    inv = ROPE_THETA ** (-jnp.arange(half, dtype=F32) / half)
    ang = pos.astype(F32)[:, None] * inv[None, :]
    cos, sin = jnp.cos(ang), jnp.sin(ang)
    reps = LANES // HALF
    return jnp.tile(cos, (1, reps)), jnp.tile(sin, (1, reps))


def _rotary_order(w, heads):
    x1 = [w[:, h * HEAD_DIM:h * HEAD_DIM + HALF] for h in heads]
    x2 = [w[:, h * HEAD_DIM + HALF:(h + 1) * HEAD_DIM] for h in heads]
    return x1 + x2


def _prep_qkv_weight(w_in):
    qa = w_in[:, 0:A_Q]
    ka = w_in[:, A_Q:A_Q + A_KV]
    va = w_in[:, A_Q + A_KV:A_Q + 2 * A_KV]
    b0 = A_Q + 2 * A_KV
    qb = w_in[:, b0:b0 + 3 * B_G]
    kb = w_in[:, b0 + 3 * B_G:b0 + 6 * B_G]
    vb = w_in[:, b0 + 6 * B_G:b0 + 9 * B_G]
    cols = []
    for c in range(A_GROUP):
        cols += _rotary_order(qa, (c, A_GROUP + c))
    cols += _rotary_order(ka, range(A_KV_HEADS)) + [va]
    for g in range(len(B_DIL)):
        gs = slice(g * B_G, (g + 1) * B_G)
        cols += _rotary_order(qb[:, gs], range(B_SLOTS)) + _rotary_order(kb[:, gs], range(B_SLOTS)) + [vb[:, gs]]
    return jnp.concatenate(cols, axis=1).astype(BF16)


def _prep_br_a(w_br_a):
    rows = []
    for c in range(A_GROUP):
        rows += [w_br_a[c * HEAD_DIM:(c + 1) * HEAD_DIM],
                 w_br_a[(A_GROUP + c) * HEAD_DIM:(A_GROUP + c + 1) * HEAD_DIM]]
    return jnp.concatenate(rows, axis=0).astype(BF16)


_TM_PROMPT = 512


def kernel(x_prompt, x_sample, cache_a, cache_b1, cache_b2, cache_b3, state_conv, w_in, sink_a, w_br_a, w_br_b, w_o, ln1_g, ln1_b, w_up, conv_w, conv_b, w_down, ln2_g, ln2_b):
    assert DEPTH == 1
    l = 0
    w_qkv = _prep_qkv_weight(w_in[l])
    w_gate = w_in[l][:, N_QKV:].astype(BF16)
    ffn_weights = (w_gate, _prep_br_a(w_br_a[l]), w_br_b[l].astype(BF16), w_o[l].astype(BF16),
                   ln1_g[l][None], ln1_b[l][None], w_up[l].astype(BF16), conv_w[l], conv_b[l][None],
                   w_down[l].astype(BF16), ln2_g[l][None], ln2_b[l][None])
    sink = sink_a[l].astype(F32)

    ms = DEC_BATCH * DEC_SEQ
    xs = x_sample.reshape(ms, D_MODEL)
    cos_s, sin_s = _rope_tables(PAST_LEN + (jnp.arange(ms, dtype=jnp.int32) % DEC_SEQ))
    (qa_s, ca_s, kva_s, qb1_s, cb1_s, kvb1_s, qb2_s, cb2_s, kvb2_s, qb3_s, cb3_s, kvb3_s) = _qkv_call(
        xs, w_qkv, cos_s, sin_s, tm=ms, prompt=False, tiles_per_seq=1, name="qkv_sample")

    def window_buffer(c):
        return c.transpose(0, 2, 3, 4, 1).reshape(c.shape[0], -1, c.shape[1])

    ca = window_buffer(cache_a[l])
    cb = [window_buffer(c[l]) for c in (cache_b1, cache_b2, cache_b3)]
    sample_attn_args = (sink, qa_s, kva_s, ca, qb1_s, kvb1_s, cb[0], qb2_s, kvb2_s, cb[1], qb3_s, kvb3_s, cb[2])

    mp = BATCH * SEQ
    xp = x_prompt.reshape(mp, D_MODEL)
    cos_p, sin_p = _rope_tables(jnp.arange(SEQ, dtype=jnp.int32))
    tps = SEQ // _TM_PROMPT
    (qa, ca_p, kva_bf, qb1, cb1_p, kvb1_bf, qb2, cb2_p, kvb2_bf, qb3, cb3_p, kvb3_bf) = _qkv_call(
        xp, w_qkv, cos_p, sin_p, tm=_TM_PROMPT, prompt=True, tiles_per_seq=tps, name="qkv_prompt")
    oa = _attn_a_call(sink, qa, kva_bf)
    ob = _attn_b_call(qb1, kvb1_bf, qb2, kvb2_bf, qb3, kvb3_bf)
    y_p, ulast, oa_s, ob_s = _ffn_call(xp, oa, ob, ffn_weights, tm=_TM_PROMPT, sample=False, tiles_per_seq=tps,
                                       sample_attn_args=sample_attn_args, name="ffn_prompt")

    def prompt_cache(c, heads):
        return c.reshape(BATCH, 2, heads, HEAD_DIM, c.shape[-1]).transpose(0, 4, 1, 2, 3)[None]

    y_prompt = y_p.reshape(BATCH, SEQ, D_MODEL)
    cache_a_prompt = prompt_cache(ca_p, A_KV_HEADS)
    kvb_p = [prompt_cache(c, B_SLOTS) for c in (cb1_p, cb2_p, cb3_p)]
    state_conv_prompt = ulast.reshape(BATCH, tps, _CARRY, D_FF)[None, :, tps - 1, _CARRY - (CONV_W - 1):]

    fill = jnp.pad(state_conv[l], ((0, 0), (0, DEC_SEQ - (CONV_W - 1)), (0, 0))).reshape(ms, D_FF)
    y_s, u_s = _ffn_call(xs, oa_s, ob_s, ffn_weights, tm=ms, sample=True, tiles_per_seq=1,
                         conv_fill=fill, name="ffn_sample")

    def sample_cache(c, heads):
        return c.reshape(DEC_SEQ, 2, heads, HEAD_DIM, DEC_BATCH).transpose(4, 0, 1, 2, 3)[None]

    y_sample = y_s.reshape(DEC_BATCH, DEC_SEQ, D_MODEL)
    cache_a_sample = sample_cache(ca_s, A_KV_HEADS)
    kvb_s = [sample_cache(c, B_SLOTS) for c in (cb1_s, cb2_s, cb3_s)]
    state_conv_sample = u_s.reshape(DEC_BATCH, DEC_SEQ, D_FF)[None, :, DEC_SEQ - (CONV_W - 1):]

    return (y_prompt, y_sample, cache_a_prompt, cache_a_sample, kvb_p[0], kvb_s[0], kvb_p[1], kvb_s[1],
            kvb_p[2], kvb_s[2], state_conv_prompt, state_conv_sample)
```

```python
import functools

import jax
import jax.numpy as jnp
from jax import lax
from jax.experimental import pallas as pl
from jax.experimental.pallas import tpu as pltpu

D_MODEL = 1024
BATCH = 8
SEQ = 2048
DEPTH = 1
DEC_BATCH = 128
DEC_SEQ = 4
PAST_LEN = 16384
HEAD_DIM = 64
A_Q_HEADS = 8
A_KV_HEADS = 2
A_GROUP = A_Q_HEADS // A_KV_HEADS
A_WINDOW = 128
B_DIL = ((128, 1), (512, 4), (2048, 16))
B_SLOTS = 4
BLK = 128
ROPE_THETA = 10000.0
D_FF = ((8 * D_MODEL // 3 + 127) // 128) * 128
CONV_W = 3
ALPHA = (2 * DEPTH) ** 0.25
LN_EPS = 1e-5
NEG = -1e30
SCALE = HEAD_DIM ** -0.5
A_Q = A_Q_HEADS * HEAD_DIM
A_KV = A_KV_HEADS * HEAD_DIM
B_G = B_SLOTS * HEAD_DIM
N_QKV = A_Q + 2 * A_KV + 3 * 3 * B_G

LANES = 128
VMEM_LIMIT = 56 * 1024 * 1024

BF16 = jnp.bfloat16
F32 = jnp.float32

_C_QA = 0
_C_KVA = _C_QA + A_Q
_C_QB = (_C_KVA + 2 * A_KV, _C_KVA + 2 * A_KV + 3 * B_G, _C_KVA + 2 * A_KV + 6 * B_G)
_C_KVB = tuple(c + B_G for c in _C_QB)


def _const_spec(shape):
    nd = len(shape)
    return pl.BlockSpec(shape, lambda *_: (0,) * nd, pipeline_mode=pl.Buffered(1))


def _nt_dot(a, b):
    return lax.dot_general(a, b, (((1,), (1,)), ((), ())), preferred_element_type=F32)


def _dot(a, b):
    return jnp.dot(a, b, preferred_element_type=F32)


HALF = HEAD_DIM // 2


def _rope_split(y, cos4, sin4):
    x1, x2 = y[:, :LANES], y[:, LANES:]
    return jnp.concatenate([x1 * cos4 - x2 * sin4, x2 * cos4 + x1 * sin4], axis=1)


def _rope_rot64(y, cos4, sin_signed):
    outs = []
    for j in range(y.shape[1] // LANES):
        yj = y[:, j * LANES:(j + 1) * LANES]
        outs.append(yj * cos4 + pltpu.roll(yj, LANES // 2, 1) * sin_signed)
    return outs[0] if len(outs) == 1 else jnp.concatenate(outs, axis=1)


def _k_feature_rows(n_heads):
    return [(HEAD_DIM * h + HALF * part, n_heads * HALF * part + HALF * h)
            for part in range(2) for h in range(n_heads)]


def _qkv_kernel(x_ref, w_ref, cos_ref, sin_ref, *refs, tm, prompt, tiles_per_seq):
    out_refs, (xb_ref, y_ref, proj0_ref, proj1_ref) = refs[:-4], refs[-4:]
    q_refs, cache_refs, kv_refs = out_refs[0::3], out_refs[1::3], out_refs[2::3]
    groups = ((A_WINDOW, 1),) + B_DIL

    col_q = (_C_QA,) + _C_QB
    col_kv = (_C_KVA,) + _C_KVB
    width_q = (A_Q,) + (B_G,) * len(B_DIL)
    heads_kv = (A_KV_HEADS,) + (B_SLOTS,) * len(B_DIL)

    def project(proj_ref, c0, width):
        proj_ref[:, c0:c0 + width] = _dot(xb_ref[...], w_ref[:, c0:c0 + width])

    def put_cache_rows(cache_ref, lead, kv_t, n_heads):
        wk = n_heads * HEAD_DIM
        for ref_row, our_row in _k_feature_rows(n_heads):
            cache_ref[lead, ref_row:ref_row + HALF, :] = kv_t[our_row:our_row + HALF]
        cache_ref[lead, wk:2 * wk, :] = kv_t[wk:2 * wk]

    def to_planes(val):
        n_col = val.shape[1] // LANES
        for c in range(n_col):
            y_ref[c] = val[:, c * LANES:(c + 1) * LANES]
        return n_col

    def rope(y, n_heads):
        cos = cos_ref[...]
        sin = sin_ref[...]
        if n_heads == B_SLOTS:
            return _rope_split(y, cos, sin)
        lane = lax.broadcasted_iota(jnp.int32, (tm, LANES), 1)
        return _rope_rot64(y, cos, jnp.where(lane < LANES // 2, -sin, sin))

    def roped_q(proj_ref, g):
        return rope(proj_ref[:, col_q[g]:col_q[g] + width_q[g]], heads_kv[g]) * SCALE

    def roped_kv(proj_ref, g):
        wk = heads_kv[g] * HEAD_DIM
        return jnp.concatenate([rope(proj_ref[:, col_kv[g]:col_kv[g] + wk], heads_kv[g]),
                                proj_ref[:, col_kv[g] + wk:col_kv[g] + 2 * wk]], axis=1)

    def deinterleave(val, out_ref, dil):
        n_col = to_planes(val)
        for r in range(dil):
            out_ref[0, r] = jnp.concatenate(
                [y_ref[c, pl.ds(r, tm // dil, stride=dil), :] for c in range(n_col)], axis=1).astype(BF16)

    if not prompt:
        xb_ref[...] = x_ref[...].astype(BF16)
        project(proj0_ref, 0, N_QKV)
        n_seq = tm // DEC_SEQ
        for g in range(len(groups)):
            q_refs[g][...] = roped_q(proj0_ref, g).astype(BF16)
            kv = roped_kv(proj0_ref, g)
            kv_refs[g][...] = kv
            n_col = to_planes(kv)
            for t in range(DEC_SEQ):
                kv_t = jnp.concatenate(
                    [y_ref[c, pl.ds(t, n_seq, stride=DEC_SEQ), :].T for c in range(n_col)], axis=0)
                put_cache_rows(cache_refs[g], t, kv_t, heads_kv[g])
        return

    step = pl.program_id(0)
    tile = jnp.maximum(step - 1, 0)
    last_tile = (tile % tiles_per_seq) == tiles_per_seq - 1

    @pl.when(step == 0)
    def _():
        proj1_ref[...] = jnp.zeros_like(proj1_ref)

    def step_body(mine, other):
        xb_ref[...] = x_ref[...].astype(BF16)
        for g, (win, dil) in enumerate(groups):
            project(mine, col_q[g], width_q[g])
            q = roped_q(other, g)
            if dil == 1:
                q_refs[g][...] = q.astype(BF16)
            else:
                deinterleave(q, q_refs[g], dil)
            project(mine, col_kv[g], 2 * heads_kv[g] * HEAD_DIM)
            kv = roped_kv(other, g)
            if dil == 1:
                kv_refs[g][...] = kv.astype(BF16)
            else:
                deinterleave(kv, kv_refs[g], dil)
            if min(win, SEQ) == SEQ:
                put_cache_rows(cache_refs[g], 0, kv.T, heads_kv[g])

        @pl.when(last_tile)
        def _():
            for g, (win, _) in enumerate(groups):
                keep = min(win, SEQ)
                if keep < SEQ:
                    assert keep <= tm
                    put_cache_rows(cache_refs[g], 0, roped_kv(other, g)[tm - keep:, :].T, heads_kv[g])

    for par, (mine, other) in enumerate(((proj0_ref, proj1_ref), (proj1_ref, proj0_ref))):
        @pl.when(step % 2 == par)
        def _(mine=mine, other=other):
            step_body(mine, other)


def _qkv_call(x2d, w_qkv, cos_t, sin_t, *, tm, prompt, tiles_per_seq, name):
    m = x2d.shape[0]
    n_tiles = m // tm
    if prompt:
        tile = lambda i: jnp.maximum(i - 1, 0)
        x_spec = pl.BlockSpec((tm, D_MODEL), lambda i: (jnp.minimum(i, n_tiles - 1), 0))
    else:
        assert n_tiles == 1
        tile = lambda i: i
        x_spec = pl.BlockSpec((tm, D_MODEL), lambda i: (i, 0))
    row = lambda w: pl.BlockSpec((tm, w), lambda i: (tile(i), 0))
    tab = pl.BlockSpec((tm, LANES), lambda i: (tile(i) % tiles_per_seq, 0))
    in_specs = [x_spec, _const_spec((D_MODEL, N_QKV)), tab, tab]
    sds = jax.ShapeDtypeStruct
    out_shape, out_specs = [], []
    for g, (win, dil) in enumerate(((A_WINDOW, 1),) + B_DIL):
        wq = A_Q if g == 0 else B_G
        wkv = 2 * (A_KV if g == 0 else B_G)
        if not prompt:
            out_shape += [sds((m, wq), BF16), sds((DEC_SEQ, wkv, m // DEC_SEQ), F32), sds((m, wkv), F32)]
            out_specs += [row(wq), pl.BlockSpec((DEC_SEQ, wkv, m // DEC_SEQ), lambda i: (0, 0, 0)),
                          row(wkv)]
            continue
        n_seq = m // SEQ
        keep = min(win, SEQ)
        if keep == SEQ:
            cache_spec = pl.BlockSpec(
                (1, wkv, tm), lambda i: (tile(i) // tiles_per_seq, 0, tile(i) % tiles_per_seq))
        else:
            cache_spec = pl.BlockSpec((1, wkv, keep), lambda i: (tile(i) // tiles_per_seq, 0, 0))
        if dil == 1:
            out_shape += [sds((m, wq), BF16), sds((n_seq, wkv, keep), F32), sds((m, wkv), BF16)]
            out_specs += [row(wq), cache_spec, row(wkv)]
        else:
            dspec = lambda w, dil=dil: pl.BlockSpec(
                (1, dil, tm // dil, w),
                lambda i: (tile(i) // tiles_per_seq, 0, tile(i) % tiles_per_seq, 0))
            out_shape += [sds((n_seq, dil, SEQ // dil, wq), BF16), sds((n_seq, wkv, keep), F32),
                          sds((n_seq, dil, SEQ // dil, wkv), BF16)]
            out_specs += [dspec(wq), cache_spec, dspec(wkv)]
    args = [x2d, w_qkv, cos_t, sin_t]
    proj_bufs = 2 if prompt else 1
    return pl.pallas_call(
        functools.partial(_qkv_kernel, tm=tm, prompt=prompt, tiles_per_seq=tiles_per_seq),
        grid=(n_tiles + 1,) if prompt else (n_tiles,),
        in_specs=in_specs,
        out_specs=out_specs,
        out_shape=out_shape,
        scratch_shapes=[pltpu.VMEM((tm, D_MODEL), BF16),
                        pltpu.VMEM((2 * B_G // LANES, tm, LANES), F32)]
        + [pltpu.VMEM((tm, N_QKV) if k < proj_bufs else (8, LANES), F32) for k in range(2)],
        compiler_params=pltpu.CompilerParams(
            dimension_semantics=("arbitrary",), vmem_limit_bytes=VMEM_LIMIT),
        name=name,
    )(*args)


def _a_query_masks(rows):
    lane = lax.broadcasted_iota(jnp.int32, (rows, LANES), 1)
    return [jnp.where(((lane >> 5) & (A_KV_HEADS - 1)) == j, 1.0, 0.0).astype(BF16)
            for j in range(A_KV_HEADS)]


def _fold_masks():
    row = lax.broadcasted_iota(jnp.int32, (BLK, BLK), 0)
    col = lax.broadcasted_iota(jnp.int32, (BLK, BLK), 1)
    return col > row


def _attn_scratch(heads):
    return [pltpu.VMEM((2, heads * BLK, 2 * BLK), F32)]


def _attn_pipeline(n_units, bufs, *, heads, load_q, load_k, load_v, has_prev, sink_of_head, finish,
                   mxu_row_sum=False):
    (sbuf,) = bufs
    upper = _fold_masks()
    upper_bf = jnp.where(upper, 1.0, 0.0).astype(BF16)
    width = BLK if has_prev is None else 2 * BLK
    assert n_units % 2 == 0

    sbuf[1] = jnp.zeros(sbuf.shape[1:], F32)

    def scores(u, par):
        sbuf[par, :, 0:width] = _nt_dot(load_q(u), load_k(u))

    def softmax_values(u, par):
        prev_ok = None if has_prev is None else has_prev(u)
        ps, ms, ls = [], [], []
        for h in range(heads):
            rows = slice(h * BLK, (h + 1) * BLK)
            if has_prev is None:
                sf = jnp.where(upper, NEG, sbuf[par, rows, 0:BLK])
            else:
                sf = jnp.where(upper, jnp.where(prev_ok, sbuf[par, rows, 0:BLK], NEG),
                               sbuf[par, rows, BLK:2 * BLK])
            mx = jnp.max(sf, axis=-1, keepdims=True)
            if sink_of_head is not None:
                mx = jnp.maximum(mx, sink_of_head(h))
            e = jnp.exp(sf - mx)
            if not mxu_row_sum:
                ls.append(jnp.broadcast_to(jnp.sum(e, axis=-1, keepdims=True), (BLK, LANES)))
            e = e.astype(BF16)
            if has_prev is None:
                ps.append(e)
            else:
                p_prev = e * upper_bf
                ps.append(jnp.concatenate([p_prev, e - p_prev], axis=1))
            ms.append(jnp.broadcast_to(mx, (BLK, LANES)))
        finish(u, _dot(jnp.concatenate(ps, axis=0), load_v(u)), jnp.concatenate(ms, axis=0),
               jnp.concatenate(ls, axis=0) if ls else None)

    last = n_units - 1

    def trip_pair(t, carry):
        for par in (0, 1):
            i = 2 * t + par
            scores(jnp.minimum(i, last), par)
            softmax_values(jnp.clip(i - 1, 0, last), 1 - par)
        return carry

    lax.fori_loop(0, n_units // 2 + 1, trip_pair, 0)


def _attn_a_kernel(sink_ref, q_ref, kv_ref, o_ref, *bufs):
    n_blk = q_ref.shape[0] // BLK
    lane = lax.broadcasted_iota(jnp.int32, (BLK, LANES), 1)
    hi = lane >= HEAD_DIM
    mask_bf = _a_query_masks(BLK)

    def rows_of(b):
        return pl.ds(pl.multiple_of(b * BLK, BLK), BLK)

    def prev_cur(b, cols):
        return jnp.concatenate([kv_ref[rows_of(jnp.maximum(b - 1, 0)), cols], kv_ref[rows_of(b), cols]],
                               axis=0)

    def load_q(b):
        qblk = q_ref[rows_of(b), :]
        return jnp.concatenate([qblk[:, g * LANES:(g + 1) * LANES] * mask_bf[j]
                                for j in range(A_KV_HEADS) for g in range(A_GROUP)], axis=0)

    ones = jnp.ones((2 * BLK, LANES), BF16)

    def load_v(b):
        return jnp.concatenate([prev_cur(b, slice(A_KV, 2 * A_KV)), ones], axis=1)

    def finish(b, o, m, _):
        def normalised(h):
            rows = slice(h * BLK, (h + 1) * BLK)
            den = o[rows, LANES:] + jnp.exp(sink_ref[h] - m[rows])
            return o[rows, :LANES] / den

        for g in range(A_GROUP):
            o_ref[rows_of(b), g * LANES:(g + 1) * LANES] = jnp.where(
                hi, normalised(A_GROUP + g), normalised(g)).astype(BF16)

    _attn_pipeline(n_blk, bufs, heads=A_Q_HEADS, load_q=load_q,
                   load_k=lambda b: prev_cur(b, slice(0, A_KV)), load_v=load_v,
                   has_prev=lambda b: b > 0, sink_of_head=lambda h: sink_ref[h], finish=finish,
                   mxu_row_sum=True)


def _attn_a_call(sink, q_a, kva_bf):
    n_seq = q_a.shape[0] // SEQ
    return pl.pallas_call(
        _attn_a_kernel,
        grid=(n_seq,),
        in_specs=[pl.BlockSpec(memory_space=pltpu.SMEM),
                  pl.BlockSpec((SEQ, A_Q), lambda n: (n, 0)),
                  pl.BlockSpec((SEQ, 2 * A_KV), lambda n: (n, 0))],
        out_specs=pl.BlockSpec((SEQ, A_Q), lambda n: (n, 0)),
        out_shape=jax.ShapeDtypeStruct(q_a.shape, BF16),
        scratch_shapes=_attn_scratch(A_Q_HEADS),
        compiler_params=pltpu.CompilerParams(
            dimension_semantics=("arbitrary",), vmem_limit_bytes=VMEM_LIMIT),
        name="attn_a_prompt",
    )(sink, q_a, kva_bf)


def _slot_masks(rows):
    lane = lax.broadcasted_iota(jnp.int32, (rows, B_G), 1)
    masks = [(lane >> 6) == s for s in range(B_SLOTS)]
    masks_bf = [jnp.where(((lane >> 5) & (B_SLOTS - 1)) == s, 1.0, 0.0).astype(BF16) for s in range(B_SLOTS)]
    return masks, masks_bf


def _attn_b_kernel(q1_ref, kv1_ref, q2_ref, kv2_ref, q3_ref, kv3_ref, o_ref,
                   acc2_ref, m2_ref, l2_ref, acc3_ref, m3_ref, l3_ref, *bufs):
    masks, masks_bf = _slot_masks(BLK)
    d2, d3 = B_DIL[1][1], B_DIL[2][1]
    nb2 = SEQ // d2 // BLK
    assert SEQ // d3 == BLK
    n_plane = B_G // LANES

    def put(ref, rows, val):
        for c in range(n_plane):
            ref[c, rows, :] = val[:, c * LANES:(c + 1) * LANES]

    def get(ref, rows):
        return jnp.concatenate([ref[c, rows, :] for c in range(n_plane)], axis=1)

    def stack_q(qblk):
        return jnp.concatenate([qblk * masks_bf[s] for s in range(B_SLOTS)], axis=0)

    def unstack(o, m, l):
        acc = jnp.where(masks[0], o[0:BLK], 0.0)
        rep = lambda x, s: jnp.concatenate([x[s * BLK:(s + 1) * BLK]] * n_plane, axis=1)
        mf, lf = rep(m, 0), rep(l, 0)
        for s in range(1, B_SLOTS):
            acc = jnp.where(masks[s], o[s * BLK:(s + 1) * BLK], acc)
            mf = jnp.where(masks[s], rep(m, s), mf)
            lf = jnp.where(masks[s], rep(l, s), lf)
        return acc, mf, lf

    def rows_of(b):
        return pl.ds(pl.multiple_of(b * BLK, BLK), BLK)

    run = functools.partial(_attn_pipeline, bufs=bufs, heads=B_SLOTS, sink_of_head=None)

    def finish3(r, o, m, l):
        rows = pl.ds(r, BLK, stride=d3)
        for ref, val in zip((acc3_ref, m3_ref, l3_ref), unstack(o, m, l)):
            put(ref, rows, val)

    run(d3, load_q=lambda r: stack_q(q3_ref[r]), load_k=lambda r: kv3_ref[r, :, 0:B_G],
        load_v=lambda r: kv3_ref[r, :, B_G:2 * B_G], has_prev=None, finish=finish3)

    def prev_cur2(u, cols):
        r, b = u // nb2, u % nb2
        return jnp.concatenate([kv2_ref[r, rows_of(jnp.maximum(b - 1, 0)), cols], kv2_ref[r, rows_of(b), cols]],
                               axis=0)

    def finish2(u, o, m, l):
        r, b = u // nb2, u % nb2
        rows = pl.ds(r + b * (BLK * d2), BLK, stride=d2)
        for ref, val in zip((acc2_ref, m2_ref, l2_ref), unstack(o, m, l)):
            put(ref, rows, val)

    run(d2 * nb2, load_q=lambda u: stack_q(q2_ref[u // nb2, rows_of(u % nb2), :]),
        load_k=lambda u: prev_cur2(u, slice(0, B_G)), load_v=lambda u: prev_cur2(u, slice(B_G, 2 * B_G)),
        has_prev=lambda u: (u % nb2) > 0, finish=finish2)

    def prev_cur1(b, cols):
        return jnp.concatenate([kv1_ref[rows_of(jnp.maximum(b - 1, 0)), cols], kv1_ref[rows_of(b), cols]],
                               axis=0)

    def finish1(b, o, m, l):
        acc1, m1, l1 = unstack(o, m, l)
        rows = rows_of(b)
        m2, m3 = get(m2_ref, rows), get(m3_ref, rows)
        mx = jnp.maximum(jnp.maximum(m1, m2), m3)
        w1, w2, w3 = jnp.exp(m1 - mx), jnp.exp(m2 - mx), jnp.exp(m3 - mx)
        num = w1 * acc1 + w2 * get(acc2_ref, rows) + w3 * get(acc3_ref, rows)
        den = w1 * l1 + w2 * get(l2_ref, rows) + w3 * get(l3_ref, rows)
        o_ref[rows, :] = (num / den).astype(BF16)

    run(SEQ // BLK, load_q=lambda b: stack_q(q1_ref[rows_of(b), :]),
        load_k=lambda b: prev_cur1(b, slice(0, B_G)), load_v=lambda b: prev_cur1(b, slice(B_G, 2 * B_G)),
        has_prev=lambda b: b > 0, finish=finish1)


def _attn_b_call(q1, kv1, q2, kv2, q3, kv3):
    n_seq = q1.shape[0] // SEQ
    d2, d3 = B_DIL[1][1], B_DIL[2][1]
    rows = lambda w: pl.BlockSpec((SEQ, w), lambda n: (n, 0))
    dsp = lambda d, w: pl.BlockSpec((None, d, SEQ // d, w), lambda n: (n, 0, 0, 0))
    return pl.pallas_call(
        _attn_b_kernel,
        grid=(n_seq,),
        in_specs=[rows(B_G), rows(2 * B_G), dsp(d2, B_G), dsp(d2, 2 * B_G), dsp(d3, B_G),
                  dsp(d3, 2 * B_G)],
        out_specs=rows(B_G),
        out_shape=jax.ShapeDtypeStruct((q1.shape[0], B_G), BF16),
        scratch_shapes=[pltpu.VMEM((B_G // LANES, SEQ, LANES), F32) for _ in range(6)]
        + _attn_scratch(B_SLOTS),
        compiler_params=pltpu.CompilerParams(
            dimension_semantics=("arbitrary",), vmem_limit_bytes=VMEM_LIMIT),
        name="attn_b_prompt",
    )(q1, kv1, q2, kv2, q3, kv3)


def _sample_attn_block(sink_ref, qa_ref, kna_ref, ca_ref, q1_ref, kn1_ref, c1_ref,
                       q2_ref, kn2_ref, c2_ref, q3_ref, kn3_ref, c3_ref, *, tt, seqs, emit):
    pad = BLK - tt
    tt_shift, seq_shift = tt.bit_length() - 1, DEC_SEQ.bit_length() - 1
    assert tt == 1 << tt_shift and DEC_SEQ == 1 << seq_shift

    def pad_rows(x):
        return jnp.concatenate([x.astype(BF16), jnp.zeros((pad, x.shape[1]), BF16)], axis=0)

    def window_group(qm, kn_ref, c_ref, dil, sinkv, finish):
        rows, kd = qm.shape
        lc = c_ref.shape[2]
        kn = kn_ref[...]
        knpad = pad_rows(kn[:, :kd])
        vnpad = pad_rows(kn[:, kd:])
        s_new = _nt_dot(qm, knpad)
        def masks(width):
            col = lax.broadcasted_iota(jnp.int32, (rows, width), 1)
            t = lax.broadcasted_iota(jnp.int32, (rows, width), 0) & (DEC_SEQ - 1)
            same_res = (col & (dil - 1)) == (t & (dil - 1))
            return (col > t) & same_res, (col <= t) & same_res

        valid, _ = masks(lc)
        _, is_new = masks(BLK)
        k_rows = sorted(_k_feature_rows(kd // HEAD_DIM), key=lambda rows: rows[1])
        scores = []
        for pos, _ in seqs:
            k_t = jnp.concatenate([c_ref[pos, ref_row:ref_row + HALF, :] for ref_row, _ in k_rows],
                                  axis=0).astype(BF16)
            scores.append(_dot(qm, k_t))
        yield
        probs = []
        for score, (_, n) in zip(scores, seqs):
            s_c = jnp.where(valid, score, NEG)
            s_n = pltpu.roll(s_new, BLK - DEC_SEQ * n, 1) if n else s_new
            first = jnp.where(is_new, s_n, s_c[:, :BLK])
            s = first if lc == BLK else jnp.concatenate([first, s_c[:, BLK:]], axis=1)
            mx = jnp.max(s, axis=-1, keepdims=True)
            if sinkv is not None:
                mx = jnp.maximum(mx, sinkv)
            e = jnp.exp(s - mx)
            l = jnp.sum(e, axis=-1, keepdims=True)
            if sinkv is not None:
                l = l + jnp.exp(sinkv - mx)
            p_new = jnp.where(is_new, e[:, :BLK], 0.0)
            if n:
                p_new = pltpu.roll(p_new, DEC_SEQ * n, 1)
            probs.append((jnp.where(valid, e, 0.0).astype(BF16), p_new.astype(BF16), mx, l))
        yield
        for (p_c, p_new, mx, l), (pos, n) in zip(probs, seqs):
            v_t = c_ref[pos, kd:2 * kd, :].astype(BF16)
            finish(n, _nt_dot(p_c, v_t) + _dot(p_new, vnpad), mx, l)

    rows_a = A_Q_HEADS * tt
    lane_t = lax.broadcasted_iota(jnp.int32, (tt, LANES), 1)
    hi_t = lane_t >= HEAD_DIM
    mask_bf = _a_query_masks(tt)
    qa = qa_ref[...]
    qm = jnp.concatenate([qa[:, g * LANES:(g + 1) * LANES] * mask_bf[j]
                          for j in range(A_KV_HEADS) for g in range(A_GROUP)], axis=0)
    head = lax.broadcasted_iota(jnp.int32, (rows_a, 1), 0) >> tt_shift
    sinkv = jnp.zeros((rows_a, 1), F32)
    for h in range(A_Q_HEADS):
        sinkv = jnp.where(head == h, sink_ref[h], sinkv)
    seq_t = lax.broadcasted_iota(jnp.int32, (tt, LANES), 0) >> seq_shift
    out_a = [jnp.zeros((tt, LANES), F32) for _ in range(A_GROUP)]

    def finish_a(n, o, mx, l):
        o = o / l
        for g in range(A_GROUP):
            og = jnp.where(hi_t, o[(A_GROUP + g) * tt:(A_GROUP + g + 1) * tt], o[g * tt:(g + 1) * tt])
            out_a[g] = jnp.where(seq_t == n, og, out_a[g])

    stages = [window_group(qm, kna_ref, ca_ref, 1, sinkv, finish_a)]

    masks_t, masks_bf_t = _slot_masks(tt)
    tok_t = lax.broadcasted_iota(jnp.int32, (tt, B_G), 0)

    def stack_q(q):
        return jnp.concatenate([q * masks_bf_t[s] for s in range(B_SLOTS)], axis=0)

    def unstack(o, mx, l):
        acc = jnp.where(masks_t[0], o[0:tt], 0.0)
        mf = jnp.broadcast_to(mx[0:tt], (tt, B_G))
        lf = jnp.broadcast_to(l[0:tt], (tt, B_G))
        for s in range(1, B_SLOTS):
            acc = jnp.where(masks_t[s], o[s * tt:(s + 1) * tt], acc)
            mf = jnp.where(masks_t[s], mx[s * tt:(s + 1) * tt], mf)
            lf = jnp.where(masks_t[s], l[s * tt:(s + 1) * tt], lf)
        return acc, mf, lf

    zero, one = jnp.zeros((tt, B_G), F32), jnp.ones((tt, B_G), F32)
    stats = []
    for (win, dil), q_ref, kn_ref, c_ref in zip(B_DIL, (q1_ref, q2_ref, q3_ref),
                                                (kn1_ref, kn2_ref, kn3_ref), (c1_ref, c2_ref, c3_ref)):
        assert c_ref.shape[2] == win
        group = [zero, zero, one]

        def finish_b(n, o, mx, l, group=group):
            sel = (tok_t >> seq_shift) == n
            for k, new in enumerate(unstack(o, mx, l)):
                group[k] = jnp.where(sel, new, group[k])

        stages.append(window_group(stack_q(q_ref[...]), kn_ref, c_ref, dil, None, finish_b))
        stats.append(group)

    for phase in range(3):
        for stage in stages:
            next(stage, None)
        if phase < 2:
            yield
    (acc1, m1, l1), (acc2, m2, l2), (acc3, m3, l3) = stats
    mx = jnp.maximum(jnp.maximum(m1, m2), m3)
    w1, w2, w3 = jnp.exp(m1 - mx), jnp.exp(m2 - mx), jnp.exp(m3 - mx)
    emit(out_a, (w1 * acc1 + w2 * acc2 + w3 * acc3) / (w1 * l1 + w2 * l2 + w3 * l3))


def _sample_attn_rider(in_refs, oa_ref, ob_ref, k, n_seqs):
    seq_shift = DEC_SEQ.bit_length() - 1

    def merge(ref, cols, new):
        if k:
            mine = (lax.broadcasted_iota(jnp.int32, new.shape, 0) >> seq_shift) == k
            new = jnp.where(mine, new, ref[:, cols].astype(F32))
        ref[:, cols] = new.astype(BF16)

    def emit(out_a, out_b):
        for c in range(A_GROUP):
            merge(oa_ref, slice(c * LANES, (c + 1) * LANES), out_a[c])
        merge(ob_ref, slice(0, B_G), out_b)

    yield from _sample_attn_block(*in_refs, tt=n_seqs * DEC_SEQ, seqs=[(0, k)], emit=emit)


def _layernorm(x, g, b):
    mu = jnp.mean(x, axis=-1, keepdims=True)
    xc = x - mu
    var = jnp.mean(xc * xc, axis=-1, keepdims=True)
    return xc * lax.rsqrt(var + LN_EPS) * g + b


def _gelu_exact(x):
    return 0.5 * x * (1.0 + lax.erf(x * (0.5 ** 0.5)))


_FF_CHUNK = 256
_CARRY = 8


def _ffn_kernel(*refs, tm, sample, tiles_per_seq):
    if sample:
        (x_ref, oa_ref, ob_ref, fill_ref, wg_ref, wa_ref, wb_ref, wo_ref, g1_ref, b1_ref,
         wup_ref, cw_ref, cb_ref, wdn_ref, g2_ref, b2_ref, y_ref, u_ref,
         xb_ref, m_ref, gg_ref, ext_ref, h0_ref, hb0_ref) = refs
    else:
        x_ref, oa_ref, ob_ref = refs[:3]
        rider_in, refs = refs[3:3 + _N_SAMPLE_ATTN_IN], refs[3 + _N_SAMPLE_ATTN_IN:]
        (wg_ref, wa_ref, wb_ref, wo_ref, g1_ref, b1_ref,
         wup_ref, cw_ref, cb_ref, wdn_ref, g2_ref, b2_ref, y_ref, ulast_ref, oa_s_ref, ob_s_ref,
         xb_ref, m_ref, gg_ref, ext_ref, h0_ref, hb0_ref, carry_ref) = refs
    half = D_MODEL // 2

    def merge_pieces(h_ref, hb_ref):
        def gate_half(c):
            if c == 0:
                xb_ref[...] = x_ref[...].astype(BF16)
            cs = slice(c * half, (c + 1) * half)
            ga = _dot(xb_ref[...], wg_ref[:, c * half:(c + 1) * half])
            gb = _dot(xb_ref[...], wg_ref[:, D_MODEL + c * half:D_MODEL + (c + 1) * half])
            ta = _dot(oa_ref[...], wa_ref[:, cs])
            tb = _dot(ob_ref[...], wb_ref[:, cs])
            m_ref[:, cs] = (jax.nn.sigmoid(ga) * ta + jax.nn.sigmoid(gb) * tb).astype(BF16)

        def out_proj():
            mix = _dot(m_ref[...], wo_ref[...])
            h = _layernorm(ALPHA * x_ref[...] + mix, g1_ref[...], b1_ref[...])
            h_ref[...] = h
            hb_ref[...] = h.astype(BF16)

        return [functools.partial(gate_half, 0), functools.partial(gate_half, 1), out_proj]

    def ffn_pieces(h_ref, hb_ref):
        def chunk(c):
            cs = slice(c * _FF_CHUNK, (c + 1) * _FF_CHUNK)
            u = _dot(hb_ref[...], wup_ref[:, c * _FF_CHUNK:(c + 1) * _FF_CHUNK])
            v = _dot(hb_ref[...], wup_ref[:, D_FF + c * _FF_CHUNK:D_FF + (c + 1) * _FF_CHUNK])
            if sample:
                ext_ref[0:_CARRY, :] = jnp.zeros((_CARRY, _FF_CHUNK), F32)
            else:
                ext_ref[0:_CARRY, :] = carry_ref[:, cs]
            ext_ref[_CARRY:_CARRY + tm, :] = u
            u1 = ext_ref[_CARRY - 1:_CARRY - 1 + tm, :]
            u2 = ext_ref[_CARRY - 2:_CARRY - 2 + tm, :]
            if sample:
                t = lax.broadcasted_iota(jnp.int32, (tm, _FF_CHUNK), 0) & (DEC_SEQ - 1)
                fill = fill_ref[:, cs]
                u1 = jnp.where(t >= 1, u1, pltpu.roll(fill, tm - 1, 0))
                u2 = jnp.where(t >= 2, u2, fill)
                u_ref[:, cs] = u
            else:
                tail = u[tm - _CARRY:tm, :]
                carry_ref[:, cs] = tail
                ulast_ref[0, :, cs] = tail
            a = cb_ref[:, cs] + cw_ref[0:1, cs] * u2 + cw_ref[1:2, cs] * u1 + cw_ref[2:3, cs] * u
            gg_ref[:, cs] = (_gelu_exact(a) * v).astype(BF16)

        def down():
            f = _dot(gg_ref[...], wdn_ref[...])
            y_ref[...] = _layernorm(ALPHA * h_ref[...] + f, g2_ref[...], b2_ref[...])

        return [functools.partial(chunk, c) for c in range(D_FF // _FF_CHUNK)] + [down]

    pieces = merge_pieces(h0_ref, hb0_ref) + ffn_pieces(h0_ref, hb0_ref)
    if sample:
        for piece in pieces:
            piece()
        return

    step, sub = pl.program_id(0), pl.program_id(1)

    @pl.when((step % tiles_per_seq == 0) & (sub == 0))
    def _():
        carry_ref[...] = jnp.zeros_like(carry_ref)

    n_chunk = D_FF // _FF_CHUNK
    phases = (pieces[:3], pieces[3:3 + n_chunk // 2], pieces[3 + n_chunk // 2:3 + n_chunk], pieces[3 + n_chunk:])
    assert len(phases) == _FFN_PHASES
    for g, phase in enumerate(phases):
        @pl.when(sub == g)
        def _(g=g, phase=phase):
            rider = _sample_attn_rider(rider_in, oa_s_ref, ob_s_ref, g, _FFN_PHASES)
            for piece in phase:
                next(rider, None)
                piece()
            for _ in rider:
                pass


_FFN_PHASES = 4
_N_SAMPLE_ATTN_IN = 13


def _ffn_call(x2d, oa, ob, weights, *, tm, sample, tiles_per_seq, conv_fill=None, sample_attn_args=(), name):
    m = x2d.shape[0]
    n_tiles = m // tm
    sds = jax.ShapeDtypeStruct
    row_in = row_out = lambda w: pl.BlockSpec((tm, w), lambda i, *_: (i, 0))
    in_specs = [row_in(D_MODEL), row_in(A_Q), row_in(B_G)]
    args = [x2d, oa, ob]
    if sample:
        in_specs += [row_in(D_FF)]
        args += [conv_fill]
    else:
        assert len(sample_attn_args) == _N_SAMPLE_ATTN_IN
        tt = _FFN_PHASES * DEC_SEQ
        n_tok = sample_attn_args[1].shape[0]
        assert n_tok == n_tiles * tt
        in_specs.append(pl.BlockSpec(memory_space=pltpu.SMEM))
        for k, a in enumerate(sample_attn_args[1:]):
            if k % 3 == 2:
                in_specs.append(pl.BlockSpec((1,) + a.shape[1:], lambda i, g: (_FFN_PHASES * i + g, 0, 0)))
            else:
                in_specs.append(pl.BlockSpec((tt, a.shape[1]), lambda i, g: (i, 0)))
        args += list(sample_attn_args)
    in_specs += [_const_spec(w.shape) for w in weights]
    args += list(weights)
    h_bufs = [pltpu.VMEM((tm, D_MODEL), F32), pltpu.VMEM((tm, D_MODEL), BF16)]
    scratch = [pltpu.VMEM((tm, D_MODEL), BF16), pltpu.VMEM((tm, D_MODEL), BF16),
               pltpu.VMEM((tm, D_FF), BF16), pltpu.VMEM((tm + _CARRY, _FF_CHUNK), F32)] + h_bufs
    if sample:
        out_shape = [sds((m, D_MODEL), F32), sds((m, D_FF), F32)]
        out_specs = [row_out(D_MODEL), row_out(D_FF)]
    else:
        out_shape = [sds((m, D_MODEL), F32), sds((n_tiles, _CARRY, D_FF), F32),
                     sds((n_tok, A_Q), BF16), sds((n_tok, B_G), BF16)]
        out_specs = [row_out(D_MODEL), pl.BlockSpec((1, _CARRY, D_FF), lambda i, g: (i, 0, 0)),
                     pl.BlockSpec((tt, A_Q), lambda i, g: (i, 0)), pl.BlockSpec((tt, B_G), lambda i, g: (i, 0))]
        scratch += [pltpu.VMEM((_CARRY, D_FF), F32)]
    grid = (n_tiles,) if sample else (n_tiles, _FFN_PHASES)
    return pl.pallas_call(
        functools.partial(_ffn_kernel, tm=tm, sample=sample, tiles_per_seq=tiles_per_seq),
        grid=grid,
        in_specs=in_specs,
        out_specs=out_specs,
        out_shape=out_shape,
        scratch_shapes=scratch,
        compiler_params=pltpu.CompilerParams(
            dimension_semantics=("arbitrary",) * len(grid), vmem_limit_bytes=VMEM_LIMIT),
        name=name,
    )(*args)


def _rope_tables(pos):
    half = HEAD_DIM // 2
    inv = ROPE_THETA ** (-jnp.arange(half, dtype=F32) / half)
    ang = pos.astype(F32)[:, None] * inv[None, :]
    cos, sin = jnp.cos(ang), jnp.sin(ang)
    reps = LANES // HALF
    return jnp.tile(cos, (1, reps)), jnp.tile(sin, (1, reps))


def _rotary_order(w, heads):
    x1 = [w[:, h * HEAD_DIM:h * HEAD_DIM + HALF] for h in heads]
    x2 = [w[:, h * HEAD_DIM + HALF:(h + 1) * HEAD_DIM] for h in heads]
    return x1 + x2


def _prep_qkv_weight(w_in):
    qa = w_in[:, 0:A_Q]
    ka = w_in[:, A_Q:A_Q + A_KV]
    va = w_in[:, A_Q + A_KV:A_Q + 2 * A_KV]
    b0 = A_Q + 2 * A_KV
    qb = w_in[:, b0:b0 + 3 * B_G]
    kb = w_in[:, b0 + 3 * B_G:b0 + 6 * B_G]
    vb = w_in[:, b0 + 6 * B_G:b0 + 9 * B_G]
    cols = []
    for c in range(A_GROUP):
        cols += _rotary_order(qa, (c, A_GROUP + c))
    cols += _rotary_order(ka, range(A_KV_HEADS)) + [va]
    for g in range(len(B_DIL)):
        gs = slice(g * B_G, (g + 1) * B_G)
        cols += _rotary_order(qb[:, gs], range(B_SLOTS)) + _rotary_order(kb[:, gs], range(B_SLOTS)) + [vb[:, gs]]
    return jnp.concatenate(cols, axis=1).astype(BF16)


def _prep_br_a(w_br_a):
    rows = []
    for c in range(A_GROUP):
        rows += [w_br_a[c * HEAD_DIM:(c + 1) * HEAD_DIM],
                 w_br_a[(A_GROUP + c) * HEAD_DIM:(A_GROUP + c + 1) * HEAD_DIM]]
    return jnp.concatenate(rows, axis=0).astype(BF16)


_TM_PROMPT = 512


def kernel(x_prompt, x_sample, cache_a, cache_b1, cache_b2, cache_b3, state_conv, w_in, sink_a, w_br_a, w_br_b, w_o, ln1_g, ln1_b, w_up, conv_w, conv_b, w_down, ln2_g, ln2_b):
    assert DEPTH == 1
    l = 0
    w_qkv = _prep_qkv_weight(w_in[l])
    w_gate = w_in[l][:, N_QKV:].astype(BF16)
    ffn_weights = (w_gate, _prep_br_a(w_br_a[l]), w_br_b[l].astype(BF16), w_o[l].astype(BF16),
                   ln1_g[l][None], ln1_b[l][None], w_up[l].astype(BF16), conv_w[l], conv_b[l][None],
                   w_down[l].astype(BF16), ln2_g[l][None], ln2_b[l][None])
    sink = sink_a[l].astype(F32)

    ms = DEC_BATCH * DEC_SEQ
    xs = x_sample.reshape(ms, D_MODEL)
    cos_s, sin_s = _rope_tables(PAST_LEN + (jnp.arange(ms, dtype=jnp.int32) % DEC_SEQ))
    (qa_s, ca_s, kva_s, qb1_s, cb1_s, kvb1_s, qb2_s, cb2_s, kvb2_s, qb3_s, cb3_s, kvb3_s) = _qkv_call(
        xs, w_qkv, cos_s, sin_s, tm=ms, prompt=False, tiles_per_seq=1, name="qkv_sample")

    def window_buffer(c):
        return c.transpose(0, 2, 3, 4, 1).reshape(c.shape[0], -1, c.shape[1])

    ca = window_buffer(cache_a[l])
    cb = [window_buffer(c[l]) for c in (cache_b1, cache_b2, cache_b3)]
    sample_attn_args = (sink, qa_s, kva_s, ca, qb1_s, kvb1_s, cb[0], qb2_s, kvb2_s, cb[1], qb3_s, kvb3_s, cb[2])

    mp = BATCH * SEQ
    xp = x_prompt.reshape(mp, D_MODEL)
    cos_p, sin_p = _rope_tables(jnp.arange(SEQ, dtype=jnp.int32))
    tps = SEQ // _TM_PROMPT
    (qa, ca_p, kva_bf, qb1, cb1_p, kvb1_bf, qb2, cb2_p, kvb2_bf, qb3, cb3_p, kvb3_bf) = _qkv_call(
        xp, w_qkv, cos_p, sin_p, tm=_TM_PROMPT, prompt=True, tiles_per_seq=tps, name="qkv_prompt")
    oa = _attn_a_call(sink, qa, kva_bf)
    ob = _attn_b_call(qb1, kvb1_bf, qb2, kvb2_bf, qb3, kvb3_bf)
    y_p, ulast, oa_s, ob_s = _ffn_call(xp, oa, ob, ffn_weights, tm=_TM_PROMPT, sample=False, tiles_per_seq=tps,
                                       sample_attn_args=sample_attn_args, name="ffn_prompt")

    def prompt_cache(c, heads):
        return c.reshape(BATCH, 2, heads, HEAD_DIM, c.shape[-1]).transpose(0, 4, 1, 2, 3)[None]

    y_prompt = y_p.reshape(BATCH, SEQ, D_MODEL)
    cache_a_prompt = prompt_cache(ca_p, A_KV_HEADS)
    kvb_p = [prompt_cache(c, B_SLOTS) for c in (cb1_p, cb2_p, cb3_p)]
    state_conv_prompt = ulast.reshape(BATCH, tps, _CARRY, D_FF)[None, :, tps - 1, _CARRY - (CONV_W - 1):]

    fill = jnp.pad(state_conv[l], ((0, 0), (0, DEC_SEQ - (CONV_W - 1)), (0, 0))).reshape(ms, D_FF)
    y_s, u_s = _ffn_call(xs, oa_s, ob_s, ffn_weights, tm=ms, sample=True, tiles_per_seq=1,
                         conv_fill=fill, name="ffn_sample")

    def sample_cache(c, heads):
        return c.reshape(DEC_SEQ, 2, heads, HEAD_DIM, DEC_BATCH).transpose(4, 0, 1, 2, 3)[None]

    y_sample = y_s.reshape(DEC_BATCH, DEC_SEQ, D_MODEL)
    cache_a_sample = sample_cache(ca_s, A_KV_HEADS)
    kvb_s = [sample_cache(c, B_SLOTS) for c in (cb1_s, cb2_s, cb3_s)]
    state_conv_sample = u_s.reshape(DEC_BATCH, DEC_SEQ, D_FF)[None, :, DEC_SEQ - (CONV_W - 1):]

    return (y_prompt, y_sample, cache_a_prompt, cache_a_sample, kvb_p[0], kvb_s[0], kvb_p[1], kvb_s[1],
            kvb_p[2], kvb_s[2], state_conv_prompt, state_conv_sample)
```

```python
import functools

import jax
import jax.numpy as jnp
import numpy as np
from jax import lax
from jax.experimental import pallas as pl
from jax.experimental.pallas import tpu as pltpu

D_MODEL = 1024
BATCH = 8
SEQ = 2048
DEPTH = 1
DEC_BATCH = 128
DEC_SEQ = 4
PAST_LEN = 16384
HEAD_DIM = 64
A_Q_HEADS = 8
A_KV_HEADS = 2
A_GROUP = A_Q_HEADS // A_KV_HEADS
A_WINDOW = 128
B_DIL = ((128, 1), (512, 4), (2048, 16))
B_SLOTS = 4
BLK = 128
ROPE_THETA = 10000.0
D_FF = ((8 * D_MODEL // 3 + 127) // 128) * 128
CONV_W = 3
ALPHA = (2 * DEPTH) ** 0.25
LN_EPS = 1e-5
NEG = -1e30
SCALE = HEAD_DIM ** -0.5
A_Q = A_Q_HEADS * HEAD_DIM
A_KV = A_KV_HEADS * HEAD_DIM
B_G = B_SLOTS * HEAD_DIM
N_QKV = A_Q + 2 * A_KV + 3 * 3 * B_G

LANES = 128
VMEM_LIMIT = 56 * 1024 * 1024

BF16 = jnp.bfloat16
F32 = jnp.float32

_C_QA = 0
_C_KVA = _C_QA + A_Q
_C_QB = (_C_KVA + 2 * A_KV, _C_KVA + 2 * A_KV + 3 * B_G, _C_KVA + 2 * A_KV + 6 * B_G)
_C_KVB = tuple(c + B_G for c in _C_QB)


def _const_spec(shape):
    nd = len(shape)
    return pl.BlockSpec(shape, lambda *_: (0,) * nd, pipeline_mode=pl.Buffered(1))


def _nt_dot(a, b):
    return lax.dot_general(a, b, (((1,), (1,)), ((), ())), preferred_element_type=F32)


def _dot(a, b):
    return jnp.dot(a, b, preferred_element_type=F32)


HALF = HEAD_DIM // 2


def _rope_split(y, cos4, sin4):
    x1, x2 = y[:, :LANES], y[:, LANES:]
    return jnp.concatenate([x1 * cos4 - x2 * sin4, x2 * cos4 + x1 * sin4], axis=1)


def _rope_rot64(y, cos4, sin_signed):
    outs = []
    for j in range(y.shape[1] // LANES):
        yj = y[:, j * LANES:(j + 1) * LANES]
        outs.append(yj * cos4 + pltpu.roll(yj, LANES // 2, 1) * sin_signed)
    return outs[0] if len(outs) == 1 else jnp.concatenate(outs, axis=1)


def _k_feature_rows(n_heads):
    return [(HEAD_DIM * h + HALF * part, n_heads * HALF * part + HALF * h)
            for part in range(2) for h in range(n_heads)]


def _qkv_kernel(x_ref, w_ref, cos_ref, sin_ref, *refs, tm, prompt, tiles_per_seq):
    out_refs, (xb_ref, y_ref, proj0_ref, proj1_ref) = refs[:-4], refs[-4:]
    q_refs, cache_refs, kv_refs = out_refs[0::3], out_refs[1::3], out_refs[2::3]
    groups = ((A_WINDOW, 1),) + B_DIL

    col_q = (_C_QA,) + _C_QB
    col_kv = (_C_KVA,) + _C_KVB
    width_q = (A_Q,) + (B_G,) * len(B_DIL)
    heads_kv = (A_KV_HEADS,) + (B_SLOTS,) * len(B_DIL)

    def project(proj_ref, c0, width):
        proj_ref[:, c0:c0 + width] = _dot(xb_ref[...], w_ref[:, c0:c0 + width])

    def put_cache_rows(cache_ref, lead, kv_t, n_heads):
        wk = n_heads * HEAD_DIM
        for ref_row, our_row in _k_feature_rows(n_heads):
            cache_ref[lead, ref_row:ref_row + HALF, :] = kv_t[our_row:our_row + HALF]
        cache_ref[lead, wk:2 * wk, :] = kv_t[wk:2 * wk]

    def to_planes(val):
        n_col = val.shape[1] // LANES
        for c in range(n_col):
            y_ref[c] = val[:, c * LANES:(c + 1) * LANES]
        return n_col

    def rope(y, n_heads):
        cos = cos_ref[...]
        sin = sin_ref[...]
        if n_heads == B_SLOTS:
            return _rope_split(y, cos, sin)
        lane = lax.broadcasted_iota(jnp.int32, (tm, LANES), 1)
        return _rope_rot64(y, cos, jnp.where(lane < LANES // 2, -sin, sin))

    def roped_q(proj_ref, g):
        return rope(proj_ref[:, col_q[g]:col_q[g] + width_q[g]], heads_kv[g]) * SCALE

    def roped_kv(proj_ref, g):
        wk = heads_kv[g] * HEAD_DIM
        return jnp.concatenate([rope(proj_ref[:, col_kv[g]:col_kv[g] + wk], heads_kv[g]),
                                proj_ref[:, col_kv[g] + wk:col_kv[g] + 2 * wk]], axis=1)

    def deinterleave(val, out_ref, dil):
        n_col = to_planes(val)
        for r in range(dil):
            out_ref[0, r] = jnp.concatenate(
                [y_ref[c, pl.ds(r, tm // dil, stride=dil), :] for c in range(n_col)], axis=1).astype(BF16)

    if not prompt:
        xb_ref[...] = x_ref[...].astype(BF16)
        project(proj0_ref, 0, N_QKV)
        n_seq = tm // DEC_SEQ
        for g in range(len(groups)):
            q_refs[g][...] = roped_q(proj0_ref, g).astype(BF16)
            kv = roped_kv(proj0_ref, g)
            kv_refs[g][...] = kv
            n_col = to_planes(kv)
            for t in range(DEC_SEQ):
                kv_t = jnp.concatenate(
                    [y_ref[c, pl.ds(t, n_seq, stride=DEC_SEQ), :].T for c in range(n_col)], axis=0)
                put_cache_rows(cache_refs[g], t, kv_t, heads_kv[g])
        return

    step = pl.program_id(0)
    tile = jnp.maximum(step - 1, 0)
    last_tile = (tile % tiles_per_seq) == tiles_per_seq - 1

    @pl.when(step == 0)
    def _():
        proj1_ref[...] = jnp.zeros_like(proj1_ref)

    def step_body(mine, other):
        xb_ref[...] = x_ref[...].astype(BF16)
        for g, (win, dil) in enumerate(groups):
            project(mine, col_q[g], width_q[g])
            q = roped_q(other, g)
            if dil == 1:
                q_refs[g][...] = q.astype(BF16)
            else:
                deinterleave(q, q_refs[g], dil)
            project(mine, col_kv[g], 2 * heads_kv[g] * HEAD_DIM)
            kv = roped_kv(other, g)
            if dil == 1:
                kv_refs[g][...] = kv.astype(BF16)
            else:
                deinterleave(kv, kv_refs[g], dil)
            if min(win, SEQ) == SEQ:
                put_cache_rows(cache_refs[g], 0, kv.T, heads_kv[g])

        @pl.when(last_tile)
        def _():
            for g, (win, _) in enumerate(groups):
                keep = min(win, SEQ)
                if keep < SEQ:
                    assert keep <= tm
                    put_cache_rows(cache_refs[g], 0, roped_kv(other, g)[tm - keep:, :].T, heads_kv[g])

    for par, (mine, other) in enumerate(((proj0_ref, proj1_ref), (proj1_ref, proj0_ref))):
        @pl.when(step % 2 == par)
        def _(mine=mine, other=other):
            step_body(mine, other)


def _qkv_call(x2d, w_qkv, cos_t, sin_t, *, tm, prompt, tiles_per_seq, name):
    m = x2d.shape[0]
    n_tiles = m // tm
    if prompt:
        tile = lambda i: jnp.maximum(i - 1, 0)
        x_spec = pl.BlockSpec((tm, D_MODEL), lambda i: (jnp.minimum(i, n_tiles - 1), 0))
    else:
        assert n_tiles == 1
        tile = lambda i: i
        x_spec = pl.BlockSpec((tm, D_MODEL), lambda i: (i, 0))
    row = lambda w: pl.BlockSpec((tm, w), lambda i: (tile(i), 0))
    tab = pl.BlockSpec((tm, LANES), lambda i: (tile(i) % tiles_per_seq, 0))
    in_specs = [x_spec, _const_spec((D_MODEL, N_QKV)), tab, tab]
    sds = jax.ShapeDtypeStruct
    out_shape, out_specs = [], []
    for g, (win, dil) in enumerate(((A_WINDOW, 1),) + B_DIL):
        wq = A_Q if g == 0 else B_G
        wkv = 2 * (A_KV if g == 0 else B_G)
        if not prompt:
            out_shape += [sds((m, wq), BF16), sds((DEC_SEQ, wkv, m // DEC_SEQ), F32), sds((m, wkv), F32)]
            out_specs += [row(wq), pl.BlockSpec((DEC_SEQ, wkv, m // DEC_SEQ), lambda i: (0, 0, 0)),
                          row(wkv)]
            continue
        n_seq = m // SEQ
        keep = min(win, SEQ)
        if keep == SEQ:
            cache_spec = pl.BlockSpec(
                (1, wkv, tm), lambda i: (tile(i) // tiles_per_seq, 0, tile(i) % tiles_per_seq))
        else:
            cache_spec = pl.BlockSpec((1, wkv, keep), lambda i: (tile(i) // tiles_per_seq, 0, 0))
        if dil == 1:
            out_shape += [sds((m, wq), BF16), sds((n_seq, wkv, keep), F32), sds((m, wkv), BF16)]
            out_specs += [row(wq), cache_spec, row(wkv)]
        else:
            dspec = lambda w, dil=dil: pl.BlockSpec(
                (1, dil, tm // dil, w),
                lambda i: (tile(i) // tiles_per_seq, 0, tile(i) % tiles_per_seq, 0))
            out_shape += [sds((n_seq, dil, SEQ // dil, wq), BF16), sds((n_seq, wkv, keep), F32),
                          sds((n_seq, dil, SEQ // dil, wkv), BF16)]
            out_specs += [dspec(wq), cache_spec, dspec(wkv)]
    args = [x2d, w_qkv, cos_t, sin_t]
    proj_bufs = 2 if prompt else 1
    return pl.pallas_call(
        functools.partial(_qkv_kernel, tm=tm, prompt=prompt, tiles_per_seq=tiles_per_seq),
        grid=(n_tiles + 1,) if prompt else (n_tiles,),
        in_specs=in_specs,
        out_specs=out_specs,
        out_shape=out_shape,
        scratch_shapes=[pltpu.VMEM((tm, D_MODEL), BF16),
                        pltpu.VMEM((2 * B_G // LANES, tm, LANES), F32)]
        + [pltpu.VMEM((tm, N_QKV) if k < proj_bufs else (8, LANES), F32) for k in range(2)],
        compiler_params=pltpu.CompilerParams(
            dimension_semantics=("arbitrary",), vmem_limit_bytes=VMEM_LIMIT),
        name=name,
    )(*args)


def _a_query_masks(rows):
    lane = lax.broadcasted_iota(jnp.int32, (rows, LANES), 1)
    return [jnp.where(((lane >> 5) & (A_KV_HEADS - 1)) == j, 1.0, 0.0).astype(BF16)
            for j in range(A_KV_HEADS)]


def _fold_masks():
    row = lax.broadcasted_iota(jnp.int32, (BLK, BLK), 0)
    col = lax.broadcasted_iota(jnp.int32, (BLK, BLK), 1)
    return col > row


def _attn_scratch(heads):
    return [pltpu.VMEM((2, heads * BLK, 2 * BLK), F32)]


def _attn_pipeline(n_units, bufs, *, heads, load_q, load_k, load_v, has_prev, sink_of_head, finish,
                   mxu_row_sum=False):
    (sbuf,) = bufs
    upper = _fold_masks()
    upper_bf = jnp.where(upper, 1.0, 0.0).astype(BF16)
    width = BLK if has_prev is None else 2 * BLK
    assert n_units % 2 == 0

    sbuf[1] = jnp.zeros(sbuf.shape[1:], F32)

    def scores(u, par):
        sbuf[par, :, 0:width] = _nt_dot(load_q(u), load_k(u))

    def softmax_values(u, par):
        prev_ok = None if has_prev is None else has_prev(u)
        ps, ms, ls = [], [], []
        for h in range(heads):
            rows = slice(h * BLK, (h + 1) * BLK)
            if has_prev is None:
                sf = jnp.where(upper, NEG, sbuf[par, rows, 0:BLK])
            else:
                sf = jnp.where(upper, jnp.where(prev_ok, sbuf[par, rows, 0:BLK], NEG),
                               sbuf[par, rows, BLK:2 * BLK])
            mx = jnp.max(sf, axis=-1, keepdims=True)
            if sink_of_head is not None:
                mx = jnp.maximum(mx, sink_of_head(h))
            e = jnp.exp(sf - mx)
            if not mxu_row_sum:
                ls.append(jnp.broadcast_to(jnp.sum(e, axis=-1, keepdims=True), (BLK, LANES)))
            e = e.astype(BF16)
            if has_prev is None:
                ps.append(e)
            else:
                p_prev = e * upper_bf
                ps.append(jnp.concatenate([p_prev, e - p_prev], axis=1))
            ms.append(jnp.broadcast_to(mx, (BLK, LANES)))
        finish(u, _dot(jnp.concatenate(ps, axis=0), load_v(u)), jnp.concatenate(ms, axis=0),
               jnp.concatenate(ls, axis=0) if ls else None)

    def trip_pair(t, carry):
        for par in (0, 1):
            i = 2 * t + par
            scores(i, par)
            softmax_values(jnp.maximum(i - 1, 0), 1 - par)
        return carry

    lax.fori_loop(0, n_units // 2, trip_pair, 0)
    lax.fori_loop(n_units - 1, n_units, lambda u, carry: (softmax_values(u, 1), carry)[1], 0)


def _attn_a_kernel(sink_ref, q_ref, kv_ref, o_ref, *bufs):
    n_blk = q_ref.shape[0] // BLK
    lane = lax.broadcasted_iota(jnp.int32, (BLK, LANES), 1)
    hi = lane >= HEAD_DIM
    mask_bf = _a_query_masks(BLK)

    def rows_of(b):
        return pl.ds(pl.multiple_of(b * BLK, BLK), BLK)

    def prev_cur(b, cols):
        return jnp.concatenate([kv_ref[rows_of(jnp.maximum(b - 1, 0)), cols], kv_ref[rows_of(b), cols]],
                               axis=0)

    def load_q(b):
        qblk = q_ref[rows_of(b), :]
        return jnp.concatenate([qblk[:, g * LANES:(g + 1) * LANES] * mask_bf[j]
                                for j in range(A_KV_HEADS) for g in range(A_GROUP)], axis=0)

    ones = jnp.ones((2 * BLK, LANES), BF16)

    def load_v(b):
        return jnp.concatenate([prev_cur(b, slice(A_KV, 2 * A_KV)), ones], axis=1)

    def finish(b, o, m, _):
        def normalised(h):
            rows = slice(h * BLK, (h + 1) * BLK)
            den = o[rows, LANES:] + jnp.exp(sink_ref[h] - m[rows])
            return o[rows, :LANES] / den

        for g in range(A_GROUP):
            o_ref[rows_of(b), g * LANES:(g + 1) * LANES] = jnp.where(
                hi, normalised(A_GROUP + g), normalised(g)).astype(BF16)

    _attn_pipeline(n_blk, bufs, heads=A_Q_HEADS, load_q=load_q,
                   load_k=lambda b: prev_cur(b, slice(0, A_KV)), load_v=load_v,
                   has_prev=lambda b: b > 0, sink_of_head=lambda h: sink_ref[h], finish=finish,
                   mxu_row_sum=True)


def _attn_a_call(sink, q_a, kva_bf):
    n_seq = q_a.shape[0] // SEQ
    return pl.pallas_call(
        _attn_a_kernel,
        grid=(n_seq,),
        in_specs=[pl.BlockSpec(memory_space=pltpu.SMEM),
                  pl.BlockSpec((SEQ, A_Q), lambda n: (n, 0)),
                  pl.BlockSpec((SEQ, 2 * A_KV), lambda n: (n, 0))],
        out_specs=pl.BlockSpec((SEQ, A_Q), lambda n: (n, 0)),
        out_shape=jax.ShapeDtypeStruct(q_a.shape, BF16),
        scratch_shapes=_attn_scratch(A_Q_HEADS),
        compiler_params=pltpu.CompilerParams(
            dimension_semantics=("arbitrary",), vmem_limit_bytes=VMEM_LIMIT),
        name="attn_a_prompt",
    )(sink, q_a, kva_bf)


def _slot_masks(rows):
    lane = lax.broadcasted_iota(jnp.int32, (rows, B_G), 1)
    masks = [(lane >> 6) == s for s in range(B_SLOTS)]
    masks_bf = [jnp.where(((lane >> 5) & (B_SLOTS - 1)) == s, 1.0, 0.0).astype(BF16) for s in range(B_SLOTS)]
    return masks, masks_bf


def _attn_b_kernel(q1_ref, kv1_ref, q2_ref, kv2_ref, q3_ref, kv3_ref, o_ref,
                   acc2_ref, m2_ref, l2_ref, acc3_ref, m3_ref, l3_ref, *bufs):
    masks, masks_bf = _slot_masks(BLK)
    d2, d3 = B_DIL[1][1], B_DIL[2][1]
    nb2 = SEQ // d2 // BLK
    assert SEQ // d3 == BLK
    n_plane = B_G // LANES

    def put(ref, rows, val):
        for c in range(n_plane):
            ref[c, rows, :] = val[:, c * LANES:(c + 1) * LANES]

    def get(ref, rows):
        return jnp.concatenate([ref[c, rows, :] for c in range(n_plane)], axis=1)

    def stack_q(qblk):
        return jnp.concatenate([qblk * masks_bf[s] for s in range(B_SLOTS)], axis=0)

    def unstack(o, m, l):
        acc = jnp.where(masks[0], o[0:BLK], 0.0)
        rep = lambda x, s: jnp.concatenate([x[s * BLK:(s + 1) * BLK]] * n_plane, axis=1)
        mf, lf = rep(m, 0), rep(l, 0)
        for s in range(1, B_SLOTS):
            acc = jnp.where(masks[s], o[s * BLK:(s + 1) * BLK], acc)
            mf = jnp.where(masks[s], rep(m, s), mf)
            lf = jnp.where(masks[s], rep(l, s), lf)
        return acc, mf, lf

    def rows_of(b):
        return pl.ds(pl.multiple_of(b * BLK, BLK), BLK)

    run = functools.partial(_attn_pipeline, bufs=bufs, heads=B_SLOTS, sink_of_head=None)

    def finish3(r, o, m, l):
        rows = pl.ds(r, BLK, stride=d3)
        for ref, val in zip((acc3_ref, m3_ref, l3_ref), unstack(o, m, l)):
            put(ref, rows, val)

    run(d3, load_q=lambda r: stack_q(q3_ref[r]), load_k=lambda r: kv3_ref[r, :, 0:B_G],
        load_v=lambda r: kv3_ref[r, :, B_G:2 * B_G], has_prev=None, finish=finish3)

    def prev_cur2(u, cols):
        r, b = u // nb2, u % nb2
        return jnp.concatenate([kv2_ref[r, rows_of(jnp.maximum(b - 1, 0)), cols], kv2_ref[r, rows_of(b), cols]],
                               axis=0)

    def finish2(u, o, m, l):
        r, b = u // nb2, u % nb2
        rows = pl.ds(r + b * (BLK * d2), BLK, stride=d2)
        for ref, val in zip((acc2_ref, m2_ref, l2_ref), unstack(o, m, l)):
            put(ref, rows, val)

    run(d2 * nb2, load_q=lambda u: stack_q(q2_ref[u // nb2, rows_of(u % nb2), :]),
        load_k=lambda u: prev_cur2(u, slice(0, B_G)), load_v=lambda u: prev_cur2(u, slice(B_G, 2 * B_G)),
        has_prev=lambda u: (u % nb2) > 0, finish=finish2)

    def prev_cur1(b, cols):
        return jnp.concatenate([kv1_ref[rows_of(jnp.maximum(b - 1, 0)), cols], kv1_ref[rows_of(b), cols]],
                               axis=0)

    def finish1(b, o, m, l):
        acc1, m1, l1 = unstack(o, m, l)
        rows = rows_of(b)
        m2, m3 = get(m2_ref, rows), get(m3_ref, rows)
        mx = jnp.maximum(jnp.maximum(m1, m2), m3)
        w1, w2, w3 = jnp.exp(m1 - mx), jnp.exp(m2 - mx), jnp.exp(m3 - mx)
        num = w1 * acc1 + w2 * get(acc2_ref, rows) + w3 * get(acc3_ref, rows)
        den = w1 * l1 + w2 * get(l2_ref, rows) + w3 * get(l3_ref, rows)
        o_ref[rows, :] = (num / den).astype(BF16)

    run(SEQ // BLK, load_q=lambda b: stack_q(q1_ref[rows_of(b), :]),
        load_k=lambda b: prev_cur1(b, slice(0, B_G)), load_v=lambda b: prev_cur1(b, slice(B_G, 2 * B_G)),
        has_prev=lambda b: b > 0, finish=finish1)


def _attn_b_call(q1, kv1, q2, kv2, q3, kv3):
    n_seq = q1.shape[0] // SEQ
    d2, d3 = B_DIL[1][1], B_DIL[2][1]
    rows = lambda w: pl.BlockSpec((SEQ, w), lambda n: (n, 0))
    dsp = lambda d, w: pl.BlockSpec((None, d, SEQ // d, w), lambda n: (n, 0, 0, 0))
    return pl.pallas_call(
        _attn_b_kernel,
        grid=(n_seq,),
        in_specs=[rows(B_G), rows(2 * B_G), dsp(d2, B_G), dsp(d2, 2 * B_G), dsp(d3, B_G),
                  dsp(d3, 2 * B_G)],
        out_specs=rows(B_G),
        out_shape=jax.ShapeDtypeStruct((q1.shape[0], B_G), BF16),
        scratch_shapes=[pltpu.VMEM((B_G // LANES, SEQ, LANES), F32) for _ in range(6)]
        + _attn_scratch(B_SLOTS),
        compiler_params=pltpu.CompilerParams(
            dimension_semantics=("arbitrary",), vmem_limit_bytes=VMEM_LIMIT),
        name="attn_b_prompt",
    )(q1, kv1, q2, kv2, q3, kv3)


def _sample_attn_block(sink_ref, qa_ref, kna_ref, ca_ref, q1_ref, kn1_ref, c1_ref,
                       q2_ref, kn2_ref, c2_ref, q3_ref, kn3_ref, c3_ref, *, tt, seqs, emit):
    pad = BLK - tt
    tt_shift, seq_shift = tt.bit_length() - 1, DEC_SEQ.bit_length() - 1
    assert tt == 1 << tt_shift and DEC_SEQ == 1 << seq_shift

    def pad_rows(x):
        return jnp.concatenate([x.astype(BF16), jnp.zeros((pad, x.shape[1]), BF16)], axis=0)

    def window_group(qm, kn_ref, c_ref, dil, sinkv, finish):
        rows, kd = qm.shape
        lc = c_ref.shape[2]
        kn = kn_ref[...]
        knpad = pad_rows(kn[:, :kd])
        vnpad = pad_rows(kn[:, kd:])
        s_new = _nt_dot(qm, knpad)
        def masks(width):
            col = lax.broadcasted_iota(jnp.int32, (rows, width), 1)
            t = lax.broadcasted_iota(jnp.int32, (rows, width), 0) & (DEC_SEQ - 1)
            same_res = (col & (dil - 1)) == (t & (dil - 1))
            return (col > t) & same_res, (col <= t) & same_res

        valid, _ = masks(lc)
        _, is_new = masks(BLK)
        k_rows = sorted(_k_feature_rows(kd // HEAD_DIM), key=lambda rows: rows[1])
        scores = []
        for pos, _ in seqs:
            k_t = jnp.concatenate([c_ref[pos, ref_row:ref_row + HALF, :] for ref_row, _ in k_rows],
                                  axis=0).astype(BF16)
            scores.append(_dot(qm, k_t))
        yield
        probs = []
        for score, (_, n) in zip(scores, seqs):
            s_c = jnp.where(valid, score, NEG)
            s_n = pltpu.roll(s_new, BLK - DEC_SEQ * n, 1) if n else s_new
            first = jnp.where(is_new, s_n, s_c[:, :BLK])
            s = first if lc == BLK else jnp.concatenate([first, s_c[:, BLK:]], axis=1)
            mx = jnp.max(s, axis=-1, keepdims=True)
            if sinkv is not None:
                mx = jnp.maximum(mx, sinkv)
            e = jnp.exp(s - mx)
            l = jnp.sum(e, axis=-1, keepdims=True)
            if sinkv is not None:
                l = l + jnp.exp(sinkv - mx)
            p_new = jnp.where(is_new, e[:, :BLK], 0.0)
            if n:
                p_new = pltpu.roll(p_new, DEC_SEQ * n, 1)
            probs.append((jnp.where(valid, e, 0.0).astype(BF16), p_new.astype(BF16), mx, l))
        yield
        for (p_c, p_new, mx, l), (pos, n) in zip(probs, seqs):
            v_t = c_ref[pos, kd:2 * kd, :].astype(BF16)
            finish(n, _nt_dot(p_c, v_t) + _dot(p_new, vnpad), mx, l)

    rows_a = A_Q_HEADS * tt
    lane_t = lax.broadcasted_iota(jnp.int32, (tt, LANES), 1)
    hi_t = lane_t >= HEAD_DIM
    mask_bf = _a_query_masks(tt)
    qa = qa_ref[...]
    qm = jnp.concatenate([qa[:, g * LANES:(g + 1) * LANES] * mask_bf[j]
                          for j in range(A_KV_HEADS) for g in range(A_GROUP)], axis=0)
    head = lax.broadcasted_iota(jnp.int32, (rows_a, 1), 0) >> tt_shift
    sinkv = jnp.zeros((rows_a, 1), F32)
    for h in range(A_Q_HEADS):
        sinkv = jnp.where(head == h, sink_ref[h], sinkv)
    seq_t = lax.broadcasted_iota(jnp.int32, (tt, LANES), 0) >> seq_shift
    out_a = [jnp.zeros((tt, LANES), F32) for _ in range(A_GROUP)]

    def finish_a(n, o, mx, l):
        o = o / l
        for g in range(A_GROUP):
            og = jnp.where(hi_t, o[(A_GROUP + g) * tt:(A_GROUP + g + 1) * tt], o[g * tt:(g + 1) * tt])
            out_a[g] = jnp.where(seq_t == n, og, out_a[g])

    stages = [window_group(qm, kna_ref, ca_ref, 1, sinkv, finish_a)]

    masks_t, masks_bf_t = _slot_masks(tt)
    tok_t = lax.broadcasted_iota(jnp.int32, (tt, B_G), 0)

    def stack_q(q):
        return jnp.concatenate([q * masks_bf_t[s] for s in range(B_SLOTS)], axis=0)

    def unstack(o, mx, l):
        acc = jnp.where(masks_t[0], o[0:tt], 0.0)
        mf = jnp.broadcast_to(mx[0:tt], (tt, B_G))
        lf = jnp.broadcast_to(l[0:tt], (tt, B_G))
        for s in range(1, B_SLOTS):
            acc = jnp.where(masks_t[s], o[s * tt:(s + 1) * tt], acc)
            mf = jnp.where(masks_t[s], mx[s * tt:(s + 1) * tt], mf)
            lf = jnp.where(masks_t[s], l[s * tt:(s + 1) * tt], lf)
        return acc, mf, lf

    zero, one = jnp.zeros((tt, B_G), F32), jnp.ones((tt, B_G), F32)
    stats = []
    for (win, dil), q_ref, kn_ref, c_ref in zip(B_DIL, (q1_ref, q2_ref, q3_ref),
                                                (kn1_ref, kn2_ref, kn3_ref), (c1_ref, c2_ref, c3_ref)):
        assert c_ref.shape[2] == win
        group = [zero, zero, one]

        def finish_b(n, o, mx, l, group=group):
            sel = (tok_t >> seq_shift) == n
            for k, new in enumerate(unstack(o, mx, l)):
                group[k] = jnp.where(sel, new, group[k])

        stages.append(window_group(stack_q(q_ref[...]), kn_ref, c_ref, dil, None, finish_b))
        stats.append(group)

    for phase in range(3):
        for stage in stages:
            next(stage, None)
        if phase < 2:
            yield
    (acc1, m1, l1), (acc2, m2, l2), (acc3, m3, l3) = stats
    mx = jnp.maximum(jnp.maximum(m1, m2), m3)
    w1, w2, w3 = jnp.exp(m1 - mx), jnp.exp(m2 - mx), jnp.exp(m3 - mx)
    emit(out_a, (w1 * acc1 + w2 * acc2 + w3 * acc3) / (w1 * l1 + w2 * l2 + w3 * l3))


def _sample_attn_rider(in_refs, oa_ref, ob_ref, k, n_seqs):
    seq_shift = DEC_SEQ.bit_length() - 1

    def merge(ref, cols, new):
        if k:
            mine = (lax.broadcasted_iota(jnp.int32, new.shape, 0) >> seq_shift) == k
            new = jnp.where(mine, new, ref[:, cols].astype(F32))
        ref[:, cols] = new.astype(BF16)

    def emit(out_a, out_b):
        for c in range(A_GROUP):
            merge(oa_ref, slice(c * LANES, (c + 1) * LANES), out_a[c])
        merge(ob_ref, slice(0, B_G), out_b)

    yield from _sample_attn_block(*in_refs, tt=n_seqs * DEC_SEQ, seqs=[(0, k)], emit=emit)


def _layernorm(x, g, b):
    mu = jnp.mean(x, axis=-1, keepdims=True)
    xc = x - mu
    var = jnp.mean(xc * xc, axis=-1, keepdims=True)
    return xc * lax.rsqrt(var + LN_EPS) * g + b


def _gelu_exact(x):
    return 0.5 * x * (1.0 + lax.erf(x * (0.5 ** 0.5)))


_FF_CHUNK = 256
_CARRY = 8


def _ffn_kernel(*refs, tm, sample, tiles_per_seq):
    if sample:
        (x_ref, oa_ref, ob_ref, fill_ref, wg_ref, wa_ref, wb_ref, wo_ref, g1_ref, b1_ref,
         wup_ref, cw_ref, cb_ref, wdn_ref, g2_ref, b2_ref, y_ref, u_ref,
         xb_ref, m_ref, gg_ref, ext_ref, h0_ref, hb0_ref) = refs
    else:
        x_ref, oa_ref, ob_ref = refs[:3]
        rider_in, refs = refs[3:3 + _N_SAMPLE_ATTN_IN], refs[3 + _N_SAMPLE_ATTN_IN:]
        (wg_ref, wa_ref, wb_ref, wo_ref, g1_ref, b1_ref,
         wup_ref, cw_ref, cb_ref, wdn_ref, g2_ref, b2_ref, y_ref, ulast_ref, oa_s_ref, ob_s_ref,
         xb_ref, m_ref, gg_ref, ext_ref, h0_ref, hb0_ref, carry_ref) = refs
    half = D_MODEL // 2

    def merge_pieces(h_ref, hb_ref):
        def gate_half(c):
            if c == 0:
                xb_ref[...] = x_ref[...].astype(BF16)
            cs = slice(c * half, (c + 1) * half)
            ga = _dot(xb_ref[...], wg_ref[:, c * half:(c + 1) * half])
            gb = _dot(xb_ref[...], wg_ref[:, D_MODEL + c * half:D_MODEL + (c + 1) * half])
            ta = _dot(oa_ref[...], wa_ref[:, cs])
            tb = _dot(ob_ref[...], wb_ref[:, cs])
            m_ref[:, cs] = (jax.nn.sigmoid(ga) * ta + jax.nn.sigmoid(gb) * tb).astype(BF16)

        def out_proj():
            mix = _dot(m_ref[...], wo_ref[...])
            h = _layernorm(ALPHA * x_ref[...] + mix, g1_ref[...], b1_ref[...])
            h_ref[...] = h
            hb_ref[...] = h.astype(BF16)

        return [functools.partial(gate_half, 0), functools.partial(gate_half, 1), out_proj]

    def ffn_pieces(h_ref, hb_ref):
        def chunk(c):
            cs = slice(c * _FF_CHUNK, (c + 1) * _FF_CHUNK)
            u = _dot(hb_ref[...], wup_ref[:, c * _FF_CHUNK:(c + 1) * _FF_CHUNK])
            v = _dot(hb_ref[...], wup_ref[:, D_FF + c * _FF_CHUNK:D_FF + (c + 1) * _FF_CHUNK])
            if sample:
                ext_ref[0:_CARRY, :] = jnp.zeros((_CARRY, _FF_CHUNK), F32)
            else:
                ext_ref[0:_CARRY, :] = carry_ref[:, cs]
            ext_ref[_CARRY:_CARRY + tm, :] = u
            u1 = ext_ref[_CARRY - 1:_CARRY - 1 + tm, :]
            u2 = ext_ref[_CARRY - 2:_CARRY - 2 + tm, :]
            if sample:
                t = lax.broadcasted_iota(jnp.int32, (tm, _FF_CHUNK), 0) & (DEC_SEQ - 1)
                fill = fill_ref[:, cs]
                u1 = jnp.where(t >= 1, u1, pltpu.roll(fill, tm - 1, 0))
                u2 = jnp.where(t >= 2, u2, fill)
                u_ref[:, cs] = u
            else:
                tail = u[tm - _CARRY:tm, :]
                carry_ref[:, cs] = tail
                ulast_ref[0, :, cs] = tail
            a = cb_ref[:, cs] + cw_ref[0:1, cs] * u2 + cw_ref[1:2, cs] * u1 + cw_ref[2:3, cs] * u
            gg_ref[:, cs] = (_gelu_exact(a) * v).astype(BF16)

        def down():
            f = _dot(gg_ref[...], wdn_ref[...])
            y_ref[...] = _layernorm(ALPHA * h_ref[...] + f, g2_ref[...], b2_ref[...])

        return [functools.partial(chunk, c) for c in range(D_FF // _FF_CHUNK)] + [down]

    pieces = merge_pieces(h0_ref, hb0_ref) + ffn_pieces(h0_ref, hb0_ref)
    if sample:
        for piece in pieces:
            piece()
        return

    step, sub = pl.program_id(0), pl.program_id(1)

    @pl.when((step % tiles_per_seq == 0) & (sub == 0))
    def _():
        carry_ref[...] = jnp.zeros_like(carry_ref)

    n_chunk = D_FF // _FF_CHUNK
    phases = (pieces[:3], pieces[3:3 + n_chunk // 2], pieces[3 + n_chunk // 2:3 + n_chunk], pieces[3 + n_chunk:])
    assert len(phases) == _FFN_PHASES
    for g, phase in enumerate(phases):
        @pl.when(sub == g)
        def _(g=g, phase=phase):
            rider = _sample_attn_rider(rider_in, oa_s_ref, ob_s_ref, g, _FFN_PHASES)
            for piece in phase:
                next(rider, None)
                piece()
            for _ in rider:
                pass


_FFN_PHASES = 4
_N_SAMPLE_ATTN_IN = 13


def _ffn_call(x2d, oa, ob, weights, *, tm, sample, tiles_per_seq, conv_fill=None, sample_attn_args=(), name):
    m = x2d.shape[0]
    n_tiles = m // tm
    sds = jax.ShapeDtypeStruct
    row_in = row_out = lambda w: pl.BlockSpec((tm, w), lambda i, *_: (i, 0))
    in_specs = [row_in(D_MODEL), row_in(A_Q), row_in(B_G)]
    args = [x2d, oa, ob]
    if sample:
        in_specs += [row_in(D_FF)]
        args += [conv_fill]
    else:
        assert len(sample_attn_args) == _N_SAMPLE_ATTN_IN
        tt = _FFN_PHASES * DEC_SEQ
        n_tok = sample_attn_args[1].shape[0]
        assert n_tok == n_tiles * tt
        in_specs.append(pl.BlockSpec(memory_space=pltpu.SMEM))
        for k, a in enumerate(sample_attn_args[1:]):
            if k % 3 == 2:
                in_specs.append(pl.BlockSpec((1,) + a.shape[1:], lambda i, g: (_FFN_PHASES * i + g, 0, 0)))
            else:
                in_specs.append(pl.BlockSpec((tt, a.shape[1]), lambda i, g: (i, 0)))
        args += list(sample_attn_args)
    in_specs += [_const_spec(w.shape) for w in weights]
    args += list(weights)
    h_bufs = [pltpu.VMEM((tm, D_MODEL), F32), pltpu.VMEM((tm, D_MODEL), BF16)]
    scratch = [pltpu.VMEM((tm, D_MODEL), BF16), pltpu.VMEM((tm, D_MODEL), BF16),
               pltpu.VMEM((tm, D_FF), BF16), pltpu.VMEM((tm + _CARRY, _FF_CHUNK), F32)] + h_bufs
    if sample:
        out_shape = [sds((m, D_MODEL), F32), sds((m, D_FF), F32)]
        out_specs = [row_out(D_MODEL), row_out(D_FF)]
    else:
        out_shape = [sds((m, D_MODEL), F32), sds((n_tiles, _CARRY, D_FF), F32),
                     sds((n_tok, A_Q), BF16), sds((n_tok, B_G), BF16)]
        out_specs = [row_out(D_MODEL), pl.BlockSpec((1, _CARRY, D_FF), lambda i, g: (i, 0, 0)),
                     pl.BlockSpec((tt, A_Q), lambda i, g: (i, 0)), pl.BlockSpec((tt, B_G), lambda i, g: (i, 0))]
        scratch += [pltpu.VMEM((_CARRY, D_FF), F32)]
    grid = (n_tiles,) if sample else (n_tiles, _FFN_PHASES)
    return pl.pallas_call(
        functools.partial(_ffn_kernel, tm=tm, sample=sample, tiles_per_seq=tiles_per_seq),
        grid=grid,
        in_specs=in_specs,
        out_specs=out_specs,
        out_shape=out_shape,
        scratch_shapes=scratch,
        compiler_params=pltpu.CompilerParams(
            dimension_semantics=("arbitrary",) * len(grid), vmem_limit_bytes=VMEM_LIMIT),
        name=name,
    )(*args)


def _rope_tables(pos):
    half = HEAD_DIM // 2
    inv = ROPE_THETA ** (-jnp.arange(half, dtype=F32) / half)
    ang = pos.astype(F32)[:, None] * inv[None, :]
    cos, sin = jnp.cos(ang), jnp.sin(ang)
    reps = LANES // HALF
    return jnp.tile(cos, (1, reps)), jnp.tile(sin, (1, reps))


def _rotary_order(w, heads):
    x1 = [w[:, h * HEAD_DIM:h * HEAD_DIM + HALF] for h in heads]
    x2 = [w[:, h * HEAD_DIM + HALF:(h + 1) * HEAD_DIM] for h in heads]
    return x1 + x2


def _qkv_source_columns():
    w_in = np.arange(N_QKV, dtype=np.int32)[None, :]
    qa = w_in[:, 0:A_Q]
    ka = w_in[:, A_Q:A_Q + A_KV]
    va = w_in[:, A_Q + A_KV:A_Q + 2 * A_KV]
    b0 = A_Q + 2 * A_KV
    qb = w_in[:, b0:b0 + 3 * B_G]
    kb = w_in[:, b0 + 3 * B_G:b0 + 6 * B_G]
    vb = w_in[:, b0 + 6 * B_G:b0 + 9 * B_G]
    cols = []
    for c in range(A_GROUP):
        cols += _rotary_order(qa, (c, A_GROUP + c))
    cols += _rotary_order(ka, range(A_KV_HEADS)) + [va]
    for g in range(len(B_DIL)):
        gs = slice(g * B_G, (g + 1) * B_G)
        cols += _rotary_order(qb[:, gs], range(B_SLOTS)) + _rotary_order(kb[:, gs], range(B_SLOTS)) + [vb[:, gs]]
    return np.concatenate(cols, axis=1)[0]


def _in_weight_kernel(src_ref, w_ref, qkv_ref, gate_ref):
    src = _qkv_source_columns()
    rows = w_ref.shape[0]
    lane = lax.broadcasted_iota(jnp.int32, (LANES, LANES), 0)
    for j in range(N_QKV // LANES):
        want = src[j * LANES:(j + 1) * LANES]
        if np.array_equal(want, want[0] + np.arange(LANES)) and want[0] % LANES == 0:
            qkv_ref[:, j * LANES:(j + 1) * LANES] = w_ref[:, want[0]:want[0] + LANES].astype(BF16)
            continue
        acc = jnp.zeros((rows, LANES), F32)
        for b in sorted(set(int(c) // LANES for c in want)):
            pick = (lane + b * LANES) == src_ref[:, j * LANES:(j + 1) * LANES]
            acc = acc + _dot(w_ref[:, b * LANES:(b + 1) * LANES].astype(BF16), jnp.where(pick, 1.0, 0.0).astype(BF16))
        qkv_ref[:, j * LANES:(j + 1) * LANES] = acc.astype(BF16)
    gate_ref[...] = w_ref[:, N_QKV:].astype(BF16)


def _prep_in_weights(w_in):
    rows = 256
    n_proj = w_in.shape[1]
    src = jnp.asarray(_qkv_source_columns())[None, :]
    return pl.pallas_call(
        _in_weight_kernel,
        grid=(D_MODEL // rows,),
        in_specs=[pl.BlockSpec((1, N_QKV), lambda i: (0, 0)), pl.BlockSpec((rows, n_proj), lambda i: (i, 0))],
        out_specs=[pl.BlockSpec((rows, N_QKV), lambda i: (i, 0)),
                   pl.BlockSpec((rows, n_proj - N_QKV), lambda i: (i, 0))],
        out_shape=[jax.ShapeDtypeStruct((D_MODEL, N_QKV), BF16),
                   jax.ShapeDtypeStruct((D_MODEL, n_proj - N_QKV), BF16)],
        compiler_params=pltpu.CompilerParams(dimension_semantics=("arbitrary",), vmem_limit_bytes=VMEM_LIMIT),
        name="in_weights",
    )(src, w_in)


def _prep_br_a(w_br_a):
    rows = []
    for c in range(A_GROUP):
        rows += [w_br_a[c * HEAD_DIM:(c + 1) * HEAD_DIM],
                 w_br_a[(A_GROUP + c) * HEAD_DIM:(A_GROUP + c + 1) * HEAD_DIM]]
    return jnp.concatenate(rows, axis=0).astype(BF16)


_TM_PROMPT = 512


def kernel(x_prompt, x_sample, cache_a, cache_b1, cache_b2, cache_b3, state_conv, w_in, sink_a, w_br_a, w_br_b, w_o, ln1_g, ln1_b, w_up, conv_w, conv_b, w_down, ln2_g, ln2_b):
    assert DEPTH == 1
    l = 0
    w_qkv, w_gate = _prep_in_weights(w_in[l])
    ffn_weights = (w_gate, _prep_br_a(w_br_a[l]), w_br_b[l].astype(BF16), w_o[l].astype(BF16),
                   ln1_g[l][None], ln1_b[l][None], w_up[l].astype(BF16), conv_w[l], conv_b[l][None],
                   w_down[l].astype(BF16), ln2_g[l][None], ln2_b[l][None])
    sink = sink_a[l].astype(F32)

    ms = DEC_BATCH * DEC_SEQ
    xs = x_sample.reshape(ms, D_MODEL)
    cos_s, sin_s = _rope_tables(PAST_LEN + (jnp.arange(ms, dtype=jnp.int32) % DEC_SEQ))
    (qa_s, ca_s, kva_s, qb1_s, cb1_s, kvb1_s, qb2_s, cb2_s, kvb2_s, qb3_s, cb3_s, kvb3_s) = _qkv_call(
        xs, w_qkv, cos_s, sin_s, tm=ms, prompt=False, tiles_per_seq=1, name="qkv_sample")

    def window_buffer(c):
        return c.transpose(0, 2, 3, 4, 1).reshape(c.shape[0], -1, c.shape[1])

    ca = window_buffer(cache_a[l])
    cb = [window_buffer(c[l]) for c in (cache_b1, cache_b2, cache_b3)]
    sample_attn_args = (sink, qa_s, kva_s, ca, qb1_s, kvb1_s, cb[0], qb2_s, kvb2_s, cb[1], qb3_s, kvb3_s, cb[2])

    mp = BATCH * SEQ
    xp = x_prompt.reshape(mp, D_MODEL)
    cos_p, sin_p = _rope_tables(jnp.arange(SEQ, dtype=jnp.int32))
    tps = SEQ // _TM_PROMPT
    (qa, ca_p, kva_bf, qb1, cb1_p, kvb1_bf, qb2, cb2_p, kvb2_bf, qb3, cb3_p, kvb3_bf) = _qkv_call(
        xp, w_qkv, cos_p, sin_p, tm=_TM_PROMPT, prompt=True, tiles_per_seq=tps, name="qkv_prompt")
    oa = _attn_a_call(sink, qa, kva_bf)
    ob = _attn_b_call(qb1, kvb1_bf, qb2, kvb2_bf, qb3, kvb3_bf)
    y_p, ulast, oa_s, ob_s = _ffn_call(xp, oa, ob, ffn_weights, tm=_TM_PROMPT, sample=False, tiles_per_seq=tps,
                                       sample_attn_args=sample_attn_args, name="ffn_prompt")

    def prompt_cache(c, heads):
        return c.reshape(BATCH, 2, heads, HEAD_DIM, c.shape[-1]).transpose(0, 4, 1, 2, 3)[None]

    y_prompt = y_p.reshape(BATCH, SEQ, D_MODEL)
    cache_a_prompt = prompt_cache(ca_p, A_KV_HEADS)
    kvb_p = [prompt_cache(c, B_SLOTS) for c in (cb1_p, cb2_p, cb3_p)]
    state_conv_prompt = ulast.reshape(BATCH, tps, _CARRY, D_FF)[None, :, tps - 1, _CARRY - (CONV_W - 1):]

    fill = jnp.pad(state_conv[l], ((0, 0), (0, DEC_SEQ - (CONV_W - 1)), (0, 0))).reshape(ms, D_FF)
    y_s, u_s = _ffn_call(xs, oa_s, ob_s, ffn_weights, tm=ms, sample=True, tiles_per_seq=1,
                         conv_fill=fill, name="ffn_sample")

    def sample_cache(c, heads):
        return c.reshape(DEC_SEQ, 2, heads, HEAD_DIM, DEC_BATCH).transpose(4, 0, 1, 2, 3)[None]

    y_sample = y_s.reshape(DEC_BATCH, DEC_SEQ, D_MODEL)
    cache_a_sample = sample_cache(ca_s, A_KV_HEADS)
    kvb_s = [sample_cache(c, B_SLOTS) for c in (cb1_s, cb2_s, cb3_s)]
    state_conv_sample = u_s.reshape(DEC_BATCH, DEC_SEQ, D_FF)[None, :, DEC_SEQ - (CONV_W - 1):]

    return (y_prompt, y_sample, cache_a_prompt, cache_a_sample, kvb_p[0], kvb_s[0], kvb_p[1], kvb_s[1],
            kvb_p[2], kvb_s[2], state_conv_prompt, state_conv_sample)
```

```python
import functools

import jax
import jax.numpy as jnp
import numpy as np
from jax import lax
from jax.experimental import pallas as pl
from jax.experimental.pallas import tpu as pltpu

D_MODEL = 1024
BATCH = 8
SEQ = 2048
DEPTH = 1
DEC_BATCH = 128
DEC_SEQ = 4
PAST_LEN = 16384
HEAD_DIM = 64
A_Q_HEADS = 8
A_KV_HEADS = 2
A_GROUP = A_Q_HEADS // A_KV_HEADS
A_WINDOW = 128
B_DIL = ((128, 1), (512, 4), (2048, 16))
B_SLOTS = 4
BLK = 128
ROPE_THETA = 10000.0
D_FF = ((8 * D_MODEL // 3 + 127) // 128) * 128
CONV_W = 3
ALPHA = (2 * DEPTH) ** 0.25
LN_EPS = 1e-5
NEG = -1e30
SCALE = HEAD_DIM ** -0.5
A_Q = A_Q_HEADS * HEAD_DIM
A_KV = A_KV_HEADS * HEAD_DIM
B_G = B_SLOTS * HEAD_DIM
N_QKV = A_Q + 2 * A_KV + 3 * 3 * B_G

LANES = 128
VMEM_LIMIT = 56 * 1024 * 1024

BF16 = jnp.bfloat16
F32 = jnp.float32

_C_QA = 0
_C_KVA = _C_QA + A_Q
_C_QB = (_C_KVA + 2 * A_KV, _C_KVA + 2 * A_KV + 3 * B_G, _C_KVA + 2 * A_KV + 6 * B_G)
_C_KVB = tuple(c + B_G for c in _C_QB)


def _const_spec(shape):
    nd = len(shape)
    return pl.BlockSpec(shape, lambda *_: (0,) * nd, pipeline_mode=pl.Buffered(1))


def _nt_dot(a, b):
    return lax.dot_general(a, b, (((1,), (1,)), ((), ())), preferred_element_type=F32)


def _dot(a, b):
    return jnp.dot(a, b, preferred_element_type=F32)


HALF = HEAD_DIM // 2


def _rope_split(y, cos4, sin4):
    x1, x2 = y[:, :LANES], y[:, LANES:]
    return jnp.concatenate([x1 * cos4 - x2 * sin4, x2 * cos4 + x1 * sin4], axis=1)


def _rope_rot64(y, cos4, sin_signed):
    outs = []
    for j in range(y.shape[1] // LANES):
        yj = y[:, j * LANES:(j + 1) * LANES]
        outs.append(yj * cos4 + pltpu.roll(yj, LANES // 2, 1) * sin_signed)
    return outs[0] if len(outs) == 1 else jnp.concatenate(outs, axis=1)


def _k_feature_rows(n_heads):
    return [(HEAD_DIM * h + HALF * part, n_heads * HALF * part + HALF * h)
            for part in range(2) for h in range(n_heads)]


def _qkv_kernel(x_ref, w_ref, cos_ref, sin_ref, *refs, tm, prompt, tiles_per_seq):
    out_refs, (xb_ref, y_ref, proj0_ref, proj1_ref) = refs[:-4], refs[-4:]
    q_refs, cache_refs, kv_refs = out_refs[0::3], out_refs[1::3], out_refs[2::3]
    groups = ((A_WINDOW, 1),) + B_DIL

    col_q = (_C_QA,) + _C_QB
    col_kv = (_C_KVA,) + _C_KVB
    width_q = (A_Q,) + (B_G,) * len(B_DIL)
    heads_kv = (A_KV_HEADS,) + (B_SLOTS,) * len(B_DIL)

    def project(proj_ref, c0, width):
        proj_ref[:, c0:c0 + width] = _dot(xb_ref[...], w_ref[:, c0:c0 + width])

    def put_cache_rows(cache_ref, lead, kv_t, n_heads):
        wk = n_heads * HEAD_DIM
        for ref_row, our_row in _k_feature_rows(n_heads):
            cache_ref[lead, ref_row:ref_row + HALF, :] = kv_t[our_row:our_row + HALF]
        cache_ref[lead, wk:2 * wk, :] = kv_t[wk:2 * wk]

    def to_planes(val):
        n_col = val.shape[1] // LANES
        for c in range(n_col):
            y_ref[c] = val[:, c * LANES:(c + 1) * LANES]
        return n_col

    def rope(y, n_heads):
        cos = cos_ref[...]
        sin = sin_ref[...]
        if n_heads == B_SLOTS:
            return _rope_split(y, cos, sin)
        lane = lax.broadcasted_iota(jnp.int32, (tm, LANES), 1)
        return _rope_rot64(y, cos, jnp.where(lane < LANES // 2, -sin, sin))

    def roped_q(proj_ref, g):
        return rope(proj_ref[:, col_q[g]:col_q[g] + width_q[g]], heads_kv[g]) * SCALE

    def roped_kv(proj_ref, g):
        wk = heads_kv[g] * HEAD_DIM
        return jnp.concatenate([rope(proj_ref[:, col_kv[g]:col_kv[g] + wk], heads_kv[g]),
                                proj_ref[:, col_kv[g] + wk:col_kv[g] + 2 * wk]], axis=1)

    def deinterleave(val, out_ref, dil):
        n_col = to_planes(val)
        for r in range(dil):
            out_ref[0, r] = jnp.concatenate(
                [y_ref[c, pl.ds(r, tm // dil, stride=dil), :] for c in range(n_col)], axis=1).astype(BF16)

    if not prompt:
        xb_ref[...] = x_ref[...].astype(BF16)
        project(proj0_ref, 0, N_QKV)
        n_seq = tm // DEC_SEQ
        for g in range(len(groups)):
            q_refs[g][...] = roped_q(proj0_ref, g).astype(BF16)
            kv = roped_kv(proj0_ref, g)
            kv_refs[g][...] = kv
            n_col = to_planes(kv)
            for t in range(DEC_SEQ):
                kv_t = jnp.concatenate(
                    [y_ref[c, pl.ds(t, n_seq, stride=DEC_SEQ), :].T for c in range(n_col)], axis=0)
                put_cache_rows(cache_refs[g], t, kv_t, heads_kv[g])
        return

    step = pl.program_id(0)
    tile = jnp.maximum(step - 1, 0)
    last_tile = (tile % tiles_per_seq) == tiles_per_seq - 1

    @pl.when(step == 0)
    def _():
        proj1_ref[...] = jnp.zeros_like(proj1_ref)

    def step_body(mine, other):
        xb_ref[...] = x_ref[...].astype(BF16)
        for g, (win, dil) in enumerate(groups):
            project(mine, col_q[g], width_q[g])
            q = roped_q(other, g)
            if dil == 1:
                q_refs[g][...] = q.astype(BF16)
            else:
                deinterleave(q, q_refs[g], dil)
            project(mine, col_kv[g], 2 * heads_kv[g] * HEAD_DIM)
            kv = roped_kv(other, g)
            if dil == 1:
                kv_refs[g][...] = kv.astype(BF16)
            else:
                deinterleave(kv, kv_refs[g], dil)
            if min(win, SEQ) == SEQ:
                put_cache_rows(cache_refs[g], 0, kv.T, heads_kv[g])

        @pl.when(last_tile)
        def _():
            for g, (win, _) in enumerate(groups):
                keep = min(win, SEQ)
                if keep < SEQ:
                    assert keep <= tm
                    put_cache_rows(cache_refs[g], 0, roped_kv(other, g)[tm - keep:, :].T, heads_kv[g])

    for par, (mine, other) in enumerate(((proj0_ref, proj1_ref), (proj1_ref, proj0_ref))):
        @pl.when(step % 2 == par)
        def _(mine=mine, other=other):
            step_body(mine, other)


def _qkv_call(x2d, w_qkv, cos_t, sin_t, *, tm, prompt, tiles_per_seq, name):
    m = x2d.shape[0]
    n_tiles = m // tm
    if prompt:
        tile = lambda i: jnp.maximum(i - 1, 0)
        x_spec = pl.BlockSpec((tm, D_MODEL), lambda i: (jnp.minimum(i, n_tiles - 1), 0))
    else:
        assert n_tiles == 1
        tile = lambda i: i
        x_spec = pl.BlockSpec((tm, D_MODEL), lambda i: (i, 0))
    row = lambda w: pl.BlockSpec((tm, w), lambda i: (tile(i), 0))
    tab = pl.BlockSpec((tm, LANES), lambda i: (tile(i) % tiles_per_seq, 0))
    in_specs = [x_spec, _const_spec((D_MODEL, N_QKV)), tab, tab]
    sds = jax.ShapeDtypeStruct
    out_shape, out_specs = [], []
    for g, (win, dil) in enumerate(((A_WINDOW, 1),) + B_DIL):
        wq = A_Q if g == 0 else B_G
        wkv = 2 * (A_KV if g == 0 else B_G)
        if not prompt:
            out_shape += [sds((m, wq), BF16), sds((DEC_SEQ, wkv, m // DEC_SEQ), F32), sds((m, wkv), F32)]
            out_specs += [row(wq), pl.BlockSpec((DEC_SEQ, wkv, m // DEC_SEQ), lambda i: (0, 0, 0)),
                          row(wkv)]
            continue
        n_seq = m // SEQ
        keep = min(win, SEQ)
        if keep == SEQ:
            cache_spec = pl.BlockSpec(
                (1, wkv, tm), lambda i: (tile(i) // tiles_per_seq, 0, tile(i) % tiles_per_seq))
        else:
            cache_spec = pl.BlockSpec((1, wkv, keep), lambda i: (tile(i) // tiles_per_seq, 0, 0))
        if dil == 1:
            out_shape += [sds((m, wq), BF16), sds((n_seq, wkv, keep), F32), sds((m, wkv), BF16)]
            out_specs += [row(wq), cache_spec, row(wkv)]
        else:
            dspec = lambda w, dil=dil: pl.BlockSpec(
                (1, dil, tm // dil, w),
                lambda i: (tile(i) // tiles_per_seq, 0, tile(i) % tiles_per_seq, 0))
            out_shape += [sds((n_seq, dil, SEQ // dil, wq), BF16), sds((n_seq, wkv, keep), F32),
                          sds((n_seq, dil, SEQ // dil, wkv), BF16)]
            out_specs += [dspec(wq), cache_spec, dspec(wkv)]
    args = [x2d, w_qkv, cos_t, sin_t]
    proj_bufs = 2 if prompt else 1
    return pl.pallas_call(
        functools.partial(_qkv_kernel, tm=tm, prompt=prompt, tiles_per_seq=tiles_per_seq),
        grid=(n_tiles + 1,) if prompt else (n_tiles,),
        in_specs=in_specs,
        out_specs=out_specs,
        out_shape=out_shape,
        scratch_shapes=[pltpu.VMEM((tm, D_MODEL), BF16),
                        pltpu.VMEM((2 * B_G // LANES, tm, LANES), F32)]
        + [pltpu.VMEM((tm, N_QKV) if k < proj_bufs else (8, LANES), F32) for k in range(2)],
        compiler_params=pltpu.CompilerParams(
            dimension_semantics=("arbitrary",), vmem_limit_bytes=VMEM_LIMIT),
        name=name,
    )(*args)


def _a_query_masks(rows):
    lane = lax.broadcasted_iota(jnp.int32, (rows, LANES), 1)
    return [jnp.where(((lane >> 5) & (A_KV_HEADS - 1)) == j, 1.0, 0.0).astype(BF16)
            for j in range(A_KV_HEADS)]


def _fold_masks():
    row = lax.broadcasted_iota(jnp.int32, (BLK, BLK), 0)
    col = lax.broadcasted_iota(jnp.int32, (BLK, BLK), 1)
    return col > row


def _attn_scratch(heads, blocks_per_trip=1):
    return [pltpu.VMEM((2, blocks_per_trip * heads * BLK, 2 * BLK), F32)]


def _attn_pipeline(n_units, bufs, *, heads, load_q, load_k, load_v, has_prev, sink_of_head, finish,
                   mxu_row_sum=False):
    (sbuf,) = bufs
    upper = _fold_masks()
    upper_bf = jnp.where(upper, 1.0, 0.0).astype(BF16)
    width = BLK if has_prev is None else 2 * BLK
    per_trip = sbuf.shape[1] // (heads * BLK)
    assert n_units % (2 * per_trip) == 0
    n_trips = n_units // per_trip

    sbuf[1] = jnp.zeros(sbuf.shape[1:], F32)

    def scores(trip, par):
        for j in range(per_trip):
            u = trip * per_trip + j
            sbuf[par, j * heads * BLK:(j + 1) * heads * BLK, 0:width] = _nt_dot(load_q(u), load_k(u))

    def softmax_values(trip, par):
        for j in range(per_trip):
            block_softmax_values(trip * per_trip + j, par, j * heads * BLK)

    def block_softmax_values(u, par, row0):
        prev_ok = None if has_prev is None else has_prev(u)
        ps, ms, ls = [], [], []
        for h in range(heads):
            rows = slice(row0 + h * BLK, row0 + (h + 1) * BLK)
            if has_prev is None:
                sf = jnp.where(upper, NEG, sbuf[par, rows, 0:BLK])
            else:
                sf = jnp.where(upper, jnp.where(prev_ok, sbuf[par, rows, 0:BLK], NEG),
                               sbuf[par, rows, BLK:2 * BLK])
            mx = jnp.max(sf, axis=-1, keepdims=True)
            if sink_of_head is not None:
                mx = jnp.maximum(mx, sink_of_head(h))
            e = jnp.exp(sf - mx)
            if not mxu_row_sum:
                ls.append(jnp.broadcast_to(jnp.sum(e, axis=-1, keepdims=True), (BLK, LANES)))
            e = e.astype(BF16)
            if has_prev is None:
                ps.append(e)
            else:
                p_prev = e * upper_bf
                ps.append(jnp.concatenate([p_prev, e - p_prev], axis=1))
            ms.append(jnp.broadcast_to(mx, (BLK, LANES)))
        finish(u, _dot(jnp.concatenate(ps, axis=0), load_v(u)), jnp.concatenate(ms, axis=0),
               jnp.concatenate(ls, axis=0) if ls else None)

    def trip_pair(t, carry):
        for par in (0, 1):
            i = 2 * t + par
            scores(i, par)
            softmax_values(jnp.maximum(i - 1, 0), 1 - par)
        return carry

    lax.fori_loop(0, n_trips // 2, trip_pair, 0)
    lax.fori_loop(n_trips - 1, n_trips, lambda i, carry: (softmax_values(i, 1), carry)[1], 0)


def _attn_a_kernel(sink_ref, q_ref, kv_ref, o_ref, *bufs):
    n_blk = q_ref.shape[0] // BLK
    lane = lax.broadcasted_iota(jnp.int32, (BLK, LANES), 1)
    hi = lane >= HEAD_DIM
    mask_bf = _a_query_masks(BLK)

    def rows_of(b):
        return pl.ds(pl.multiple_of(b * BLK, BLK), BLK)

    def prev_cur(b, cols):
        return jnp.concatenate([kv_ref[rows_of(jnp.maximum(b - 1, 0)), cols], kv_ref[rows_of(b), cols]],
                               axis=0)

    def load_q(b):
        qblk = q_ref[rows_of(b), :]
        return jnp.concatenate([qblk[:, g * LANES:(g + 1) * LANES] * mask_bf[j]
                                for j in range(A_KV_HEADS) for g in range(A_GROUP)], axis=0)

    ones = jnp.ones((2 * BLK, LANES), BF16)

    def load_v(b):
        return jnp.concatenate([prev_cur(b, slice(A_KV, 2 * A_KV)), ones], axis=1)

    def finish(b, o, m, _):
        def normalised(h):
            rows = slice(h * BLK, (h + 1) * BLK)
            den = o[rows, LANES:] + jnp.exp(sink_ref[h] - m[rows])
            return o[rows, :LANES] / den

        for g in range(A_GROUP):
            o_ref[rows_of(b), g * LANES:(g + 1) * LANES] = jnp.where(
                hi, normalised(A_GROUP + g), normalised(g)).astype(BF16)

    _attn_pipeline(n_blk, bufs, heads=A_Q_HEADS, load_q=load_q,
                   load_k=lambda b: prev_cur(b, slice(0, A_KV)), load_v=load_v,
                   has_prev=lambda b: b > 0, sink_of_head=lambda h: sink_ref[h], finish=finish,
                   mxu_row_sum=True)


def _attn_a_call(sink, q_a, kva_bf):
    n_seq = q_a.shape[0] // SEQ
    return pl.pallas_call(
        _attn_a_kernel,
        grid=(n_seq,),
        in_specs=[pl.BlockSpec(memory_space=pltpu.SMEM),
                  pl.BlockSpec((SEQ, A_Q), lambda n: (n, 0)),
                  pl.BlockSpec((SEQ, 2 * A_KV), lambda n: (n, 0))],
        out_specs=pl.BlockSpec((SEQ, A_Q), lambda n: (n, 0)),
        out_shape=jax.ShapeDtypeStruct(q_a.shape, BF16),
        scratch_shapes=_attn_scratch(A_Q_HEADS),
        compiler_params=pltpu.CompilerParams(
            dimension_semantics=("arbitrary",), vmem_limit_bytes=VMEM_LIMIT),
        name="attn_a_prompt",
    )(sink, q_a, kva_bf)


def _slot_masks(rows):
    lane = lax.broadcasted_iota(jnp.int32, (rows, B_G), 1)
    masks = [(lane >> 6) == s for s in range(B_SLOTS)]
    masks_bf = [jnp.where(((lane >> 5) & (B_SLOTS - 1)) == s, 1.0, 0.0).astype(BF16) for s in range(B_SLOTS)]
    return masks, masks_bf


def _attn_b_kernel(q1_ref, kv1_ref, q2_ref, kv2_ref, q3_ref, kv3_ref, o_ref,
                   acc2_ref, m2_ref, l2_ref, acc3_ref, m3_ref, l3_ref, *bufs):
    masks, masks_bf = _slot_masks(BLK)
    d2, d3 = B_DIL[1][1], B_DIL[2][1]
    nb2 = SEQ // d2 // BLK
    assert SEQ // d3 == BLK
    n_plane = B_G // LANES

    def put(ref, rows, val):
        for c in range(n_plane):
            ref[c, rows, :] = val[:, c * LANES:(c + 1) * LANES]

    def get(ref, rows):
        return jnp.concatenate([ref[c, rows, :] for c in range(n_plane)], axis=1)

    def stack_q(qblk):
        return jnp.concatenate([qblk * masks_bf[s] for s in range(B_SLOTS)], axis=0)

    def unstack(o, m, l):
        acc = jnp.where(masks[0], o[0:BLK], 0.0)
        rep = lambda x, s: jnp.concatenate([x[s * BLK:(s + 1) * BLK]] * n_plane, axis=1)
        mf, lf = rep(m, 0), rep(l, 0)
        for s in range(1, B_SLOTS):
            acc = jnp.where(masks[s], o[s * BLK:(s + 1) * BLK], acc)
            mf = jnp.where(masks[s], rep(m, s), mf)
            lf = jnp.where(masks[s], rep(l, s), lf)
        return acc, mf, lf

    def rows_of(b):
        return pl.ds(pl.multiple_of(b * BLK, BLK), BLK)

    run = functools.partial(_attn_pipeline, bufs=bufs, heads=B_SLOTS, sink_of_head=None)

    def finish3(r, o, m, l):
        rows = pl.ds(r, BLK, stride=d3)
        for ref, val in zip((acc3_ref, m3_ref, l3_ref), unstack(o, m, l)):
            put(ref, rows, val)

    run(d3, load_q=lambda r: stack_q(q3_ref[r]), load_k=lambda r: kv3_ref[r, :, 0:B_G],
        load_v=lambda r: kv3_ref[r, :, B_G:2 * B_G], has_prev=None, finish=finish3)

    def prev_cur2(u, cols):
        r, b = u // nb2, u % nb2
        return jnp.concatenate([kv2_ref[r, rows_of(jnp.maximum(b - 1, 0)), cols], kv2_ref[r, rows_of(b), cols]],
                               axis=0)

    def finish2(u, o, m, l):
        r, b = u // nb2, u % nb2
        rows = pl.ds(r + b * (BLK * d2), BLK, stride=d2)
        for ref, val in zip((acc2_ref, m2_ref, l2_ref), unstack(o, m, l)):
            put(ref, rows, val)

    run(d2 * nb2, load_q=lambda u: stack_q(q2_ref[u // nb2, rows_of(u % nb2), :]),
        load_k=lambda u: prev_cur2(u, slice(0, B_G)), load_v=lambda u: prev_cur2(u, slice(B_G, 2 * B_G)),
        has_prev=lambda u: (u % nb2) > 0, finish=finish2)

    def prev_cur1(b, cols):
        return jnp.concatenate([kv1_ref[rows_of(jnp.maximum(b - 1, 0)), cols], kv1_ref[rows_of(b), cols]],
                               axis=0)

    def finish1(b, o, m, l):
        acc1, m1, l1 = unstack(o, m, l)
        rows = rows_of(b)
        m2, m3 = get(m2_ref, rows), get(m3_ref, rows)
        mx = jnp.maximum(jnp.maximum(m1, m2), m3)
        w1, w2, w3 = jnp.exp(m1 - mx), jnp.exp(m2 - mx), jnp.exp(m3 - mx)
        num = w1 * acc1 + w2 * get(acc2_ref, rows) + w3 * get(acc3_ref, rows)
        den = w1 * l1 + w2 * get(l2_ref, rows) + w3 * get(l3_ref, rows)
        o_ref[rows, :] = (num / den).astype(BF16)

    run(SEQ // BLK, load_q=lambda b: stack_q(q1_ref[rows_of(b), :]),
        load_k=lambda b: prev_cur1(b, slice(0, B_G)), load_v=lambda b: prev_cur1(b, slice(B_G, 2 * B_G)),
        has_prev=lambda b: b > 0, finish=finish1)


def _attn_b_call(q1, kv1, q2, kv2, q3, kv3):
    n_seq = q1.shape[0] // SEQ
    d2, d3 = B_DIL[1][1], B_DIL[2][1]
    rows = lambda w: pl.BlockSpec((SEQ, w), lambda n: (n, 0))
    dsp = lambda d, w: pl.BlockSpec((None, d, SEQ // d, w), lambda n: (n, 0, 0, 0))
    return pl.pallas_call(
        _attn_b_kernel,
        grid=(n_seq,),
        in_specs=[rows(B_G), rows(2 * B_G), dsp(d2, B_G), dsp(d2, 2 * B_G), dsp(d3, B_G),
                  dsp(d3, 2 * B_G)],
        out_specs=rows(B_G),
        out_shape=jax.ShapeDtypeStruct((q1.shape[0], B_G), BF16),
        scratch_shapes=[pltpu.VMEM((B_G // LANES, SEQ, LANES), F32) for _ in range(6)]
        + _attn_scratch(B_SLOTS, blocks_per_trip=2),
        compiler_params=pltpu.CompilerParams(
            dimension_semantics=("arbitrary",), vmem_limit_bytes=VMEM_LIMIT),
        name="attn_b_prompt",
    )(q1, kv1, q2, kv2, q3, kv3)


def _sample_attn_block(sink_ref, qa_ref, kna_ref, ca_ref, q1_ref, kn1_ref, c1_ref,
                       q2_ref, kn2_ref, c2_ref, q3_ref, kn3_ref, c3_ref, *, tt, seqs, emit):
    pad = BLK - tt
    tt_shift, seq_shift = tt.bit_length() - 1, DEC_SEQ.bit_length() - 1
    assert tt == 1 << tt_shift and DEC_SEQ == 1 << seq_shift

    def pad_rows(x):
        return jnp.concatenate([x.astype(BF16), jnp.zeros((pad, x.shape[1]), BF16)], axis=0)

    def window_group(qm, kn_ref, c_ref, dil, sinkv, finish):
        rows, kd = qm.shape
        lc = c_ref.shape[2]
        kn = kn_ref[...]
        knpad = pad_rows(kn[:, :kd])
        vnpad = pad_rows(kn[:, kd:])
        s_new = _nt_dot(qm, knpad)
        def masks(width):
            col = lax.broadcasted_iota(jnp.int32, (rows, width), 1)
            t = lax.broadcasted_iota(jnp.int32, (rows, width), 0) & (DEC_SEQ - 1)
            same_res = (col & (dil - 1)) == (t & (dil - 1))
            return (col > t) & same_res, (col <= t) & same_res

        valid, _ = masks(lc)
        _, is_new = masks(BLK)
        k_rows = sorted(_k_feature_rows(kd // HEAD_DIM), key=lambda rows: rows[1])
        scores = []
        for pos, _ in seqs:
            k_t = jnp.concatenate([c_ref[pos, ref_row:ref_row + HALF, :] for ref_row, _ in k_rows],
                                  axis=0).astype(BF16)
            scores.append(_dot(qm, k_t))
        yield
        probs = []
        for score, (_, n) in zip(scores, seqs):
            s_c = jnp.where(valid, score, NEG)
            s_n = pltpu.roll(s_new, BLK - DEC_SEQ * n, 1) if n else s_new
            first = jnp.where(is_new, s_n, s_c[:, :BLK])
            s = first if lc == BLK else jnp.concatenate([first, s_c[:, BLK:]], axis=1)
            mx = jnp.max(s, axis=-1, keepdims=True)
            if sinkv is not None:
                mx = jnp.maximum(mx, sinkv)
            e = jnp.exp(s - mx)
            l = jnp.sum(e, axis=-1, keepdims=True)
            if sinkv is not None:
                l = l + jnp.exp(sinkv - mx)
            p_new = jnp.where(is_new, e[:, :BLK], 0.0)
            if n:
                p_new = pltpu.roll(p_new, DEC_SEQ * n, 1)
            probs.append((jnp.where(valid, e, 0.0).astype(BF16), p_new.astype(BF16), mx, l))
        yield
        for (p_c, p_new, mx, l), (pos, n) in zip(probs, seqs):
            v_t = c_ref[pos, kd:2 * kd, :].astype(BF16)
            finish(n, _nt_dot(p_c, v_t) + _dot(p_new, vnpad), mx, l)

    rows_a = A_Q_HEADS * tt
    lane_t = lax.broadcasted_iota(jnp.int32, (tt, LANES), 1)
    hi_t = lane_t >= HEAD_DIM
    mask_bf = _a_query_masks(tt)
    qa = qa_ref[...]
    qm = jnp.concatenate([qa[:, g * LANES:(g + 1) * LANES] * mask_bf[j]
                          for j in range(A_KV_HEADS) for g in range(A_GROUP)], axis=0)
    head = lax.broadcasted_iota(jnp.int32, (rows_a, 1), 0) >> tt_shift
    sinkv = jnp.zeros((rows_a, 1), F32)
    for h in range(A_Q_HEADS):
        sinkv = jnp.where(head == h, sink_ref[h], sinkv)
    seq_t = lax.broadcasted_iota(jnp.int32, (tt, LANES), 0) >> seq_shift
    out_a = [jnp.zeros((tt, LANES), F32) for _ in range(A_GROUP)]

    def finish_a(n, o, mx, l):
        o = o / l
        for g in range(A_GROUP):
            og = jnp.where(hi_t, o[(A_GROUP + g) * tt:(A_GROUP + g + 1) * tt], o[g * tt:(g + 1) * tt])
            out_a[g] = jnp.where(seq_t == n, og, out_a[g])

    stages = [window_group(qm, kna_ref, ca_ref, 1, sinkv, finish_a)]

    masks_t, masks_bf_t = _slot_masks(tt)
    tok_t = lax.broadcasted_iota(jnp.int32, (tt, B_G), 0)

    def stack_q(q):
        return jnp.concatenate([q * masks_bf_t[s] for s in range(B_SLOTS)], axis=0)

    def unstack(o, mx, l):
        acc = jnp.where(masks_t[0], o[0:tt], 0.0)
        mf = jnp.broadcast_to(mx[0:tt], (tt, B_G))
        lf = jnp.broadcast_to(l[0:tt], (tt, B_G))
        for s in range(1, B_SLOTS):
            acc = jnp.where(masks_t[s], o[s * tt:(s + 1) * tt], acc)
            mf = jnp.where(masks_t[s], mx[s * tt:(s + 1) * tt], mf)
            lf = jnp.where(masks_t[s], l[s * tt:(s + 1) * tt], lf)
        return acc, mf, lf

    zero, one = jnp.zeros((tt, B_G), F32), jnp.ones((tt, B_G), F32)
    stats = []
    for (win, dil), q_ref, kn_ref, c_ref in zip(B_DIL, (q1_ref, q2_ref, q3_ref),
                                                (kn1_ref, kn2_ref, kn3_ref), (c1_ref, c2_ref, c3_ref)):
        assert c_ref.shape[2] == win
        group = [zero, zero, one]

        def finish_b(n, o, mx, l, group=group):
            sel = (tok_t >> seq_shift) == n
            for k, new in enumerate(unstack(o, mx, l)):
                group[k] = jnp.where(sel, new, group[k])

        stages.append(window_group(stack_q(q_ref[...]), kn_ref, c_ref, dil, None, finish_b))
        stats.append(group)

    for phase in range(3):
        for stage in stages:
            next(stage, None)
        if phase < 2:
            yield
    (acc1, m1, l1), (acc2, m2, l2), (acc3, m3, l3) = stats
    mx = jnp.maximum(jnp.maximum(m1, m2), m3)
    w1, w2, w3 = jnp.exp(m1 - mx), jnp.exp(m2 - mx), jnp.exp(m3 - mx)
    emit(out_a, (w1 * acc1 + w2 * acc2 + w3 * acc3) / (w1 * l1 + w2 * l2 + w3 * l3))


def _sample_attn_rider(in_refs, oa_ref, ob_ref, k, n_seqs):
    seq_shift = DEC_SEQ.bit_length() - 1

    def merge(ref, cols, new):
        if k:
            mine = (lax.broadcasted_iota(jnp.int32, new.shape, 0) >> seq_shift) == k
            new = jnp.where(mine, new, ref[:, cols].astype(F32))
        ref[:, cols] = new.astype(BF16)

    def emit(out_a, out_b):
        for c in range(A_GROUP):
            merge(oa_ref, slice(c * LANES, (c + 1) * LANES), out_a[c])
        merge(ob_ref, slice(0, B_G), out_b)

    yield from _sample_attn_block(*in_refs, tt=n_seqs * DEC_SEQ, seqs=[(0, k)], emit=emit)


def _layernorm(x, g, b):
    mu = jnp.mean(x, axis=-1, keepdims=True)
    xc = x - mu
    var = jnp.mean(xc * xc, axis=-1, keepdims=True)
    return xc * lax.rsqrt(var + LN_EPS) * g + b


def _gelu_exact(x):
    return 0.5 * x * (1.0 + lax.erf(x * (0.5 ** 0.5)))


_FF_CHUNK = 256
_CARRY = 8


def _ffn_kernel(*refs, tm, sample, tiles_per_seq):
    if sample:
        (x_ref, oa_ref, ob_ref, fill_ref, wg_ref, wa_ref, wb_ref, wo_ref, g1_ref, b1_ref,
         wup_ref, cw_ref, cb_ref, wdn_ref, g2_ref, b2_ref, y_ref, u_ref,
         xb_ref, m_ref, gg_ref, ext_ref, h0_ref, hb0_ref) = refs
    else:
        x_ref, oa_ref, ob_ref = refs[:3]
        rider_in, refs = refs[3:3 + _N_SAMPLE_ATTN_IN], refs[3 + _N_SAMPLE_ATTN_IN:]
        (wg_ref, wa_ref, wb_ref, wo_ref, g1_ref, b1_ref,
         wup_ref, cw_ref, cb_ref, wdn_ref, g2_ref, b2_ref, y_ref, ulast_ref, oa_s_ref, ob_s_ref,
         xb_ref, m_ref, gg_ref, ext_ref, h0_ref, hb0_ref, carry_ref) = refs
    half = D_MODEL // 2

    def merge_pieces(h_ref, hb_ref):
        def gate_half(c):
            if c == 0:
                xb_ref[...] = x_ref[...].astype(BF16)
            cs = slice(c * half, (c + 1) * half)
            ga = _dot(xb_ref[...], wg_ref[:, c * half:(c + 1) * half])
            gb = _dot(xb_ref[...], wg_ref[:, D_MODEL + c * half:D_MODEL + (c + 1) * half])
            ta = _dot(oa_ref[...], wa_ref[:, cs])
            tb = _dot(ob_ref[...], wb_ref[:, cs])
            m_ref[:, cs] = (jax.nn.sigmoid(ga) * ta + jax.nn.sigmoid(gb) * tb).astype(BF16)

        def out_proj():
            mix = _dot(m_ref[...], wo_ref[...])
            h = _layernorm(ALPHA * x_ref[...] + mix, g1_ref[...], b1_ref[...])
            h_ref[...] = h
            hb_ref[...] = h.astype(BF16)

        return [functools.partial(gate_half, 0), functools.partial(gate_half, 1), out_proj]

    def ffn_pieces(h_ref, hb_ref):
        def chunk(c):
            cs = slice(c * _FF_CHUNK, (c + 1) * _FF_CHUNK)
            u = _dot(hb_ref[...], wup_ref[:, c * _FF_CHUNK:(c + 1) * _FF_CHUNK])
            v = _dot(hb_ref[...], wup_ref[:, D_FF + c * _FF_CHUNK:D_FF + (c + 1) * _FF_CHUNK])
            if sample:
                ext_ref[0:_CARRY, :] = jnp.zeros((_CARRY, _FF_CHUNK), F32)
            else:
                ext_ref[0:_CARRY, :] = carry_ref[:, cs]
            ext_ref[_CARRY:_CARRY + tm, :] = u
            u1 = ext_ref[_CARRY - 1:_CARRY - 1 + tm, :]
            u2 = ext_ref[_CARRY - 2:_CARRY - 2 + tm, :]
            if sample:
                t = lax.broadcasted_iota(jnp.int32, (tm, _FF_CHUNK), 0) & (DEC_SEQ - 1)
                fill = fill_ref[:, cs]
                u1 = jnp.where(t >= 1, u1, pltpu.roll(fill, tm - 1, 0))
                u2 = jnp.where(t >= 2, u2, fill)
                u_ref[:, cs] = u
            else:
                tail = u[tm - _CARRY:tm, :]
                carry_ref[:, cs] = tail
                ulast_ref[0, :, cs] = tail
            a = cb_ref[:, cs] + cw_ref[0:1, cs] * u2 + cw_ref[1:2, cs] * u1 + cw_ref[2:3, cs] * u
            gg_ref[:, cs] = (_gelu_exact(a) * v).astype(BF16)

        def down():
            f = _dot(gg_ref[...], wdn_ref[...])
            y_ref[...] = _layernorm(ALPHA * h_ref[...] + f, g2_ref[...], b2_ref[...])

        return [functools.partial(chunk, c) for c in range(D_FF // _FF_CHUNK)] + [down]

    pieces = merge_pieces(h0_ref, hb0_ref) + ffn_pieces(h0_ref, hb0_ref)
    if sample:
        for piece in pieces:
            piece()
        return

    step, sub = pl.program_id(0), pl.program_id(1)

    @pl.when((step % tiles_per_seq == 0) & (sub == 0))
    def _():
        carry_ref[...] = jnp.zeros_like(carry_ref)

    n_chunk = D_FF // _FF_CHUNK
    phases = (pieces[:3], pieces[3:3 + n_chunk // 2], pieces[3 + n_chunk // 2:3 + n_chunk], pieces[3 + n_chunk:])
    assert len(phases) == _FFN_PHASES
    for g, phase in enumerate(phases):
        @pl.when(sub == g)
        def _(g=g, phase=phase):
            rider = _sample_attn_rider(rider_in, oa_s_ref, ob_s_ref, g, _FFN_PHASES)
            for piece in phase:
                next(rider, None)
                piece()
            for _ in rider:
                pass


_FFN_PHASES = 4
_N_SAMPLE_ATTN_IN = 13


def _ffn_call(x2d, oa, ob, weights, *, tm, sample, tiles_per_seq, conv_fill=None, sample_attn_args=(), name):
    m = x2d.shape[0]
    n_tiles = m // tm
    sds = jax.ShapeDtypeStruct
    row_in = row_out = lambda w: pl.BlockSpec((tm, w), lambda i, *_: (i, 0))
    in_specs = [row_in(D_MODEL), row_in(A_Q), row_in(B_G)]
    args = [x2d, oa, ob]
    if sample:
        in_specs += [row_in(D_FF)]
        args += [conv_fill]
    else:
        assert len(sample_attn_args) == _N_SAMPLE_ATTN_IN
        tt = _FFN_PHASES * DEC_SEQ
        n_tok = sample_attn_args[1].shape[0]
        assert n_tok == n_tiles * tt
        in_specs.append(pl.BlockSpec(memory_space=pltpu.SMEM))
        for k, a in enumerate(sample_attn_args[1:]):
            if k % 3 == 2:
                in_specs.append(pl.BlockSpec((1,) + a.shape[1:], lambda i, g: (_FFN_PHASES * i + g, 0, 0)))
            else:
                in_specs.append(pl.BlockSpec((tt, a.shape[1]), lambda i, g: (i, 0)))
        args += list(sample_attn_args)
    in_specs += [_const_spec(w.shape) for w in weights]
    args += list(weights)
    h_bufs = [pltpu.VMEM((tm, D_MODEL), F32), pltpu.VMEM((tm, D_MODEL), BF16)]
    scratch = [pltpu.VMEM((tm, D_MODEL), BF16), pltpu.VMEM((tm, D_MODEL), BF16),
               pltpu.VMEM((tm, D_FF), BF16), pltpu.VMEM((tm + _CARRY, _FF_CHUNK), F32)] + h_bufs
    if sample:
        out_shape = [sds((m, D_MODEL), F32), sds((m, D_FF), F32)]
        out_specs = [row_out(D_MODEL), row_out(D_FF)]
    else:
        out_shape = [sds((m, D_MODEL), F32), sds((n_tiles, _CARRY, D_FF), F32),
                     sds((n_tok, A_Q), BF16), sds((n_tok, B_G), BF16)]
        out_specs = [row_out(D_MODEL), pl.BlockSpec((1, _CARRY, D_FF), lambda i, g: (i, 0, 0)),
                     pl.BlockSpec((tt, A_Q), lambda i, g: (i, 0)), pl.BlockSpec((tt, B_G), lambda i, g: (i, 0))]
        scratch += [pltpu.VMEM((_CARRY, D_FF), F32)]
    grid = (n_tiles,) if sample else (n_tiles, _FFN_PHASES)
    return pl.pallas_call(
        functools.partial(_ffn_kernel, tm=tm, sample=sample, tiles_per_seq=tiles_per_seq),
        grid=grid,
        in_specs=in_specs,
        out_specs=out_specs,
        out_shape=out_shape,
        scratch_shapes=scratch,
        compiler_params=pltpu.CompilerParams(
            dimension_semantics=("arbitrary",) * len(grid), vmem_limit_bytes=VMEM_LIMIT),
        name=name,
    )(*args)


def _rope_tables(pos):
    half = HEAD_DIM // 2
    inv = ROPE_THETA ** (-jnp.arange(half, dtype=F32) / half)
    ang = pos.astype(F32)[:, None] * inv[None, :]
    cos, sin = jnp.cos(ang), jnp.sin(ang)
    reps = LANES // HALF
    return jnp.tile(cos, (1, reps)), jnp.tile(sin, (1, reps))


def _rotary_order(w, heads):
    x1 = [w[:, h * HEAD_DIM:h * HEAD_DIM + HALF] for h in heads]
    x2 = [w[:, h * HEAD_DIM + HALF:(h + 1) * HEAD_DIM] for h in heads]
    return x1 + x2


def _qkv_source_columns():
    w_in = np.arange(N_QKV, dtype=np.int32)[None, :]
    qa = w_in[:, 0:A_Q]
    ka = w_in[:, A_Q:A_Q + A_KV]
    va = w_in[:, A_Q + A_KV:A_Q + 2 * A_KV]
    b0 = A_Q + 2 * A_KV
    qb = w_in[:, b0:b0 + 3 * B_G]
    kb = w_in[:, b0 + 3 * B_G:b0 + 6 * B_G]
    vb = w_in[:, b0 + 6 * B_G:b0 + 9 * B_G]
    cols = []
    for c in range(A_GROUP):
        cols += _rotary_order(qa, (c, A_GROUP + c))
    cols += _rotary_order(ka, range(A_KV_HEADS)) + [va]
    for g in range(len(B_DIL)):
        gs = slice(g * B_G, (g + 1) * B_G)
        cols += _rotary_order(qb[:, gs], range(B_SLOTS)) + _rotary_order(kb[:, gs], range(B_SLOTS)) + [vb[:, gs]]
    return np.concatenate(cols, axis=1)[0]


def _in_weight_kernel(src_ref, w_ref, qkv_ref, gate_ref):
    src = _qkv_source_columns()
    rows = w_ref.shape[0]
    lane = lax.broadcasted_iota(jnp.int32, (LANES, LANES), 0)
    for j in range(N_QKV // LANES):
        want = src[j * LANES:(j + 1) * LANES]
        if np.array_equal(want, want[0] + np.arange(LANES)) and want[0] % LANES == 0:
            qkv_ref[:, j * LANES:(j + 1) * LANES] = w_ref[:, want[0]:want[0] + LANES].astype(BF16)
            continue
        acc = jnp.zeros((rows, LANES), F32)
        for b in sorted(set(int(c) // LANES for c in want)):
            pick = (lane + b * LANES) == src_ref[:, j * LANES:(j + 1) * LANES]
            acc = acc + _dot(w_ref[:, b * LANES:(b + 1) * LANES].astype(BF16), jnp.where(pick, 1.0, 0.0).astype(BF16))
        qkv_ref[:, j * LANES:(j + 1) * LANES] = acc.astype(BF16)
    gate_ref[...] = w_ref[:, N_QKV:].astype(BF16)


def _prep_in_weights(w_in):
    rows = 256
    n_proj = w_in.shape[1]
    src = jnp.asarray(_qkv_source_columns())[None, :]
    return pl.pallas_call(
        _in_weight_kernel,
        grid=(D_MODEL // rows,),
        in_specs=[pl.BlockSpec((1, N_QKV), lambda i: (0, 0)), pl.BlockSpec((rows, n_proj), lambda i: (i, 0))],
        out_specs=[pl.BlockSpec((rows, N_QKV), lambda i: (i, 0)),
                   pl.BlockSpec((rows, n_proj - N_QKV), lambda i: (i, 0))],
        out_shape=[jax.ShapeDtypeStruct((D_MODEL, N_QKV), BF16),
                   jax.ShapeDtypeStruct((D_MODEL, n_proj - N_QKV), BF16)],
        compiler_params=pltpu.CompilerParams(dimension_semantics=("arbitrary",), vmem_limit_bytes=VMEM_LIMIT),
        name="in_weights",
    )(src, w_in)


def _prep_br_a(w_br_a):
    rows = []
    for c in range(A_GROUP):
        rows += [w_br_a[c * HEAD_DIM:(c + 1) * HEAD_DIM],
                 w_br_a[(A_GROUP + c) * HEAD_DIM:(A_GROUP + c + 1) * HEAD_DIM]]
    return jnp.concatenate(rows, axis=0).astype(BF16)


_TM_PROMPT = 512


def kernel(x_prompt, x_sample, cache_a, cache_b1, cache_b2, cache_b3, state_conv, w_in, sink_a, w_br_a, w_br_b, w_o, ln1_g, ln1_b, w_up, conv_w, conv_b, w_down, ln2_g, ln2_b):
    assert DEPTH == 1
    l = 0
    w_qkv, w_gate = _prep_in_weights(w_in[l])
    ffn_weights = (w_gate, _prep_br_a(w_br_a[l]), w_br_b[l].astype(BF16), w_o[l].astype(BF16),
                   ln1_g[l][None], ln1_b[l][None], w_up[l].astype(BF16), conv_w[l], conv_b[l][None],
                   w_down[l].astype(BF16), ln2_g[l][None], ln2_b[l][None])
    sink = sink_a[l].astype(F32)

    ms = DEC_BATCH * DEC_SEQ
    xs = x_sample.reshape(ms, D_MODEL)
    cos_s, sin_s = _rope_tables(PAST_LEN + (jnp.arange(ms, dtype=jnp.int32) % DEC_SEQ))
    (qa_s, ca_s, kva_s, qb1_s, cb1_s, kvb1_s, qb2_s, cb2_s, kvb2_s, qb3_s, cb3_s, kvb3_s) = _qkv_call(
        xs, w_qkv, cos_s, sin_s, tm=ms, prompt=False, tiles_per_seq=1, name="qkv_sample")

    def window_buffer(c):
        return c.transpose(0, 2, 3, 4, 1).reshape(c.shape[0], -1, c.shape[1])

    ca = window_buffer(cache_a[l])
    cb = [window_buffer(c[l]) for c in (cache_b1, cache_b2, cache_b3)]
    sample_attn_args = (sink, qa_s, kva_s, ca, qb1_s, kvb1_s, cb[0], qb2_s, kvb2_s, cb[1], qb3_s, kvb3_s, cb[2])

    mp = BATCH * SEQ
    xp = x_prompt.reshape(mp, D_MODEL)
    cos_p, sin_p = _rope_tables(jnp.arange(SEQ, dtype=jnp.int32))
    tps = SEQ // _TM_PROMPT
    (qa, ca_p, kva_bf, qb1, cb1_p, kvb1_bf, qb2, cb2_p, kvb2_bf, qb3, cb3_p, kvb3_bf) = _qkv_call(
        xp, w_qkv, cos_p, sin_p, tm=_TM_PROMPT, prompt=True, tiles_per_seq=tps, name="qkv_prompt")
    oa = _attn_a_call(sink, qa, kva_bf)
    ob = _attn_b_call(qb1, kvb1_bf, qb2, kvb2_bf, qb3, kvb3_bf)
    y_p, ulast, oa_s, ob_s = _ffn_call(xp, oa, ob, ffn_weights, tm=_TM_PROMPT, sample=False, tiles_per_seq=tps,
                                       sample_attn_args=sample_attn_args, name="ffn_prompt")

    def prompt_cache(c, heads):
        return c.reshape(BATCH, 2, heads, HEAD_DIM, c.shape[-1]).transpose(0, 4, 1, 2, 3)[None]

    y_prompt = y_p.reshape(BATCH, SEQ, D_MODEL)
    cache_a_prompt = prompt_cache(ca_p, A_KV_HEADS)
    kvb_p = [prompt_cache(c, B_SLOTS) for c in (cb1_p, cb2_p, cb3_p)]
    state_conv_prompt = ulast.reshape(BATCH, tps, _CARRY, D_FF)[None, :, tps - 1, _CARRY - (CONV_W - 1):]

    fill = jnp.pad(state_conv[l], ((0, 0), (0, DEC_SEQ - (CONV_W - 1)), (0, 0))).reshape(ms, D_FF)
    y_s, u_s = _ffn_call(xs, oa_s, ob_s, ffn_weights, tm=ms, sample=True, tiles_per_seq=1,
                         conv_fill=fill, name="ffn_sample")

    def sample_cache(c, heads):
        return c.reshape(DEC_SEQ, 2, heads, HEAD_DIM, DEC_BATCH).transpose(4, 0, 1, 2, 3)[None]

    y_sample = y_s.reshape(DEC_BATCH, DEC_SEQ, D_MODEL)
    cache_a_sample = sample_cache(ca_s, A_KV_HEADS)
    kvb_s = [sample_cache(c, B_SLOTS) for c in (cb1_s, cb2_s, cb3_s)]
    state_conv_sample = u_s.reshape(DEC_BATCH, DEC_SEQ, D_FF)[None, :, DEC_SEQ - (CONV_W - 1):]

    return (y_prompt, y_sample, cache_a_prompt, cache_a_sample, kvb_p[0], kvb_s[0], kvb_p[1], kvb_s[1],
            kvb_p[2], kvb_s[2], state_conv_prompt, state_conv_sample)
```

```python
import functools

import jax
import jax.numpy as jnp
import numpy as np
from jax import lax
from jax.experimental import pallas as pl
from jax.experimental.pallas import tpu as pltpu

D_MODEL = 1024
BATCH = 8
SEQ = 2048
DEPTH = 1
DEC_BATCH = 128
DEC_SEQ = 4
PAST_LEN = 16384
HEAD_DIM = 64
A_Q_HEADS = 8
A_KV_HEADS = 2
A_GROUP = A_Q_HEADS // A_KV_HEADS
A_WINDOW = 128
B_DIL = ((128, 1), (512, 4), (2048, 16))
B_SLOTS = 4
BLK = 128
ROPE_THETA = 10000.0
D_FF = ((8 * D_MODEL // 3 + 127) // 128) * 128
CONV_W = 3
ALPHA = (2 * DEPTH) ** 0.25
LN_EPS = 1e-5
NEG = -1e30
SCALE = HEAD_DIM ** -0.5
A_Q = A_Q_HEADS * HEAD_DIM
A_KV = A_KV_HEADS * HEAD_DIM
B_G = B_SLOTS * HEAD_DIM
N_QKV = A_Q + 2 * A_KV + 3 * 3 * B_G

LANES = 128
VMEM_LIMIT = 56 * 1024 * 1024

BF16 = jnp.bfloat16
F32 = jnp.float32

_C_QA = 0
_C_KVA = _C_QA + A_Q
_C_QB = (_C_KVA + 2 * A_KV, _C_KVA + 2 * A_KV + 3 * B_G, _C_KVA + 2 * A_KV + 6 * B_G)
_C_KVB = tuple(c + B_G for c in _C_QB)


def _const_spec(shape):
    nd = len(shape)
    return pl.BlockSpec(shape, lambda *_: (0,) * nd, pipeline_mode=pl.Buffered(1))


def _nt_dot(a, b):
    return lax.dot_general(a, b, (((1,), (1,)), ((), ())), preferred_element_type=F32)


def _dot(a, b):
    return jnp.dot(a, b, preferred_element_type=F32)


HALF = HEAD_DIM // 2


def _rope_split(y, cos4, sin4):
    x1, x2 = y[:, :LANES], y[:, LANES:]
    return jnp.concatenate([x1 * cos4 - x2 * sin4, x2 * cos4 + x1 * sin4], axis=1)


def _rope_rot64(y, cos4, sin_signed):
    outs = []
    for j in range(y.shape[1] // LANES):
        yj = y[:, j * LANES:(j + 1) * LANES]
        outs.append(yj * cos4 + pltpu.roll(yj, LANES // 2, 1) * sin_signed)
    return outs[0] if len(outs) == 1 else jnp.concatenate(outs, axis=1)


def _k_feature_rows(n_heads):
    return [(HEAD_DIM * h + HALF * part, n_heads * HALF * part + HALF * h)
            for part in range(2) for h in range(n_heads)]


def _qkv_kernel(x_ref, w_ref, cos_ref, sin_ref, *refs, tm, prompt, tiles_per_seq):
    out_refs, (xb_ref, y_ref, proj0_ref, proj1_ref) = refs[:-4], refs[-4:]
    q_refs, cache_refs, kv_refs = out_refs[0::3], out_refs[1::3], out_refs[2::3]
    groups = ((A_WINDOW, 1),) + B_DIL

    col_q = (_C_QA,) + _C_QB
    col_kv = (_C_KVA,) + _C_KVB
    width_q = (A_Q,) + (B_G,) * len(B_DIL)
    heads_kv = (A_KV_HEADS,) + (B_SLOTS,) * len(B_DIL)

    def project(proj_ref, c0, width):
        proj_ref[:, c0:c0 + width] = _dot(xb_ref[...], w_ref[:, c0:c0 + width])

    def put_cache_rows(cache_ref, lead, kv_t, n_heads):
        wk = n_heads * HEAD_DIM
        for ref_row, our_row in _k_feature_rows(n_heads):
            cache_ref[lead, ref_row:ref_row + HALF, :] = kv_t[our_row:our_row + HALF]
        cache_ref[lead, wk:2 * wk, :] = kv_t[wk:2 * wk]

    def to_planes(val):
        n_col = val.shape[1] // LANES
        for c in range(n_col):
            y_ref[c] = val[:, c * LANES:(c + 1) * LANES]
        return n_col

    def rope(y, n_heads):
        cos = cos_ref[...]
        sin = sin_ref[...]
        if n_heads == B_SLOTS:
            return _rope_split(y, cos, sin)
        lane = lax.broadcasted_iota(jnp.int32, (tm, LANES), 1)
        return _rope_rot64(y, cos, jnp.where(lane < LANES // 2, -sin, sin))

    def roped_q(proj_ref, g):
        return rope(proj_ref[:, col_q[g]:col_q[g] + width_q[g]], heads_kv[g]) * SCALE

    def roped_kv(proj_ref, g):
        wk = heads_kv[g] * HEAD_DIM
        return jnp.concatenate([rope(proj_ref[:, col_kv[g]:col_kv[g] + wk], heads_kv[g]),
                                proj_ref[:, col_kv[g] + wk:col_kv[g] + 2 * wk]], axis=1)

    def deinterleave(val, out_ref, dil):
        n_col = to_planes(val)
        for r in range(dil):
            out_ref[0, r] = jnp.concatenate(
                [y_ref[c, pl.ds(r, tm // dil, stride=dil), :] for c in range(n_col)], axis=1).astype(BF16)

    if not prompt:
        xb_ref[...] = x_ref[...].astype(BF16)
        project(proj0_ref, 0, N_QKV)
        n_seq = tm // DEC_SEQ
        for g in range(len(groups)):
            q_refs[g][...] = roped_q(proj0_ref, g).astype(BF16)
            kv = roped_kv(proj0_ref, g)
            kv_refs[g][...] = kv
            n_col = to_planes(kv)
            for t in range(DEC_SEQ):
                kv_t = jnp.concatenate(
                    [y_ref[c, pl.ds(t, n_seq, stride=DEC_SEQ), :].T for c in range(n_col)], axis=0)
                put_cache_rows(cache_refs[g], t, kv_t, heads_kv[g])
        return

    step = pl.program_id(0)
    tile = jnp.maximum(step - 1, 0)
    last_tile = (tile % tiles_per_seq) == tiles_per_seq - 1

    @pl.when(step == 0)
    def _():
        proj1_ref[...] = jnp.zeros_like(proj1_ref)

    def step_body(mine, other):
        xb_ref[...] = x_ref[...].astype(BF16)
        for g, (win, dil) in enumerate(groups):
            project(mine, col_q[g], width_q[g])
            q = roped_q(other, g)
            if dil == 1:
                q_refs[g][...] = q.astype(BF16)
            else:
                deinterleave(q, q_refs[g], dil)
            project(mine, col_kv[g], 2 * heads_kv[g] * HEAD_DIM)
            kv = roped_kv(other, g)
            if dil == 1:
                kv_refs[g][...] = kv.astype(BF16)
            else:
                deinterleave(kv, kv_refs[g], dil)
            if min(win, SEQ) == SEQ:
                put_cache_rows(cache_refs[g], 0, kv.T, heads_kv[g])

        @pl.when(last_tile)
        def _():
            for g, (win, _) in enumerate(groups):
                keep = min(win, SEQ)
                if keep < SEQ:
                    assert keep <= tm
                    put_cache_rows(cache_refs[g], 0, roped_kv(other, g)[tm - keep:, :].T, heads_kv[g])

    for par, (mine, other) in enumerate(((proj0_ref, proj1_ref), (proj1_ref, proj0_ref))):
        @pl.when(step % 2 == par)
        def _(mine=mine, other=other):
            step_body(mine, other)


def _qkv_call(x2d, w_qkv, cos_t, sin_t, *, tm, prompt, tiles_per_seq, name):
    m = x2d.shape[0]
    n_tiles = m // tm
    if prompt:
        tile = lambda i: jnp.maximum(i - 1, 0)
        x_spec = pl.BlockSpec((tm, D_MODEL), lambda i: (jnp.minimum(i, n_tiles - 1), 0))
    else:
        assert n_tiles == 1
        tile = lambda i: i
        x_spec = pl.BlockSpec((tm, D_MODEL), lambda i: (i, 0))
    row = lambda w: pl.BlockSpec((tm, w), lambda i: (tile(i), 0))
    tab = pl.BlockSpec((tm, LANES), lambda i: (tile(i) % tiles_per_seq, 0))
    in_specs = [x_spec, _const_spec((D_MODEL, N_QKV)), tab, tab]
    sds = jax.ShapeDtypeStruct
    out_shape, out_specs = [], []
    for g, (win, dil) in enumerate(((A_WINDOW, 1),) + B_DIL):
        wq = A_Q if g == 0 else B_G
        wkv = 2 * (A_KV if g == 0 else B_G)
        if not prompt:
            out_shape += [sds((m, wq), BF16), sds((DEC_SEQ, wkv, m // DEC_SEQ), F32), sds((m, wkv), F32)]
            out_specs += [row(wq), pl.BlockSpec((DEC_SEQ, wkv, m // DEC_SEQ), lambda i: (0, 0, 0)),
                          row(wkv)]
            continue
        n_seq = m // SEQ
        keep = min(win, SEQ)
        if keep == SEQ:
            cache_spec = pl.BlockSpec(
                (1, wkv, tm), lambda i: (tile(i) // tiles_per_seq, 0, tile(i) % tiles_per_seq))
        else:
            cache_spec = pl.BlockSpec((1, wkv, keep), lambda i: (tile(i) // tiles_per_seq, 0, 0))
        if dil == 1:
            out_shape += [sds((m, wq), BF16), sds((n_seq, wkv, keep), F32), sds((m, wkv), BF16)]
            out_specs += [row(wq), cache_spec, row(wkv)]
        else:
            dspec = lambda w, dil=dil: pl.BlockSpec(
                (1, dil, tm // dil, w),
                lambda i: (tile(i) // tiles_per_seq, 0, tile(i) % tiles_per_seq, 0))
            out_shape += [sds((n_seq, dil, SEQ // dil, wq), BF16), sds((n_seq, wkv, keep), F32),
                          sds((n_seq, dil, SEQ // dil, wkv), BF16)]
            out_specs += [dspec(wq), cache_spec, dspec(wkv)]
    args = [x2d, w_qkv, cos_t, sin_t]
    proj_bufs = 2 if prompt else 1
    return pl.pallas_call(
        functools.partial(_qkv_kernel, tm=tm, prompt=prompt, tiles_per_seq=tiles_per_seq),
        grid=(n_tiles + 1,) if prompt else (n_tiles,),
        in_specs=in_specs,
        out_specs=out_specs,
        out_shape=out_shape,
        scratch_shapes=[pltpu.VMEM((tm, D_MODEL), BF16),
                        pltpu.VMEM((2 * B_G // LANES, tm, LANES), F32)]
        + [pltpu.VMEM((tm, N_QKV) if k < proj_bufs else (8, LANES), F32) for k in range(2)],
        compiler_params=pltpu.CompilerParams(
            dimension_semantics=("arbitrary",), vmem_limit_bytes=VMEM_LIMIT),
        name=name,
    )(*args)


def _a_query_masks(rows):
    lane = lax.broadcasted_iota(jnp.int32, (rows, LANES), 1)
    return [jnp.where(((lane >> 5) & (A_KV_HEADS - 1)) == j, 1.0, 0.0).astype(BF16)
            for j in range(A_KV_HEADS)]


def _fold_masks():
    row = lax.broadcasted_iota(jnp.int32, (BLK, BLK), 0)
    col = lax.broadcasted_iota(jnp.int32, (BLK, BLK), 1)
    return col > row


def _attn_scratch(heads, blocks_per_trip=1):
    return [pltpu.VMEM((2, blocks_per_trip * heads * BLK, 2 * BLK), F32)]


def _attn_pipeline(n_units, bufs, *, heads, load_q, load_k, load_v, has_prev, sink_of_head, finish,
                   mxu_row_sum=False):
    (sbuf,) = bufs
    upper = _fold_masks()
    upper_bf = jnp.where(upper, 1.0, 0.0).astype(BF16)
    width = BLK if has_prev is None else 2 * BLK
    per_trip = sbuf.shape[1] // (heads * BLK)
    assert n_units % (2 * per_trip) == 0
    n_trips = n_units // per_trip

    def scores(trip, par):
        for j in range(per_trip):
            u = trip * per_trip + j
            sbuf[par, j * heads * BLK:(j + 1) * heads * BLK, 0:width] = _nt_dot(load_q(u), load_k(u))

    def softmax_values(trip, par):
        for j in range(per_trip):
            block_softmax_values(trip * per_trip + j, par, j * heads * BLK)

    def block_softmax_values(u, par, row0):
        prev_ok = None if has_prev is None else has_prev(u)
        ps, ms, ls = [], [], []
        for h in range(heads):
            rows = slice(row0 + h * BLK, row0 + (h + 1) * BLK)
            if has_prev is None:
                sf = jnp.where(upper, NEG, sbuf[par, rows, 0:BLK])
            else:
                sf = jnp.where(upper, jnp.where(prev_ok, sbuf[par, rows, 0:BLK], NEG),
                               sbuf[par, rows, BLK:2 * BLK])
            mx = jnp.max(sf, axis=-1, keepdims=True)
            if sink_of_head is not None:
                mx = jnp.maximum(mx, sink_of_head(h))
            e = jnp.exp(sf - mx)
            if not mxu_row_sum:
                ls.append(jnp.broadcast_to(jnp.sum(e, axis=-1, keepdims=True), (BLK, LANES)))
            e = e.astype(BF16)
            if has_prev is None:
                ps.append(e)
            else:
                p_prev = e * upper_bf
                ps.append(jnp.concatenate([p_prev, e - p_prev], axis=1))
            ms.append(jnp.broadcast_to(mx, (BLK, LANES)))
        finish(u, _dot(jnp.concatenate(ps, axis=0), load_v(u)), jnp.concatenate(ms, axis=0),
               jnp.concatenate(ls, axis=0) if ls else None)

    def trip_pair(t, carry):
        for par in (1, 0):
            i = 2 * t + 2 - par
            scores(i, par)
            softmax_values(i - 1, 1 - par)
        return carry

    def last_trip(i, carry):
        scores(i, 1)
        softmax_values(i - 1, 0)
        return carry

    lax.fori_loop(0, 1, lambda i, carry: (scores(i, 0), carry)[1], 0)
    lax.fori_loop(0, n_trips // 2 - 1, trip_pair, 0)
    lax.fori_loop(n_trips - 1, n_trips, last_trip, 0)
    lax.fori_loop(n_trips - 1, n_trips, lambda i, carry: (softmax_values(i, 1), carry)[1], 0)


def _attn_a_kernel(sink_ref, q_ref, kv_ref, o_ref, *bufs):
    n_blk = q_ref.shape[0] // BLK
    lane = lax.broadcasted_iota(jnp.int32, (BLK, LANES), 1)
    hi = lane >= HEAD_DIM
    mask_bf = _a_query_masks(BLK)

    def rows_of(b):
        return pl.ds(pl.multiple_of(b * BLK, BLK), BLK)

    def prev_cur(b, cols):
        return jnp.concatenate([kv_ref[rows_of(jnp.maximum(b - 1, 0)), cols], kv_ref[rows_of(b), cols]],
                               axis=0)

    def load_q(b):
        qblk = q_ref[rows_of(b), :]
        return jnp.concatenate([qblk[:, g * LANES:(g + 1) * LANES] * mask_bf[j]
                                for j in range(A_KV_HEADS) for g in range(A_GROUP)], axis=0)

    ones = jnp.ones((2 * BLK, LANES), BF16)

    def load_v(b):
        return jnp.concatenate([prev_cur(b, slice(A_KV, 2 * A_KV)), ones], axis=1)

    def finish(b, o, m, _):
        def normalised(h):
            rows = slice(h * BLK, (h + 1) * BLK)
            den = o[rows, LANES:] + jnp.exp(sink_ref[h] - m[rows])
            return o[rows, :LANES] / den

        for g in range(A_GROUP):
            o_ref[rows_of(b), g * LANES:(g + 1) * LANES] = jnp.where(
                hi, normalised(A_GROUP + g), normalised(g)).astype(BF16)

    _attn_pipeline(n_blk, bufs, heads=A_Q_HEADS, load_q=load_q,
                   load_k=lambda b: prev_cur(b, slice(0, A_KV)), load_v=load_v,
                   has_prev=lambda b: b > 0, sink_of_head=lambda h: sink_ref[h], finish=finish,
                   mxu_row_sum=True)


def _attn_a_call(sink, q_a, kva_bf):
    n_seq = q_a.shape[0] // SEQ
    return pl.pallas_call(
        _attn_a_kernel,
        grid=(n_seq,),
        in_specs=[pl.BlockSpec(memory_space=pltpu.SMEM),
                  pl.BlockSpec((SEQ, A_Q), lambda n: (n, 0)),
                  pl.BlockSpec((SEQ, 2 * A_KV), lambda n: (n, 0))],
        out_specs=pl.BlockSpec((SEQ, A_Q), lambda n: (n, 0)),
        out_shape=jax.ShapeDtypeStruct(q_a.shape, BF16),
        scratch_shapes=_attn_scratch(A_Q_HEADS),
        compiler_params=pltpu.CompilerParams(
            dimension_semantics=("arbitrary",), vmem_limit_bytes=VMEM_LIMIT),
        name="attn_a_prompt",
    )(sink, q_a, kva_bf)


def _slot_masks(rows):
    lane = lax.broadcasted_iota(jnp.int32, (rows, B_G), 1)
    masks = [(lane >> 6) == s for s in range(B_SLOTS)]
    masks_bf = [jnp.where(((lane >> 5) & (B_SLOTS - 1)) == s, 1.0, 0.0).astype(BF16) for s in range(B_SLOTS)]
    return masks, masks_bf


def _attn_b_kernel(q1_ref, kv1_ref, q2_ref, kv2_ref, q3_ref, kv3_ref, o_ref,
                   acc2_ref, m2_ref, l2_ref, acc3_ref, m3_ref, l3_ref, *bufs):
    masks, masks_bf = _slot_masks(BLK)
    d2, d3 = B_DIL[1][1], B_DIL[2][1]
    nb2 = SEQ // d2 // BLK
    assert SEQ // d3 == BLK
    n_plane = B_G // LANES

    def put(ref, rows, val):
        for c in range(n_plane):
            ref[c, rows, :] = val[:, c * LANES:(c + 1) * LANES]

    def get(ref, rows):
        return jnp.concatenate([ref[c, rows, :] for c in range(n_plane)], axis=1)

    def stack_q(qblk):
        return jnp.concatenate([qblk * masks_bf[s] for s in range(B_SLOTS)], axis=0)

    def unstack(o, m, l):
        acc = jnp.where(masks[0], o[0:BLK], 0.0)
        rep = lambda x, s: jnp.concatenate([x[s * BLK:(s + 1) * BLK]] * n_plane, axis=1)
        mf, lf = rep(m, 0), rep(l, 0)
        for s in range(1, B_SLOTS):
            acc = jnp.where(masks[s], o[s * BLK:(s + 1) * BLK], acc)
            mf = jnp.where(masks[s], rep(m, s), mf)
            lf = jnp.where(masks[s], rep(l, s), lf)
        return acc, mf, lf

    def rows_of(b):
        return pl.ds(pl.multiple_of(b * BLK, BLK), BLK)

    run = functools.partial(_attn_pipeline, bufs=bufs, heads=B_SLOTS, sink_of_head=None)

    def finish3(r, o, m, l):
        rows = pl.ds(r, BLK, stride=d3)
        for ref, val in zip((acc3_ref, m3_ref, l3_ref), unstack(o, m, l)):
            put(ref, rows, val)

    run(d3, load_q=lambda r: stack_q(q3_ref[r]), load_k=lambda r: kv3_ref[r, :, 0:B_G],
        load_v=lambda r: kv3_ref[r, :, B_G:2 * B_G], has_prev=None, finish=finish3)

    def prev_cur2(u, cols):
        r, b = u // nb2, u % nb2
        return jnp.concatenate([kv2_ref[r, rows_of(jnp.maximum(b - 1, 0)), cols], kv2_ref[r, rows_of(b), cols]],
                               axis=0)

    def finish2(u, o, m, l):
        r, b = u // nb2, u % nb2
        rows = pl.ds(r + b * (BLK * d2), BLK, stride=d2)
        for ref, val in zip((acc2_ref, m2_ref, l2_ref), unstack(o, m, l)):
            put(ref, rows, val)

    run(d2 * nb2, load_q=lambda u: stack_q(q2_ref[u // nb2, rows_of(u % nb2), :]),
        load_k=lambda u: prev_cur2(u, slice(0, B_G)), load_v=lambda u: prev_cur2(u, slice(B_G, 2 * B_G)),
        has_prev=lambda u: (u % nb2) > 0, finish=finish2)

    def prev_cur1(b, cols):
        return jnp.concatenate([kv1_ref[rows_of(jnp.maximum(b - 1, 0)), cols], kv1_ref[rows_of(b), cols]],
                               axis=0)

    def finish1(b, o, m, l):
        acc1, m1, l1 = unstack(o, m, l)
        rows = rows_of(b)
        m2, m3 = get(m2_ref, rows), get(m3_ref, rows)
        mx = jnp.maximum(jnp.maximum(m1, m2), m3)
        w1, w2, w3 = jnp.exp(m1 - mx), jnp.exp(m2 - mx), jnp.exp(m3 - mx)
        num = w1 * acc1 + w2 * get(acc2_ref, rows) + w3 * get(acc3_ref, rows)
        den = w1 * l1 + w2 * get(l2_ref, rows) + w3 * get(l3_ref, rows)
        o_ref[rows, :] = (num / den).astype(BF16)

    run(SEQ // BLK, load_q=lambda b: stack_q(q1_ref[rows_of(b), :]),
        load_k=lambda b: prev_cur1(b, slice(0, B_G)), load_v=lambda b: prev_cur1(b, slice(B_G, 2 * B_G)),
        has_prev=lambda b: b > 0, finish=finish1)


def _attn_b_call(q1, kv1, q2, kv2, q3, kv3):
    n_seq = q1.shape[0] // SEQ
    d2, d3 = B_DIL[1][1], B_DIL[2][1]
    rows = lambda w: pl.BlockSpec((SEQ, w), lambda n: (n, 0))
    dsp = lambda d, w: pl.BlockSpec((None, d, SEQ // d, w), lambda n: (n, 0, 0, 0))
    return pl.pallas_call(
        _attn_b_kernel,
        grid=(n_seq,),
        in_specs=[rows(B_G), rows(2 * B_G), dsp(d2, B_G), dsp(d2, 2 * B_G), dsp(d3, B_G),
                  dsp(d3, 2 * B_G)],
        out_specs=rows(B_G),
        out_shape=jax.ShapeDtypeStruct((q1.shape[0], B_G), BF16),
        scratch_shapes=[pltpu.VMEM((B_G // LANES, SEQ, LANES), F32) for _ in range(6)]
        + _attn_scratch(B_SLOTS, blocks_per_trip=2),
        compiler_params=pltpu.CompilerParams(
            dimension_semantics=("arbitrary",), vmem_limit_bytes=VMEM_LIMIT),
        name="attn_b_prompt",
    )(q1, kv1, q2, kv2, q3, kv3)


def _sample_attn_block(sink_ref, qa_ref, kna_ref, ca_ref, q1_ref, kn1_ref, c1_ref,
                       q2_ref, kn2_ref, c2_ref, q3_ref, kn3_ref, c3_ref, *, tt, seqs, emit):
    pad = BLK - tt
    tt_shift, seq_shift = tt.bit_length() - 1, DEC_SEQ.bit_length() - 1
    assert tt == 1 << tt_shift and DEC_SEQ == 1 << seq_shift

    def pad_rows(x):
        return jnp.concatenate([x.astype(BF16), jnp.zeros((pad, x.shape[1]), BF16)], axis=0)

    def window_group(qm, kn_ref, c_ref, dil, sinkv, finish):
        rows, kd = qm.shape
        lc = c_ref.shape[2]
        kn = kn_ref[...]
        knpad = pad_rows(kn[:, :kd])
        vnpad = pad_rows(kn[:, kd:])
        s_new = _nt_dot(qm, knpad)
        def masks(width):
            col = lax.broadcasted_iota(jnp.int32, (rows, width), 1)
            t = lax.broadcasted_iota(jnp.int32, (rows, width), 0) & (DEC_SEQ - 1)
            same_res = (col & (dil - 1)) == (t & (dil - 1))
            return (col > t) & same_res, (col <= t) & same_res

        valid, _ = masks(lc)
        _, is_new = masks(BLK)
        k_rows = sorted(_k_feature_rows(kd // HEAD_DIM), key=lambda rows: rows[1])
        scores = []
        for pos, _ in seqs:
            k_t = jnp.concatenate([c_ref[pos, ref_row:ref_row + HALF, :] for ref_row, _ in k_rows],
                                  axis=0).astype(BF16)
            scores.append(_dot(qm, k_t))
        yield
        probs = []
        for score, (_, n) in zip(scores, seqs):
            s_c = jnp.where(valid, score, NEG)
            s_n = pltpu.roll(s_new, BLK - DEC_SEQ * n, 1) if n else s_new
            first = jnp.where(is_new, s_n, s_c[:, :BLK])
            s = first if lc == BLK else jnp.concatenate([first, s_c[:, BLK:]], axis=1)
            mx = jnp.max(s, axis=-1, keepdims=True)
            if sinkv is not None:
                mx = jnp.maximum(mx, sinkv)
            e = jnp.exp(s - mx)
            l = jnp.sum(e, axis=-1, keepdims=True)
            if sinkv is not None:
                l = l + jnp.exp(sinkv - mx)
            p_new = jnp.where(is_new, e[:, :BLK], 0.0)
            if n:
                p_new = pltpu.roll(p_new, DEC_SEQ * n, 1)
            probs.append((jnp.where(valid, e, 0.0).astype(BF16), p_new.astype(BF16), mx, l))
        yield
        for (p_c, p_new, mx, l), (pos, n) in zip(probs, seqs):
            v_t = c_ref[pos, kd:2 * kd, :].astype(BF16)
            finish(n, _nt_dot(p_c, v_t) + _dot(p_new, vnpad), mx, l)

    rows_a = A_Q_HEADS * tt
    lane_t = lax.broadcasted_iota(jnp.int32, (tt, LANES), 1)
    hi_t = lane_t >= HEAD_DIM
    mask_bf = _a_query_masks(tt)
    qa = qa_ref[...]
    qm = jnp.concatenate([qa[:, g * LANES:(g + 1) * LANES] * mask_bf[j]
                          for j in range(A_KV_HEADS) for g in range(A_GROUP)], axis=0)
    head = lax.broadcasted_iota(jnp.int32, (rows_a, 1), 0) >> tt_shift
    sinkv = jnp.zeros((rows_a, 1), F32)
    for h in range(A_Q_HEADS):
        sinkv = jnp.where(head == h, sink_ref[h], sinkv)
    seq_t = lax.broadcasted_iota(jnp.int32, (tt, LANES), 0) >> seq_shift
    out_a = [jnp.zeros((tt, LANES), F32) for _ in range(A_GROUP)]

    def finish_a(n, o, mx, l):
        o = o / l
        for g in range(A_GROUP):
            og = jnp.where(hi_t, o[(A_GROUP + g) * tt:(A_GROUP + g + 1) * tt], o[g * tt:(g + 1) * tt])
            out_a[g] = jnp.where(seq_t == n, og, out_a[g])

    stages = [window_group(qm, kna_ref, ca_ref, 1, sinkv, finish_a)]

    masks_t, masks_bf_t = _slot_masks(tt)
    tok_t = lax.broadcasted_iota(jnp.int32, (tt, B_G), 0)

    def stack_q(q):
        return jnp.concatenate([q * masks_bf_t[s] for s in range(B_SLOTS)], axis=0)

    def unstack(o, mx, l):
        acc = jnp.where(masks_t[0], o[0:tt], 0.0)
        mf = jnp.broadcast_to(mx[0:tt], (tt, B_G))
        lf = jnp.broadcast_to(l[0:tt], (tt, B_G))
        for s in range(1, B_SLOTS):
            acc = jnp.where(masks_t[s], o[s * tt:(s + 1) * tt], acc)
            mf = jnp.where(masks_t[s], mx[s * tt:(s + 1) * tt], mf)
            lf = jnp.where(masks_t[s], l[s * tt:(s + 1) * tt], lf)
        return acc, mf, lf

    zero, one = jnp.zeros((tt, B_G), F32), jnp.ones((tt, B_G), F32)
    stats = []
    for (win, dil), q_ref, kn_ref, c_ref in zip(B_DIL, (q1_ref, q2_ref, q3_ref),
                                                (kn1_ref, kn2_ref, kn3_ref), (c1_ref, c2_ref, c3_ref)):
        assert c_ref.shape[2] == win
        group = [zero, zero, one]

        def finish_b(n, o, mx, l, group=group):
            sel = (tok_t >> seq_shift) == n
            for k, new in enumerate(unstack(o, mx, l)):
                group[k] = jnp.where(sel, new, group[k])

        stages.append(window_group(stack_q(q_ref[...]), kn_ref, c_ref, dil, None, finish_b))
        stats.append(group)

    for phase in range(3):
        for stage in stages:
            next(stage, None)
        if phase < 2:
            yield
    (acc1, m1, l1), (acc2, m2, l2), (acc3, m3, l3) = stats
    mx = jnp.maximum(jnp.maximum(m1, m2), m3)
    w1, w2, w3 = jnp.exp(m1 - mx), jnp.exp(m2 - mx), jnp.exp(m3 - mx)
    emit(out_a, (w1 * acc1 + w2 * acc2 + w3 * acc3) / (w1 * l1 + w2 * l2 + w3 * l3))


def _sample_attn_rider(in_refs, oa_ref, ob_ref, k, n_seqs):
    seq_shift = DEC_SEQ.bit_length() - 1

    def merge(ref, cols, new):
        if k:
            mine = (lax.broadcasted_iota(jnp.int32, new.shape, 0) >> seq_shift) == k
            new = jnp.where(mine, new, ref[:, cols].astype(F32))
        ref[:, cols] = new.astype(BF16)

    def emit(out_a, out_b):
        for c in range(A_GROUP):
            merge(oa_ref, slice(c * LANES, (c + 1) * LANES), out_a[c])
        merge(ob_ref, slice(0, B_G), out_b)

    yield from _sample_attn_block(*in_refs, tt=n_seqs * DEC_SEQ, seqs=[(0, k)], emit=emit)


def _layernorm(x, g, b):
    mu = jnp.mean(x, axis=-1, keepdims=True)
    xc = x - mu
    var = jnp.mean(xc * xc, axis=-1, keepdims=True)
    return xc * lax.rsqrt(var + LN_EPS) * g + b


def _gelu_exact(x):
    return 0.5 * x * (1.0 + lax.erf(x * (0.5 ** 0.5)))


_FF_CHUNK = 256
_CARRY = 8


def _ffn_kernel(*refs, tm, sample, tiles_per_seq):
    if sample:
        (x_ref, oa_ref, ob_ref, fill_ref, wg_ref, wa_ref, wb_ref, wo_ref, g1_ref, b1_ref,
         wup_ref, cw_ref, cb_ref, wdn_ref, g2_ref, b2_ref, y_ref, u_ref,
         xb_ref, m_ref, gg_ref, ext_ref, h0_ref, hb0_ref) = refs
    else:
        x_ref, oa_ref, ob_ref = refs[:3]
        rider_in, refs = refs[3:3 + _N_SAMPLE_ATTN_IN], refs[3 + _N_SAMPLE_ATTN_IN:]
        (wg_ref, wa_ref, wb_ref, wo_ref, g1_ref, b1_ref,
         wup_ref, cw_ref, cb_ref, wdn_ref, g2_ref, b2_ref, y_ref, ulast_ref, oa_s_ref, ob_s_ref,
         xb_ref, m_ref, gg_ref, ext_ref, h0_ref, hb0_ref, carry_ref) = refs
    half = D_MODEL // 2

    def merge_pieces(h_ref, hb_ref):
        def gate_half(c):
            if c == 0:
                xb_ref[...] = x_ref[...].astype(BF16)
            cs = slice(c * half, (c + 1) * half)
            ga = _dot(xb_ref[...], wg_ref[:, c * half:(c + 1) * half])
            gb = _dot(xb_ref[...], wg_ref[:, D_MODEL + c * half:D_MODEL + (c + 1) * half])
            ta = _dot(oa_ref[...], wa_ref[:, cs])
            tb = _dot(ob_ref[...], wb_ref[:, cs])
            m_ref[:, cs] = (jax.nn.sigmoid(ga) * ta + jax.nn.sigmoid(gb) * tb).astype(BF16)

        def out_proj():
            mix = _dot(m_ref[...], wo_ref[...])
            h = _layernorm(ALPHA * x_ref[...] + mix, g1_ref[...], b1_ref[...])
            h_ref[...] = h
            hb_ref[...] = h.astype(BF16)

        return [functools.partial(gate_half, 0), functools.partial(gate_half, 1), out_proj]

    def ffn_pieces(h_ref, hb_ref):
        def chunk(c):
            cs = slice(c * _FF_CHUNK, (c + 1) * _FF_CHUNK)
            u = _dot(hb_ref[...], wup_ref[:, c * _FF_CHUNK:(c + 1) * _FF_CHUNK])
            v = _dot(hb_ref[...], wup_ref[:, D_FF + c * _FF_CHUNK:D_FF + (c + 1) * _FF_CHUNK])
            if sample:
                ext_ref[0:_CARRY, :] = jnp.zeros((_CARRY, _FF_CHUNK), F32)
            else:
                ext_ref[0:_CARRY, :] = carry_ref[:, cs]
            ext_ref[_CARRY:_CARRY + tm, :] = u
            u1 = ext_ref[_CARRY - 1:_CARRY - 1 + tm, :]
            u2 = ext_ref[_CARRY - 2:_CARRY - 2 + tm, :]
            if sample:
                t = lax.broadcasted_iota(jnp.int32, (tm, _FF_CHUNK), 0) & (DEC_SEQ - 1)
                fill = fill_ref[:, cs]
                u1 = jnp.where(t >= 1, u1, pltpu.roll(fill, tm - 1, 0))
                u2 = jnp.where(t >= 2, u2, fill)
                u_ref[:, cs] = u
            else:
                tail = u[tm - _CARRY:tm, :]
                carry_ref[:, cs] = tail
                ulast_ref[0, :, cs] = tail
            a = cb_ref[:, cs] + cw_ref[0:1, cs] * u2 + cw_ref[1:2, cs] * u1 + cw_ref[2:3, cs] * u
            gg_ref[:, cs] = (_gelu_exact(a) * v).astype(BF16)

        def down():
            f = _dot(gg_ref[...], wdn_ref[...])
            y_ref[...] = _layernorm(ALPHA * h_ref[...] + f, g2_ref[...], b2_ref[...])

        return [functools.partial(chunk, c) for c in range(D_FF // _FF_CHUNK)] + [down]

    pieces = merge_pieces(h0_ref, hb0_ref) + ffn_pieces(h0_ref, hb0_ref)
    if sample:
        for piece in pieces:
            piece()
        return

    step, sub = pl.program_id(0), pl.program_id(1)

    @pl.when((step % tiles_per_seq == 0) & (sub == 0))
    def _():
        carry_ref[...] = jnp.zeros_like(carry_ref)

    n_chunk = D_FF // _FF_CHUNK
    phases = (pieces[:3], pieces[3:3 + n_chunk // 2], pieces[3 + n_chunk // 2:3 + n_chunk], pieces[3 + n_chunk:])
    assert len(phases) == _FFN_PHASES
    for g, phase in enumerate(phases):
        @pl.when(sub == g)
        def _(g=g, phase=phase):
            rider = _sample_attn_rider(rider_in, oa_s_ref, ob_s_ref, g, _FFN_PHASES)
            for piece in phase:
                next(rider, None)
                piece()
            for _ in rider:
                pass


_FFN_PHASES = 4
_N_SAMPLE_ATTN_IN = 13


def _ffn_call(x2d, oa, ob, weights, *, tm, sample, tiles_per_seq, conv_fill=None, sample_attn_args=(), name):
    m = x2d.shape[0]
    n_tiles = m // tm
    sds = jax.ShapeDtypeStruct
    row_in = row_out = lambda w: pl.BlockSpec((tm, w), lambda i, *_: (i, 0))
    in_specs = [row_in(D_MODEL), row_in(A_Q), row_in(B_G)]
    args = [x2d, oa, ob]
    if sample:
        in_specs += [row_in(D_FF)]
        args += [conv_fill]
    else:
        assert len(sample_attn_args) == _N_SAMPLE_ATTN_IN
        tt = _FFN_PHASES * DEC_SEQ
        n_tok = sample_attn_args[1].shape[0]
        assert n_tok == n_tiles * tt
        in_specs.append(pl.BlockSpec(memory_space=pltpu.SMEM))
        for k, a in enumerate(sample_attn_args[1:]):
            if k % 3 == 2:
                in_specs.append(pl.BlockSpec((1,) + a.shape[1:], lambda i, g: (_FFN_PHASES * i + g, 0, 0)))
            else:
                in_specs.append(pl.BlockSpec((tt, a.shape[1]), lambda i, g: (i, 0)))
        args += list(sample_attn_args)
    in_specs += [_const_spec(w.shape) for w in weights]
    args += list(weights)
    h_bufs = [pltpu.VMEM((tm, D_MODEL), F32), pltpu.VMEM((tm, D_MODEL), BF16)]
    scratch = [pltpu.VMEM((tm, D_MODEL), BF16), pltpu.VMEM((tm, D_MODEL), BF16),
               pltpu.VMEM((tm, D_FF), BF16), pltpu.VMEM((tm + _CARRY, _FF_CHUNK), F32)] + h_bufs
    if sample:
        out_shape = [sds((m, D_MODEL), F32), sds((m, D_FF), F32)]
        out_specs = [row_out(D_MODEL), row_out(D_FF)]
    else:
        out_shape = [sds((m, D_MODEL), F32), sds((n_tiles, _CARRY, D_FF), F32),
                     sds((n_tok, A_Q), BF16), sds((n_tok, B_G), BF16)]
        out_specs = [row_out(D_MODEL), pl.BlockSpec((1, _CARRY, D_FF), lambda i, g: (i, 0, 0)),
                     pl.BlockSpec((tt, A_Q), lambda i, g: (i, 0)), pl.BlockSpec((tt, B_G), lambda i, g: (i, 0))]
        scratch += [pltpu.VMEM((_CARRY, D_FF), F32)]
    grid = (n_tiles,) if sample else (n_tiles, _FFN_PHASES)
    return pl.pallas_call(
        functools.partial(_ffn_kernel, tm=tm, sample=sample, tiles_per_seq=tiles_per_seq),
        grid=grid,
        in_specs=in_specs,
        out_specs=out_specs,
        out_shape=out_shape,
        scratch_shapes=scratch,
        compiler_params=pltpu.CompilerParams(
            dimension_semantics=("arbitrary",) * len(grid), vmem_limit_bytes=VMEM_LIMIT),
        name=name,
    )(*args)


def _rope_tables(pos):
    half = HEAD_DIM // 2
    inv = ROPE_THETA ** (-jnp.arange(half, dtype=F32) / half)
    ang = pos.astype(F32)[:, None] * inv[None, :]
    cos, sin = jnp.cos(ang), jnp.sin(ang)
    reps = LANES // HALF
    return jnp.tile(cos, (1, reps)), jnp.tile(sin, (1, reps))


def _rotary_order(w, heads):
    x1 = [w[:, h * HEAD_DIM:h * HEAD_DIM + HALF] for h in heads]
    x2 = [w[:, h * HEAD_DIM + HALF:(h + 1) * HEAD_DIM] for h in heads]
    return x1 + x2


def _qkv_source_columns():
    w_in = np.arange(N_QKV, dtype=np.int32)[None, :]
    qa = w_in[:, 0:A_Q]
    ka = w_in[:, A_Q:A_Q + A_KV]
    va = w_in[:, A_Q + A_KV:A_Q + 2 * A_KV]
    b0 = A_Q + 2 * A_KV
    qb = w_in[:, b0:b0 + 3 * B_G]
    kb = w_in[:, b0 + 3 * B_G:b0 + 6 * B_G]
    vb = w_in[:, b0 + 6 * B_G:b0 + 9 * B_G]
    cols = []
    for c in range(A_GROUP):
        cols += _rotary_order(qa, (c, A_GROUP + c))
    cols += _rotary_order(ka, range(A_KV_HEADS)) + [va]
    for g in range(len(B_DIL)):
        gs = slice(g * B_G, (g + 1) * B_G)
        cols += _rotary_order(qb[:, gs], range(B_SLOTS)) + _rotary_order(kb[:, gs], range(B_SLOTS)) + [vb[:, gs]]
    return np.concatenate(cols, axis=1)[0]


def _in_weight_kernel(src_ref, w_ref, qkv_ref, gate_ref):
    src = _qkv_source_columns()
    rows = w_ref.shape[0]
    lane = lax.broadcasted_iota(jnp.int32, (LANES, LANES), 0)
    for j in range(N_QKV // LANES):
        want = src[j * LANES:(j + 1) * LANES]
        if np.array_equal(want, want[0] + np.arange(LANES)) and want[0] % LANES == 0:
            qkv_ref[:, j * LANES:(j + 1) * LANES] = w_ref[:, want[0]:want[0] + LANES].astype(BF16)
            continue
        acc = jnp.zeros((rows, LANES), F32)
        for b in sorted(set(int(c) // LANES for c in want)):
            pick = (lane + b * LANES) == src_ref[:, j * LANES:(j + 1) * LANES]
            acc = acc + _dot(w_ref[:, b * LANES:(b + 1) * LANES].astype(BF16), jnp.where(pick, 1.0, 0.0).astype(BF16))
        qkv_ref[:, j * LANES:(j + 1) * LANES] = acc.astype(BF16)
    gate_ref[...] = w_ref[:, N_QKV:].astype(BF16)


def _prep_in_weights(w_in):
    rows = 256
    n_proj = w_in.shape[1]
    src = jnp.asarray(_qkv_source_columns())[None, :]
    return pl.pallas_call(
        _in_weight_kernel,
        grid=(D_MODEL // rows,),
        in_specs=[pl.BlockSpec((1, N_QKV), lambda i: (0, 0)), pl.BlockSpec((rows, n_proj), lambda i: (i, 0))],
        out_specs=[pl.BlockSpec((rows, N_QKV), lambda i: (i, 0)),
                   pl.BlockSpec((rows, n_proj - N_QKV), lambda i: (i, 0))],
        out_shape=[jax.ShapeDtypeStruct((D_MODEL, N_QKV), BF16),
                   jax.ShapeDtypeStruct((D_MODEL, n_proj - N_QKV), BF16)],
        compiler_params=pltpu.CompilerParams(dimension_semantics=("arbitrary",), vmem_limit_bytes=VMEM_LIMIT),
        name="in_weights",
    )(src, w_in)


def _prep_br_a(w_br_a):
    rows = []
    for c in range(A_GROUP):
        rows += [w_br_a[c * HEAD_DIM:(c + 1) * HEAD_DIM],
                 w_br_a[(A_GROUP + c) * HEAD_DIM:(A_GROUP + c + 1) * HEAD_DIM]]
    return jnp.concatenate(rows, axis=0).astype(BF16)


_TM_PROMPT = 512


def kernel(x_prompt, x_sample, cache_a, cache_b1, cache_b2, cache_b3, state_conv, w_in, sink_a, w_br_a, w_br_b, w_o, ln1_g, ln1_b, w_up, conv_w, conv_b, w_down, ln2_g, ln2_b):
    assert DEPTH == 1
    l = 0
    w_qkv, w_gate = _prep_in_weights(w_in[l])
    ffn_weights = (w_gate, _prep_br_a(w_br_a[l]), w_br_b[l].astype(BF16), w_o[l].astype(BF16),
                   ln1_g[l][None], ln1_b[l][None], w_up[l].astype(BF16), conv_w[l], conv_b[l][None],
                   w_down[l].astype(BF16), ln2_g[l][None], ln2_b[l][None])
    sink = sink_a[l].astype(F32)

    ms = DEC_BATCH * DEC_SEQ
    xs = x_sample.reshape(ms, D_MODEL)
    cos_s, sin_s = _rope_tables(PAST_LEN + (jnp.arange(ms, dtype=jnp.int32) % DEC_SEQ))
    (qa_s, ca_s, kva_s, qb1_s, cb1_s, kvb1_s, qb2_s, cb2_s, kvb2_s, qb3_s, cb3_s, kvb3_s) = _qkv_call(
        xs, w_qkv, cos_s, sin_s, tm=ms, prompt=False, tiles_per_seq=1, name="qkv_sample")

    def window_buffer(c):
        return c.transpose(0, 2, 3, 4, 1).reshape(c.shape[0], -1, c.shape[1])

    ca = window_buffer(cache_a[l])
    cb = [window_buffer(c[l]) for c in (cache_b1, cache_b2, cache_b3)]
    sample_attn_args = (sink, qa_s, kva_s, ca, qb1_s, kvb1_s, cb[0], qb2_s, kvb2_s, cb[1], qb3_s, kvb3_s, cb[2])

    mp = BATCH * SEQ
    xp = x_prompt.reshape(mp, D_MODEL)
    cos_p, sin_p = _rope_tables(jnp.arange(SEQ, dtype=jnp.int32))
    tps = SEQ // _TM_PROMPT
    (qa, ca_p, kva_bf, qb1, cb1_p, kvb1_bf, qb2, cb2_p, kvb2_bf, qb3, cb3_p, kvb3_bf) = _qkv_call(
        xp, w_qkv, cos_p, sin_p, tm=_TM_PROMPT, prompt=True, tiles_per_seq=tps, name="qkv_prompt")
    oa = _attn_a_call(sink, qa, kva_bf)
    ob = _attn_b_call(qb1, kvb1_bf, qb2, kvb2_bf, qb3, kvb3_bf)
    y_p, ulast, oa_s, ob_s = _ffn_call(xp, oa, ob, ffn_weights, tm=_TM_PROMPT, sample=False, tiles_per_seq=tps,
                                       sample_attn_args=sample_attn_args, name="ffn_prompt")

    def prompt_cache(c, heads):
        return c.reshape(BATCH, 2, heads, HEAD_DIM, c.shape[-1]).transpose(0, 4, 1, 2, 3)[None]

    y_prompt = y_p.reshape(BATCH, SEQ, D_MODEL)
    cache_a_prompt = prompt_cache(ca_p, A_KV_HEADS)
    kvb_p = [prompt_cache(c, B_SLOTS) for c in (cb1_p, cb2_p, cb3_p)]
    state_conv_prompt = ulast.reshape(BATCH, tps, _CARRY, D_FF)[None, :, tps - 1, _CARRY - (CONV_W - 1):]

    fill = jnp.pad(state_conv[l], ((0, 0), (0, DEC_SEQ - (CONV_W - 1)), (0, 0))).reshape(ms, D_FF)
    y_s, u_s = _ffn_call(xs, oa_s, ob_s, ffn_weights, tm=ms, sample=True, tiles_per_seq=1,
                         conv_fill=fill, name="ffn_sample")

    def sample_cache(c, heads):
        return c.reshape(DEC_SEQ, 2, heads, HEAD_DIM, DEC_BATCH).transpose(4, 0, 1, 2, 3)[None]

    y_sample = y_s.reshape(DEC_BATCH, DEC_SEQ, D_MODEL)
    cache_a_sample = sample_cache(ca_s, A_KV_HEADS)
    kvb_s = [sample_cache(c, B_SLOTS) for c in (cb1_s, cb2_s, cb3_s)]
    state_conv_sample = u_s.reshape(DEC_BATCH, DEC_SEQ, D_FF)[None, :, DEC_SEQ - (CONV_W - 1):]

    return (y_prompt, y_sample, cache_a_prompt, cache_a_sample, kvb_p[0], kvb_s[0], kvb_p[1], kvb_s[1],
            kvb_p[2], kvb_s[2], state_conv_prompt, state_conv_sample)
```

```python
import functools

import jax
import jax.numpy as jnp
import numpy as np
from jax import lax
from jax.experimental import pallas as pl
from jax.experimental.pallas import tpu as pltpu

D_MODEL = 1024
BATCH = 8
SEQ = 2048
DEPTH = 1
DEC_BATCH = 128
DEC_SEQ = 4
PAST_LEN = 16384
HEAD_DIM = 64
A_Q_HEADS = 8
A_KV_HEADS = 2
A_GROUP = A_Q_HEADS // A_KV_HEADS
A_WINDOW = 128
B_DIL = ((128, 1), (512, 4), (2048, 16))
B_SLOTS = 4
BLK = 128
ROPE_THETA = 10000.0
D_FF = ((8 * D_MODEL // 3 + 127) // 128) * 128
CONV_W = 3
ALPHA = (2 * DEPTH) ** 0.25
LN_EPS = 1e-5
NEG = -1e30
SCALE = HEAD_DIM ** -0.5
A_Q = A_Q_HEADS * HEAD_DIM
A_KV = A_KV_HEADS * HEAD_DIM
B_G = B_SLOTS * HEAD_DIM
N_QKV = A_Q + 2 * A_KV + 3 * 3 * B_G

LANES = 128
VMEM_LIMIT = 56 * 1024 * 1024

BF16 = jnp.bfloat16
F32 = jnp.float32

_C_QA = 0
_C_KVA = _C_QA + A_Q
_C_QB = (_C_KVA + 2 * A_KV, _C_KVA + 2 * A_KV + 3 * B_G, _C_KVA + 2 * A_KV + 6 * B_G)
_C_KVB = tuple(c + B_G for c in _C_QB)


def _const_spec(shape):
    nd = len(shape)
    return pl.BlockSpec(shape, lambda *_: (0,) * nd, pipeline_mode=pl.Buffered(1))


def _nt_dot(a, b):
    return lax.dot_general(a, b, (((1,), (1,)), ((), ())), preferred_element_type=F32)


def _dot(a, b):
    return jnp.dot(a, b, preferred_element_type=F32)


HALF = HEAD_DIM // 2


def _rope_split(y, cos4, sin4):
    x1, x2 = y[:, :LANES], y[:, LANES:]
    return jnp.concatenate([x1 * cos4 - x2 * sin4, x2 * cos4 + x1 * sin4], axis=1)


def _rope_rot64(y, cos4, sin_signed):
    outs = []
    for j in range(y.shape[1] // LANES):
        yj = y[:, j * LANES:(j + 1) * LANES]
        outs.append(yj * cos4 + pltpu.roll(yj, LANES // 2, 1) * sin_signed)
    return outs[0] if len(outs) == 1 else jnp.concatenate(outs, axis=1)


def _k_feature_rows(n_heads):
    return [(HEAD_DIM * h + HALF * part, n_heads * HALF * part + HALF * h)
            for part in range(2) for h in range(n_heads)]


def _qkv_kernel(x_ref, w_ref, cos_ref, sin_ref, *refs, tm, prompt, tiles_per_seq):
    out_refs, (xb_ref, y_ref, proj0_ref, proj1_ref) = refs[:-4], refs[-4:]
    q_refs, cache_refs, kv_refs = out_refs[0::3], out_refs[1::3], out_refs[2::3]
    groups = ((A_WINDOW, 1),) + B_DIL

    col_q = (_C_QA,) + _C_QB
    col_kv = (_C_KVA,) + _C_KVB
    width_q = (A_Q,) + (B_G,) * len(B_DIL)
    heads_kv = (A_KV_HEADS,) + (B_SLOTS,) * len(B_DIL)

    def project(proj_ref, c0, width):
        proj_ref[:, c0:c0 + width] = _dot(xb_ref[...], w_ref[:, c0:c0 + width])

    def put_cache_rows(cache_ref, lead, kv_t, n_heads):
        wk = n_heads * HEAD_DIM
        for ref_row, our_row in _k_feature_rows(n_heads):
            cache_ref[lead, ref_row:ref_row + HALF, :] = kv_t[our_row:our_row + HALF]
        cache_ref[lead, wk:2 * wk, :] = kv_t[wk:2 * wk]

    def to_planes(val):
        n_col = val.shape[1] // LANES
        for c in range(n_col):
            y_ref[c] = val[:, c * LANES:(c + 1) * LANES]
        return n_col

    def rope(y, n_heads):
        cos = cos_ref[...]
        sin = sin_ref[...]
        if n_heads == B_SLOTS:
            return _rope_split(y, cos, sin)
        lane = lax.broadcasted_iota(jnp.int32, (tm, LANES), 1)
        return _rope_rot64(y, cos, jnp.where(lane < LANES // 2, -sin, sin))

    def roped_q(proj_ref, g):
        return rope(proj_ref[:, col_q[g]:col_q[g] + width_q[g]], heads_kv[g]) * SCALE

    def roped_kv(proj_ref, g):
        wk = heads_kv[g] * HEAD_DIM
        return jnp.concatenate([rope(proj_ref[:, col_kv[g]:col_kv[g] + wk], heads_kv[g]),
                                proj_ref[:, col_kv[g] + wk:col_kv[g] + 2 * wk]], axis=1)

    def deinterleave(val, out_ref, dil):
        n_col = to_planes(val)
        for r in range(dil):
            out_ref[0, r] = jnp.concatenate(
                [y_ref[c, pl.ds(r, tm // dil, stride=dil), :] for c in range(n_col)], axis=1).astype(BF16)

    if not prompt:
        xb_ref[...] = x_ref[...].astype(BF16)
        project(proj0_ref, 0, N_QKV)
        n_seq = tm // DEC_SEQ
        for g in range(len(groups)):
            q_refs[g][...] = roped_q(proj0_ref, g).astype(BF16)
            kv = roped_kv(proj0_ref, g)
            kv_refs[g][...] = kv
            n_col = to_planes(kv)
            for t in range(DEC_SEQ):
                kv_t = jnp.concatenate(
                    [y_ref[c, pl.ds(t, n_seq, stride=DEC_SEQ), :].T for c in range(n_col)], axis=0)
                put_cache_rows(cache_refs[g], t, kv_t, heads_kv[g])
        return

    step = pl.program_id(0)
    tile = jnp.maximum(step - 1, 0)
    last_tile = (tile % tiles_per_seq) == tiles_per_seq - 1

    @pl.when(step == 0)
    def _():
        proj1_ref[...] = jnp.zeros_like(proj1_ref)

    def step_body(mine, other):
        xb_ref[...] = x_ref[...].astype(BF16)
        for g, (win, dil) in enumerate(groups):
            project(mine, col_q[g], width_q[g])
            q = roped_q(other, g)
            if dil == 1:
                q_refs[g][...] = q.astype(BF16)
            else:
                deinterleave(q, q_refs[g], dil)
            project(mine, col_kv[g], 2 * heads_kv[g] * HEAD_DIM)
            kv = roped_kv(other, g)
            if dil == 1:
                kv_refs[g][...] = kv.astype(BF16)
            else:
                deinterleave(kv, kv_refs[g], dil)
            if min(win, SEQ) == SEQ:
                put_cache_rows(cache_refs[g], 0, kv.T, heads_kv[g])

        @pl.when(last_tile)
        def _():
            for g, (win, _) in enumerate(groups):
                keep = min(win, SEQ)
                if keep < SEQ:
                    assert keep <= tm
                    put_cache_rows(cache_refs[g], 0, roped_kv(other, g)[tm - keep:, :].T, heads_kv[g])

    for par, (mine, other) in enumerate(((proj0_ref, proj1_ref), (proj1_ref, proj0_ref))):
        @pl.when(step % 2 == par)
        def _(mine=mine, other=other):
            step_body(mine, other)


def _qkv_call(x2d, w_qkv, cos_t, sin_t, *, tm, prompt, tiles_per_seq, name):
    m = x2d.shape[0]
    n_tiles = m // tm
    if prompt:
        tile = lambda i: jnp.maximum(i - 1, 0)
        x_spec = pl.BlockSpec((tm, D_MODEL), lambda i: (jnp.minimum(i, n_tiles - 1), 0))
    else:
        assert n_tiles == 1
        tile = lambda i: i
        x_spec = pl.BlockSpec((tm, D_MODEL), lambda i: (i, 0))
    row = lambda w: pl.BlockSpec((tm, w), lambda i: (tile(i), 0))
    tab = pl.BlockSpec((tm, LANES), lambda i: (tile(i) % tiles_per_seq, 0))
    in_specs = [x_spec, _const_spec((D_MODEL, N_QKV)), tab, tab]
    sds = jax.ShapeDtypeStruct
    out_shape, out_specs = [], []
    for g, (win, dil) in enumerate(((A_WINDOW, 1),) + B_DIL):
        wq = A_Q if g == 0 else B_G
        wkv = 2 * (A_KV if g == 0 else B_G)
        if not prompt:
            out_shape += [sds((m, wq), BF16), sds((DEC_SEQ, wkv, m // DEC_SEQ), F32), sds((m, wkv), F32)]
            out_specs += [row(wq), pl.BlockSpec((DEC_SEQ, wkv, m // DEC_SEQ), lambda i: (0, 0, 0)),
                          row(wkv)]
            continue
        n_seq = m // SEQ
        keep = min(win, SEQ)
        if keep == SEQ:
            cache_spec = pl.BlockSpec(
                (1, wkv, tm), lambda i: (tile(i) // tiles_per_seq, 0, tile(i) % tiles_per_seq))
        else:
            cache_spec = pl.BlockSpec((1, wkv, keep), lambda i: (tile(i) // tiles_per_seq, 0, 0))
        if dil == 1:
            out_shape += [sds((m, wq), BF16), sds((n_seq, wkv, keep), F32), sds((m, wkv), BF16)]
            out_specs += [row(wq), cache_spec, row(wkv)]
        else:
            dspec = lambda w, dil=dil: pl.BlockSpec(
                (1, dil, tm // dil, w),
                lambda i: (tile(i) // tiles_per_seq, 0, tile(i) % tiles_per_seq, 0))
            out_shape += [sds((n_seq, dil, SEQ // dil, wq), BF16), sds((n_seq, wkv, keep), F32),
                          sds((n_seq, dil, SEQ // dil, wkv), BF16)]
            out_specs += [dspec(wq), cache_spec, dspec(wkv)]
    args = [x2d, w_qkv, cos_t, sin_t]
    proj_bufs = 2 if prompt else 1
    return pl.pallas_call(
        functools.partial(_qkv_kernel, tm=tm, prompt=prompt, tiles_per_seq=tiles_per_seq),
        grid=(n_tiles + 1,) if prompt else (n_tiles,),
        in_specs=in_specs,
        out_specs=out_specs,
        out_shape=out_shape,
        scratch_shapes=[pltpu.VMEM((tm, D_MODEL), BF16),
                        pltpu.VMEM((2 * B_G // LANES, tm, LANES), F32)]
        + [pltpu.VMEM((tm, N_QKV) if k < proj_bufs else (8, LANES), F32) for k in range(2)],
        compiler_params=pltpu.CompilerParams(
            dimension_semantics=("arbitrary",), vmem_limit_bytes=VMEM_LIMIT),
        name=name,
    )(*args)


def _a_query_masks(rows):
    lane = lax.broadcasted_iota(jnp.int32, (rows, LANES), 1)
    return [jnp.where(((lane >> 5) & (A_KV_HEADS - 1)) == j, 1.0, 0.0).astype(BF16)
            for j in range(A_KV_HEADS)]


def _fold_masks():
    row = lax.broadcasted_iota(jnp.int32, (BLK, BLK), 0)
    col = lax.broadcasted_iota(jnp.int32, (BLK, BLK), 1)
    return col > row


def _attn_scratch(heads, blocks_per_trip=1):
    return [pltpu.VMEM((2, blocks_per_trip * heads * BLK, 2 * BLK), F32)]


def _attn_pipeline(n_units, bufs, *, heads, load_q, load_k, load_v, has_prev, sink_of_head, finish,
                   mxu_row_sum=False):
    (sbuf,) = bufs
    upper = _fold_masks()
    upper_bf = jnp.where(upper, 1.0, 0.0).astype(BF16)
    width = BLK if has_prev is None else 2 * BLK
    per_trip = sbuf.shape[1] // (heads * BLK)
    assert n_units % (2 * per_trip) == 0
    n_trips = n_units // per_trip

    def scores(trip, par):
        for j in range(per_trip):
            u = trip * per_trip + j
            sbuf[par, j * heads * BLK:(j + 1) * heads * BLK, 0:width] = _nt_dot(load_q(u), load_k(u))

    def softmax_values(trip, par):
        for j in range(per_trip):
            block_softmax_values(trip * per_trip + j, par, j * heads * BLK)

    def block_softmax_values(u, par, row0):
        prev_ok = None if has_prev is None else has_prev(u)
        ps, ms, ls = [], [], []
        for h in range(heads):
            rows = slice(row0 + h * BLK, row0 + (h + 1) * BLK)
            if has_prev is None:
                sf = jnp.where(upper, NEG, sbuf[par, rows, 0:BLK])
            else:
                sf = jnp.where(upper, jnp.where(prev_ok, sbuf[par, rows, 0:BLK], NEG),
                               sbuf[par, rows, BLK:2 * BLK])
            mx = jnp.max(sf, axis=-1, keepdims=True)
            if sink_of_head is not None:
                mx = jnp.maximum(mx, sink_of_head(h))
            e = jnp.exp(sf - mx)
            if not mxu_row_sum:
                ls.append(jnp.broadcast_to(jnp.sum(e, axis=-1, keepdims=True), (BLK, LANES)))
            e = e.astype(BF16)
            if has_prev is None:
                ps.append(e)
            else:
                p_prev = e * upper_bf
                ps.append(jnp.concatenate([p_prev, e - p_prev], axis=1))
            ms.append(jnp.broadcast_to(mx, (BLK, LANES)))
        finish(u, _dot(jnp.concatenate(ps, axis=0), load_v(u)), jnp.concatenate(ms, axis=0),
               jnp.concatenate(ls, axis=0) if ls else None)

    def trip_pair(t, carry):
        for par in (1, 0):
            i = 2 * t + 2 - par
            scores(i, par)
            softmax_values(i - 1, 1 - par)
        return carry

    def last_trip(i, carry):
        scores(i, 1)
        softmax_values(i - 1, 0)
        return carry

    lax.fori_loop(0, 1, lambda i, carry: (scores(i, 0), carry)[1], 0)
    lax.fori_loop(0, n_trips // 2 - 1, trip_pair, 0)
    lax.fori_loop(n_trips - 1, n_trips, last_trip, 0)
    lax.fori_loop(n_trips - 1, n_trips, lambda i, carry: (softmax_values(i, 1), carry)[1], 0)


def _attn_a_kernel(sink_ref, q_ref, kv_ref, o_ref, *bufs):
    n_blk = q_ref.shape[0] // BLK
    lane = lax.broadcasted_iota(jnp.int32, (BLK, LANES), 1)
    hi = lane >= HEAD_DIM
    mask_bf = _a_query_masks(BLK)

    def rows_of(b):
        return pl.ds(pl.multiple_of(b * BLK, BLK), BLK)

    def prev_cur(b, cols):
        return jnp.concatenate([kv_ref[rows_of(jnp.maximum(b - 1, 0)), cols], kv_ref[rows_of(b), cols]],
                               axis=0)

    def load_q(b):
        qblk = q_ref[rows_of(b), :]
        return jnp.concatenate([qblk[:, g * LANES:(g + 1) * LANES] * mask_bf[j]
                                for j in range(A_KV_HEADS) for g in range(A_GROUP)], axis=0)

    ones = jnp.ones((2 * BLK, LANES), BF16)

    def load_v(b):
        return jnp.concatenate([prev_cur(b, slice(A_KV, 2 * A_KV)), ones], axis=1)

    def finish(b, o, m, _):
        def normalised(h):
            rows = slice(h * BLK, (h + 1) * BLK)
            den = o[rows, LANES:] + jnp.exp(sink_ref[h] - m[rows])
            return o[rows, :LANES] / den

        for g in range(A_GROUP):
            o_ref[rows_of(b), g * LANES:(g + 1) * LANES] = jnp.where(
                hi, normalised(A_GROUP + g), normalised(g)).astype(BF16)

    _attn_pipeline(n_blk, bufs, heads=A_Q_HEADS, load_q=load_q,
                   load_k=lambda b: prev_cur(b, slice(0, A_KV)), load_v=load_v,
                   has_prev=lambda b: b > 0, sink_of_head=lambda h: sink_ref[h], finish=finish,
                   mxu_row_sum=True)


def _attn_a_call(sink, q_a, kva_bf):
    n_seq = q_a.shape[0] // SEQ
    return pl.pallas_call(
        _attn_a_kernel,
        grid=(n_seq,),
        in_specs=[pl.BlockSpec(memory_space=pltpu.SMEM),
                  pl.BlockSpec((SEQ, A_Q), lambda n: (n, 0)),
                  pl.BlockSpec((SEQ, 2 * A_KV), lambda n: (n, 0))],
        out_specs=pl.BlockSpec((SEQ, A_Q), lambda n: (n, 0)),
        out_shape=jax.ShapeDtypeStruct(q_a.shape, BF16),
        scratch_shapes=_attn_scratch(A_Q_HEADS),
        compiler_params=pltpu.CompilerParams(
            dimension_semantics=("arbitrary",), vmem_limit_bytes=VMEM_LIMIT),
        name="attn_a_prompt",
    )(sink, q_a, kva_bf)


def _slot_masks(rows):
    lane = lax.broadcasted_iota(jnp.int32, (rows, B_G), 1)
    masks = [(lane >> 6) == s for s in range(B_SLOTS)]
    masks_bf = [jnp.where(((lane >> 5) & (B_SLOTS - 1)) == s, 1.0, 0.0).astype(BF16) for s in range(B_SLOTS)]
    return masks, masks_bf


def _attn_b_kernel(q1_ref, kv1_ref, q2_ref, kv2_ref, q3_ref, kv3_ref, o_ref,
                   acc2_ref, m2_ref, l2_ref, acc3_ref, m3_ref, l3_ref, *bufs):
    masks, masks_bf = _slot_masks(BLK)
    d2, d3 = B_DIL[1][1], B_DIL[2][1]
    nb2 = SEQ // d2 // BLK
    assert SEQ // d3 == BLK
    n_plane = B_G // LANES

    def put(ref, rows, val):
        for c in range(n_plane):
            ref[c, rows, :] = val[:, c * LANES:(c + 1) * LANES]

    def get(ref, rows):
        return jnp.concatenate([ref[c, rows, :] for c in range(n_plane)], axis=1)

    def stack_q(qblk):
        return jnp.concatenate([qblk * masks_bf[s] for s in range(B_SLOTS)], axis=0)

    def unstack(o, m, l):
        acc = jnp.where(masks[0], o[0:BLK], 0.0)
        rep = lambda x, s: jnp.concatenate([x[s * BLK:(s + 1) * BLK]] * n_plane, axis=1)
        mf, lf = rep(m, 0), rep(l, 0)
        for s in range(1, B_SLOTS):
            acc = jnp.where(masks[s], o[s * BLK:(s + 1) * BLK], acc)
            mf = jnp.where(masks[s], rep(m, s), mf)
            lf = jnp.where(masks[s], rep(l, s), lf)
        return acc, mf, lf

    def rows_of(b):
        return pl.ds(pl.multiple_of(b * BLK, BLK), BLK)

    run = functools.partial(_attn_pipeline, bufs=bufs, heads=B_SLOTS, sink_of_head=None)

    def finish3(r, o, m, l):
        rows = pl.ds(r, BLK, stride=d3)
        for ref, val in zip((acc3_ref, m3_ref, l3_ref), unstack(o, m, l)):
            put(ref, rows, val)

    run(d3, load_q=lambda r: stack_q(q3_ref[r]), load_k=lambda r: kv3_ref[r, :, 0:B_G],
        load_v=lambda r: kv3_ref[r, :, B_G:2 * B_G], has_prev=None, finish=finish3)

    def prev_cur2(u, cols):
        r, b = u // nb2, u % nb2
        return jnp.concatenate([kv2_ref[r, rows_of(jnp.maximum(b - 1, 0)), cols], kv2_ref[r, rows_of(b), cols]],
                               axis=0)

    def finish2(u, o, m, l):
        r, b = u // nb2, u % nb2
        rows = pl.ds(r + b * (BLK * d2), BLK, stride=d2)
        for ref, val in zip((acc2_ref, m2_ref, l2_ref), unstack(o, m, l)):
            put(ref, rows, val)

    run(d2 * nb2, load_q=lambda u: stack_q(q2_ref[u // nb2, rows_of(u % nb2), :]),
        load_k=lambda u: prev_cur2(u, slice(0, B_G)), load_v=lambda u: prev_cur2(u, slice(B_G, 2 * B_G)),
        has_prev=lambda u: (u % nb2) > 0, finish=finish2)

    def prev_cur1(b, cols):
        return jnp.concatenate([kv1_ref[rows_of(jnp.maximum(b - 1, 0)), cols], kv1_ref[rows_of(b), cols]],
                               axis=0)

    def finish1(b, o, m, l):
        acc1, m1, l1 = unstack(o, m, l)
        rows = rows_of(b)
        m2, m3 = get(m2_ref, rows), get(m3_ref, rows)
        mx = jnp.maximum(jnp.maximum(m1, m2), m3)
        w1, w2, w3 = jnp.exp(m1 - mx), jnp.exp(m2 - mx), jnp.exp(m3 - mx)
        num = w1 * acc1 + w2 * get(acc2_ref, rows) + w3 * get(acc3_ref, rows)
        den = w1 * l1 + w2 * get(l2_ref, rows) + w3 * get(l3_ref, rows)
        o_ref[rows, :] = (num / den).astype(BF16)

    run(SEQ // BLK, load_q=lambda b: stack_q(q1_ref[rows_of(b), :]),
        load_k=lambda b: prev_cur1(b, slice(0, B_G)), load_v=lambda b: prev_cur1(b, slice(B_G, 2 * B_G)),
        has_prev=lambda b: b > 0, finish=finish1)


def _attn_b_call(q1, kv1, q2, kv2, q3, kv3):
    n_seq = q1.shape[0] // SEQ
    d2, d3 = B_DIL[1][1], B_DIL[2][1]
    rows = lambda w: pl.BlockSpec((SEQ, w), lambda n: (n, 0))
    dsp = lambda d, w: pl.BlockSpec((None, d, SEQ // d, w), lambda n: (n, 0, 0, 0))
    return pl.pallas_call(
        _attn_b_kernel,
        grid=(n_seq,),
        in_specs=[rows(B_G), rows(2 * B_G), dsp(d2, B_G), dsp(d2, 2 * B_G), dsp(d3, B_G),
                  dsp(d3, 2 * B_G)],
        out_specs=rows(B_G),
        out_shape=jax.ShapeDtypeStruct((q1.shape[0], B_G), BF16),
        scratch_shapes=[pltpu.VMEM((B_G // LANES, SEQ, LANES), F32) for _ in range(6)]
        + _attn_scratch(B_SLOTS, blocks_per_trip=2),
        compiler_params=pltpu.CompilerParams(
            dimension_semantics=("arbitrary",), vmem_limit_bytes=VMEM_LIMIT),
        name="attn_b_prompt",
    )(q1, kv1, q2, kv2, q3, kv3)


def _sample_attn_block(sink_ref, qa_ref, kna_ref, ca_ref, q1_ref, kn1_ref, c1_ref,
                       q2_ref, kn2_ref, c2_ref, q3_ref, kn3_ref, c3_ref, *, tt, seqs, emit):
    pad = BLK - tt
    tt_shift, seq_shift = tt.bit_length() - 1, DEC_SEQ.bit_length() - 1
    assert tt == 1 << tt_shift and DEC_SEQ == 1 << seq_shift

    def pad_rows(x):
        return jnp.concatenate([x.astype(BF16), jnp.zeros((pad, x.shape[1]), BF16)], axis=0)

    def window_group(qm, kn_ref, c_ref, dil, sinkv, finish):
        rows, kd = qm.shape
        lc = c_ref.shape[2]
        kn = kn_ref[...]
        knpad = pad_rows(kn[:, :kd])
        vnpad = pad_rows(kn[:, kd:])
        s_new = _nt_dot(qm, knpad)
        def masks(width):
            col = lax.broadcasted_iota(jnp.int32, (rows, width), 1)
            t = lax.broadcasted_iota(jnp.int32, (rows, width), 0) & (DEC_SEQ - 1)
            same_res = (col & (dil - 1)) == (t & (dil - 1))
            return (col > t) & same_res, (col <= t) & same_res

        valid, _ = masks(lc)
        _, is_new = masks(BLK)
        k_rows = sorted(_k_feature_rows(kd // HEAD_DIM), key=lambda rows: rows[1])
        scores = []
        for pos, _ in seqs:
            k_t = jnp.concatenate([c_ref[pos, ref_row:ref_row + HALF, :] for ref_row, _ in k_rows],
                                  axis=0).astype(BF16)
            scores.append(_dot(qm, k_t))
        yield
        probs = []
        for score, (_, n) in zip(scores, seqs):
            s_c = jnp.where(valid, score, NEG)
            s_n = pltpu.roll(s_new, BLK - DEC_SEQ * n, 1) if n else s_new
            first = jnp.where(is_new, s_n, s_c[:, :BLK])
            s = first if lc == BLK else jnp.concatenate([first, s_c[:, BLK:]], axis=1)
            mx = jnp.max(s, axis=-1, keepdims=True)
            if sinkv is not None:
                mx = jnp.maximum(mx, sinkv)
            e = jnp.exp(s - mx)
            l = jnp.sum(e, axis=-1, keepdims=True)
            if sinkv is not None:
                l = l + jnp.exp(sinkv - mx)
            p_new = jnp.where(is_new, e[:, :BLK], 0.0)
            if n:
                p_new = pltpu.roll(p_new, DEC_SEQ * n, 1)
            probs.append((jnp.where(valid, e, 0.0).astype(BF16), p_new.astype(BF16), mx, l))
        yield
        for (p_c, p_new, mx, l), (pos, n) in zip(probs, seqs):
            v_t = c_ref[pos, kd:2 * kd, :].astype(BF16)
            finish(n, _nt_dot(p_c, v_t) + _dot(p_new, vnpad), mx, l)

    rows_a = A_Q_HEADS * tt
    lane_t = lax.broadcasted_iota(jnp.int32, (tt, LANES), 1)
    hi_t = lane_t >= HEAD_DIM
    mask_bf = _a_query_masks(tt)
    qa = qa_ref[...]
    qm = jnp.concatenate([qa[:, g * LANES:(g + 1) * LANES] * mask_bf[j]
                          for j in range(A_KV_HEADS) for g in range(A_GROUP)], axis=0)
    head = lax.broadcasted_iota(jnp.int32, (rows_a, 1), 0) >> tt_shift
    sinkv = jnp.zeros((rows_a, 1), F32)
    for h in range(A_Q_HEADS):
        sinkv = jnp.where(head == h, sink_ref[h], sinkv)
    seq_t = lax.broadcasted_iota(jnp.int32, (tt, LANES), 0) >> seq_shift
    out_a = [jnp.zeros((tt, LANES), F32) for _ in range(A_GROUP)]

    def finish_a(n, o, mx, l):
        o = o / l
        for g in range(A_GROUP):
            og = jnp.where(hi_t, o[(A_GROUP + g) * tt:(A_GROUP + g + 1) * tt], o[g * tt:(g + 1) * tt])
            out_a[g] = jnp.where(seq_t == n, og, out_a[g])

    stages = [window_group(qm, kna_ref, ca_ref, 1, sinkv, finish_a)]

    masks_t, masks_bf_t = _slot_masks(tt)
    tok_t = lax.broadcasted_iota(jnp.int32, (tt, B_G), 0)

    def stack_q(q):
        return jnp.concatenate([q * masks_bf_t[s] for s in range(B_SLOTS)], axis=0)

    def unstack(o, mx, l):
        acc = jnp.where(masks_t[0], o[0:tt], 0.0)
        mf = jnp.broadcast_to(mx[0:tt], (tt, B_G))
        lf = jnp.broadcast_to(l[0:tt], (tt, B_G))
        for s in range(1, B_SLOTS):
            acc = jnp.where(masks_t[s], o[s * tt:(s + 1) * tt], acc)
            mf = jnp.where(masks_t[s], mx[s * tt:(s + 1) * tt], mf)
            lf = jnp.where(masks_t[s], l[s * tt:(s + 1) * tt], lf)
        return acc, mf, lf

    zero, one = jnp.zeros((tt, B_G), F32), jnp.ones((tt, B_G), F32)
    stats = []
    for (win, dil), q_ref, kn_ref, c_ref in zip(B_DIL, (q1_ref, q2_ref, q3_ref),
                                                (kn1_ref, kn2_ref, kn3_ref), (c1_ref, c2_ref, c3_ref)):
        assert c_ref.shape[2] == win
        group = [zero, zero, one]

        def finish_b(n, o, mx, l, group=group):
            sel = (tok_t >> seq_shift) == n
            for k, new in enumerate(unstack(o, mx, l)):
                group[k] = jnp.where(sel, new, group[k])

        stages.append(window_group(stack_q(q_ref[...]), kn_ref, c_ref, dil, None, finish_b))
        stats.append(group)

    for phase in range(3):
        for stage in stages:
            next(stage, None)
        if phase < 2:
            yield
    (acc1, m1, l1), (acc2, m2, l2), (acc3, m3, l3) = stats
    mx = jnp.maximum(jnp.maximum(m1, m2), m3)
    w1, w2, w3 = jnp.exp(m1 - mx), jnp.exp(m2 - mx), jnp.exp(m3 - mx)
    emit(out_a, (w1 * acc1 + w2 * acc2 + w3 * acc3) / (w1 * l1 + w2 * l2 + w3 * l3))


def _sample_attn_rider(in_refs, oa_ref, ob_ref, k, n_seqs):
    seq_shift = DEC_SEQ.bit_length() - 1

    def merge(ref, cols, new):
        if k:
            mine = (lax.broadcasted_iota(jnp.int32, new.shape, 0) >> seq_shift) == k
            new = jnp.where(mine, new, ref[:, cols].astype(F32))
        ref[:, cols] = new.astype(BF16)

    def emit(out_a, out_b):
        for c in range(A_GROUP):
            merge(oa_ref, slice(c * LANES, (c + 1) * LANES), out_a[c])
        merge(ob_ref, slice(0, B_G), out_b)

    yield from _sample_attn_block(*in_refs, tt=n_seqs * DEC_SEQ, seqs=[(0, k)], emit=emit)


def _layernorm(x, g, b):
    mu = jnp.mean(x, axis=-1, keepdims=True)
    xc = x - mu
    var = jnp.mean(xc * xc, axis=-1, keepdims=True)
    return xc * lax.rsqrt(var + LN_EPS) * g + b


def _gelu_exact(x):
    return 0.5 * x * (1.0 + lax.erf(x * (0.5 ** 0.5)))


_FF_CHUNK = 256
_CARRY = 8


def _ffn_kernel(*refs, tm, sample, tiles_per_seq):
    if sample:
        (x_ref, oa_ref, ob_ref, fill_ref, wg_ref, wa_ref, wb_ref, wo_ref, g1_ref, b1_ref,
         wup_ref, cw_ref, cb_ref, wdn_ref, g2_ref, b2_ref, y_ref, u_ref,
         xb_ref, m_ref, gg_ref, ext_ref, h0_ref, hb0_ref) = refs
    else:
        x_ref, oa_ref, ob_ref = refs[:3]
        rider_in, refs = refs[3:3 + _N_SAMPLE_ATTN_IN], refs[3 + _N_SAMPLE_ATTN_IN:]
        (wg_ref, wa_ref, wb_ref, wo_ref, g1_ref, b1_ref,
         wup_ref, cw_ref, cb_ref, wdn_ref, g2_ref, b2_ref, y_ref, ulast_ref, oa_s_ref, ob_s_ref,
         xb_ref, m_ref, gg_ref, ext_ref, h0_ref, hb0_ref, carry_ref) = refs
    half = D_MODEL // 2

    def merge_pieces(h_ref, hb_ref):
        def gate_half(c):
            if c == 0:
                xb_ref[...] = x_ref[...].astype(BF16)
            cs = slice(c * half, (c + 1) * half)
            ga = _dot(xb_ref[...], wg_ref[:, c * half:(c + 1) * half])
            gb = _dot(xb_ref[...], wg_ref[:, D_MODEL + c * half:D_MODEL + (c + 1) * half])
            ta = _dot(oa_ref[...], wa_ref[:, cs])
            tb = _dot(ob_ref[...], wb_ref[:, cs])
            m_ref[:, cs] = (jax.nn.sigmoid(ga) * ta + jax.nn.sigmoid(gb) * tb).astype(BF16)

        def out_proj():
            mix = _dot(m_ref[...], wo_ref[...])
            h = _layernorm(ALPHA * x_ref[...] + mix, g1_ref[...], b1_ref[...])
            h_ref[...] = h
            hb_ref[...] = h.astype(BF16)

        return [functools.partial(gate_half, 0), functools.partial(gate_half, 1), out_proj]

    def ffn_pieces(h_ref, hb_ref):
        def chunk(c):
            cs = slice(c * _FF_CHUNK, (c + 1) * _FF_CHUNK)
            u = _dot(hb_ref[...], wup_ref[:, c * _FF_CHUNK:(c + 1) * _FF_CHUNK])
            v = _dot(hb_ref[...], wup_ref[:, D_FF + c * _FF_CHUNK:D_FF + (c + 1) * _FF_CHUNK])
            if sample:
                ext_ref[0:_CARRY, :] = jnp.zeros((_CARRY, _FF_CHUNK), F32)
            else:
                ext_ref[0:_CARRY, :] = carry_ref[0:_CARRY, cs]
            ext_ref[_CARRY:_CARRY + tm, :] = u
            u1 = ext_ref[_CARRY - 1:_CARRY - 1 + tm, :]
            u2 = ext_ref[_CARRY - 2:_CARRY - 2 + tm, :]
            if sample:
                t = lax.broadcasted_iota(jnp.int32, (tm, _FF_CHUNK), 0) & (DEC_SEQ - 1)
                fill = fill_ref[:, cs]
                u1 = jnp.where(t >= 1, u1, pltpu.roll(fill, tm - 1, 0))
                u2 = jnp.where(t >= 2, u2, fill)
                u_ref[:, cs] = u
            else:
                tail = u[tm - _CARRY:tm, :]
                carry_ref[0:_CARRY, cs] = tail
                ulast_ref[0, :, cs] = tail
            a = cb_ref[:, cs] + cw_ref[0:1, cs] * u2 + cw_ref[1:2, cs] * u1 + cw_ref[2:3, cs] * u
            gg_ref[:, cs] = (_gelu_exact(a) * v).astype(BF16)

        def down():
            f = _dot(gg_ref[...], wdn_ref[...])
            y_ref[...] = _layernorm(ALPHA * h_ref[...] + f, g2_ref[...], b2_ref[...])

        return [functools.partial(chunk, c) for c in range(D_FF // _FF_CHUNK)] + [down]

    pieces = merge_pieces(h0_ref, hb0_ref) + ffn_pieces(h0_ref, hb0_ref)
    if sample:
        for piece in pieces:
            piece()
        return

    step, sub = pl.program_id(0), pl.program_id(1)

    @pl.when((step % tiles_per_seq == 0) & (sub == 0))
    def _():
        carry_ref[...] = jnp.zeros_like(carry_ref)

    n_chunk = D_FF // _FF_CHUNK
    phases = (pieces[:3], pieces[3:3 + n_chunk // 2], pieces[3 + n_chunk // 2:3 + n_chunk], pieces[3 + n_chunk:])
    assert len(phases) == _FFN_PHASES
    for g, phase in enumerate(phases):
        @pl.when(sub == g)
        def _(g=g, phase=phase):
            rider = _sample_attn_rider(rider_in, oa_s_ref, ob_s_ref, g, _FFN_PHASES)
            for piece in phase:
                next(rider, None)
                piece()
            for _ in rider:
                pass


_FFN_PHASES = 4
_N_SAMPLE_ATTN_IN = 13


def _ffn_call(x2d, oa, ob, weights, *, tm, sample, tiles_per_seq, conv_fill=None, sample_attn_args=(), name):
    m = x2d.shape[0]
    n_tiles = m // tm
    sds = jax.ShapeDtypeStruct
    row_in = row_out = lambda w: pl.BlockSpec((tm, w), lambda i, *_: (i, 0))
    in_specs = [row_in(D_MODEL), row_in(A_Q), row_in(B_G)]
    args = [x2d, oa, ob]
    if sample:
        in_specs += [row_in(D_FF)]
        args += [conv_fill]
    else:
        assert len(sample_attn_args) == _N_SAMPLE_ATTN_IN
        tt = _FFN_PHASES * DEC_SEQ
        n_tok = sample_attn_args[1].shape[0]
        assert n_tok == n_tiles * tt
        in_specs.append(pl.BlockSpec(memory_space=pltpu.SMEM))
        for k, a in enumerate(sample_attn_args[1:]):
            if k % 3 == 2:
                in_specs.append(pl.BlockSpec((1,) + a.shape[1:], lambda i, g: (_FFN_PHASES * i + g, 0, 0)))
            else:
                in_specs.append(pl.BlockSpec((tt, a.shape[1]), lambda i, g: (i, 0)))
        args += list(sample_attn_args)
    in_specs += [_const_spec(w.shape) for w in weights]
    args += list(weights)
    h_bufs = [pltpu.VMEM((tm, D_MODEL), F32), pltpu.VMEM((tm, D_MODEL), BF16)]
    scratch = [pltpu.VMEM((tm, D_MODEL), BF16), pltpu.VMEM((tm, D_MODEL), BF16),
               pltpu.VMEM((tm, D_FF), BF16), pltpu.VMEM((tm + 2 * _CARRY, _FF_CHUNK), F32)] + h_bufs
    if sample:
        out_shape = [sds((m, D_MODEL), F32), sds((m, D_FF), F32)]
        out_specs = [row_out(D_MODEL), row_out(D_FF)]
    else:
        out_shape = [sds((m, D_MODEL), F32), sds((n_tiles, _CARRY, D_FF), F32),
                     sds((n_tok, A_Q), BF16), sds((n_tok, B_G), BF16)]
        out_specs = [row_out(D_MODEL), pl.BlockSpec((1, _CARRY, D_FF), lambda i, g: (i, 0, 0)),
                     pl.BlockSpec((tt, A_Q), lambda i, g: (i, 0)), pl.BlockSpec((tt, B_G), lambda i, g: (i, 0))]
        scratch += [pltpu.VMEM((2 * _CARRY, D_FF), F32)]
    assert all(int(np.prod(s.shape)) * jnp.dtype(s.dtype).itemsize % (16 << 10) == 0 for s in scratch)
    grid = (n_tiles,) if sample else (n_tiles, _FFN_PHASES)
    return pl.pallas_call(
        functools.partial(_ffn_kernel, tm=tm, sample=sample, tiles_per_seq=tiles_per_seq),
        grid=grid,
        in_specs=in_specs,
        out_specs=out_specs,
        out_shape=out_shape,
        scratch_shapes=scratch,
        compiler_params=pltpu.CompilerParams(
            dimension_semantics=("arbitrary",) * len(grid), vmem_limit_bytes=VMEM_LIMIT),
        name=name,
    )(*args)


def _rope_tables(pos):
    half = HEAD_DIM // 2
    inv = ROPE_THETA ** (-jnp.arange(half, dtype=F32) / half)
    ang = pos.astype(F32)[:, None] * inv[None, :]
    cos, sin = jnp.cos(ang), jnp.sin(ang)
    reps = LANES // HALF
    return jnp.tile(cos, (1, reps)), jnp.tile(sin, (1, reps))


def _rotary_order(w, heads):
    x1 = [w[:, h * HEAD_DIM:h * HEAD_DIM + HALF] for h in heads]
    x2 = [w[:, h * HEAD_DIM + HALF:(h + 1) * HEAD_DIM] for h in heads]
    return x1 + x2


def _qkv_source_columns():
    w_in = np.arange(N_QKV, dtype=np.int32)[None, :]
    qa = w_in[:, 0:A_Q]
    ka = w_in[:, A_Q:A_Q + A_KV]
    va = w_in[:, A_Q + A_KV:A_Q + 2 * A_KV]
    b0 = A_Q + 2 * A_KV
    qb = w_in[:, b0:b0 + 3 * B_G]
    kb = w_in[:, b0 + 3 * B_G:b0 + 6 * B_G]
    vb = w_in[:, b0 + 6 * B_G:b0 + 9 * B_G]
    cols = []
    for c in range(A_GROUP):
        cols += _rotary_order(qa, (c, A_GROUP + c))
    cols += _rotary_order(ka, range(A_KV_HEADS)) + [va]
    for g in range(len(B_DIL)):
        gs = slice(g * B_G, (g + 1) * B_G)
        cols += _rotary_order(qb[:, gs], range(B_SLOTS)) + _rotary_order(kb[:, gs], range(B_SLOTS)) + [vb[:, gs]]
    return np.concatenate(cols, axis=1)[0]


def _in_weight_kernel(src_ref, w_ref, qkv_ref, gate_ref):
    src = _qkv_source_columns()
    rows = w_ref.shape[0]
    lane = lax.broadcasted_iota(jnp.int32, (LANES, LANES), 0)
    for j in range(N_QKV // LANES):
        want = src[j * LANES:(j + 1) * LANES]
        if np.array_equal(want, want[0] + np.arange(LANES)) and want[0] % LANES == 0:
            qkv_ref[:, j * LANES:(j + 1) * LANES] = w_ref[:, want[0]:want[0] + LANES].astype(BF16)
            continue
        acc = jnp.zeros((rows, LANES), F32)
        for b in sorted(set(int(c) // LANES for c in want)):
            pick = (lane + b * LANES) == src_ref[:, j * LANES:(j + 1) * LANES]
            acc = acc + _dot(w_ref[:, b * LANES:(b + 1) * LANES].astype(BF16), jnp.where(pick, 1.0, 0.0).astype(BF16))
        qkv_ref[:, j * LANES:(j + 1) * LANES] = acc.astype(BF16)
    gate_ref[...] = w_ref[:, N_QKV:].astype(BF16)


def _prep_in_weights(w_in):
    rows = 256
    n_proj = w_in.shape[1]
    src = jnp.asarray(_qkv_source_columns())[None, :]
    return pl.pallas_call(
        _in_weight_kernel,
        grid=(D_MODEL // rows,),
        in_specs=[pl.BlockSpec((1, N_QKV), lambda i: (0, 0)), pl.BlockSpec((rows, n_proj), lambda i: (i, 0))],
        out_specs=[pl.BlockSpec((rows, N_QKV), lambda i: (i, 0)),
                   pl.BlockSpec((rows, n_proj - N_QKV), lambda i: (i, 0))],
        out_shape=[jax.ShapeDtypeStruct((D_MODEL, N_QKV), BF16),
                   jax.ShapeDtypeStruct((D_MODEL, n_proj - N_QKV), BF16)],
        compiler_params=pltpu.CompilerParams(dimension_semantics=("arbitrary",), vmem_limit_bytes=VMEM_LIMIT),
        name="in_weights",
    )(src, w_in)


def _prep_br_a(w_br_a):
    rows = []
    for c in range(A_GROUP):
        rows += [w_br_a[c * HEAD_DIM:(c + 1) * HEAD_DIM],
                 w_br_a[(A_GROUP + c) * HEAD_DIM:(A_GROUP + c + 1) * HEAD_DIM]]
    return jnp.concatenate(rows, axis=0).astype(BF16)


_TM_PROMPT = 512


def kernel(x_prompt, x_sample, cache_a, cache_b1, cache_b2, cache_b3, state_conv, w_in, sink_a, w_br_a, w_br_b, w_o, ln1_g, ln1_b, w_up, conv_w, conv_b, w_down, ln2_g, ln2_b):
    assert DEPTH == 1
    l = 0
    w_qkv, w_gate = _prep_in_weights(w_in[l])
    ffn_weights = (w_gate, _prep_br_a(w_br_a[l]), w_br_b[l].astype(BF16), w_o[l].astype(BF16),
                   ln1_g[l][None], ln1_b[l][None], w_up[l].astype(BF16), conv_w[l], conv_b[l][None],
                   w_down[l].astype(BF16), ln2_g[l][None], ln2_b[l][None])
    sink = sink_a[l].astype(F32)

    ms = DEC_BATCH * DEC_SEQ
    xs = x_sample.reshape(ms, D_MODEL)
    cos_s, sin_s = _rope_tables(PAST_LEN + (jnp.arange(ms, dtype=jnp.int32) % DEC_SEQ))
    (qa_s, ca_s, kva_s, qb1_s, cb1_s, kvb1_s, qb2_s, cb2_s, kvb2_s, qb3_s, cb3_s, kvb3_s) = _qkv_call(
        xs, w_qkv, cos_s, sin_s, tm=ms, prompt=False, tiles_per_seq=1, name="qkv_sample")

    def window_buffer(c):
        return c.transpose(0, 2, 3, 4, 1).reshape(c.shape[0], -1, c.shape[1])

    ca = window_buffer(cache_a[l])
    cb = [window_buffer(c[l]) for c in (cache_b1, cache_b2, cache_b3)]
    sample_attn_args = (sink, qa_s, kva_s, ca, qb1_s, kvb1_s, cb[0], qb2_s, kvb2_s, cb[1], qb3_s, kvb3_s, cb[2])

    mp = BATCH * SEQ
    xp = x_prompt.reshape(mp, D_MODEL)
    cos_p, sin_p = _rope_tables(jnp.arange(SEQ, dtype=jnp.int32))
    tps = SEQ // _TM_PROMPT
    (qa, ca_p, kva_bf, qb1, cb1_p, kvb1_bf, qb2, cb2_p, kvb2_bf, qb3, cb3_p, kvb3_bf) = _qkv_call(
        xp, w_qkv, cos_p, sin_p, tm=_TM_PROMPT, prompt=True, tiles_per_seq=tps, name="qkv_prompt")
    oa = _attn_a_call(sink, qa, kva_bf)
    ob = _attn_b_call(qb1, kvb1_bf, qb2, kvb2_bf, qb3, kvb3_bf)
    y_p, ulast, oa_s, ob_s = _ffn_call(xp, oa, ob, ffn_weights, tm=_TM_PROMPT, sample=False, tiles_per_seq=tps,
                                       sample_attn_args=sample_attn_args, name="ffn_prompt")

    def prompt_cache(c, heads):
        return c.reshape(BATCH, 2, heads, HEAD_DIM, c.shape[-1]).transpose(0, 4, 1, 2, 3)[None]

    y_prompt = y_p.reshape(BATCH, SEQ, D_MODEL)
    cache_a_prompt = prompt_cache(ca_p, A_KV_HEADS)
    kvb_p = [prompt_cache(c, B_SLOTS) for c in (cb1_p, cb2_p, cb3_p)]
    state_conv_prompt = ulast.reshape(BATCH, tps, _CARRY, D_FF)[None, :, tps - 1, _CARRY - (CONV_W - 1):]

    fill = jnp.pad(state_conv[l], ((0, 0), (0, DEC_SEQ - (CONV_W - 1)), (0, 0))).reshape(ms, D_FF)
    y_s, u_s = _ffn_call(xs, oa_s, ob_s, ffn_weights, tm=ms, sample=True, tiles_per_seq=1,
                         conv_fill=fill, name="ffn_sample")

    def sample_cache(c, heads):
        return c.reshape(DEC_SEQ, 2, heads, HEAD_DIM, DEC_BATCH).transpose(4, 0, 1, 2, 3)[None]

    y_sample = y_s.reshape(DEC_BATCH, DEC_SEQ, D_MODEL)
    cache_a_sample = sample_cache(ca_s, A_KV_HEADS)
    kvb_s = [sample_cache(c, B_SLOTS) for c in (cb1_s, cb2_s, cb3_s)]
    state_conv_sample = u_s.reshape(DEC_BATCH, DEC_SEQ, D_FF)[None, :, DEC_SEQ - (CONV_W - 1):]

    return (y_prompt, y_sample, cache_a_prompt, cache_a_sample, kvb_p[0], kvb_s[0], kvb_p[1], kvb_s[1],
            kvb_p[2], kvb_s[2], state_conv_prompt, state_conv_sample)
```

```python
import functools

import jax
import jax.numpy as jnp
import numpy as np
from jax import lax
from jax.experimental import pallas as pl
from jax.experimental.pallas import tpu as pltpu

D_MODEL = 1024
BATCH = 8
SEQ = 2048
DEPTH = 1
DEC_BATCH = 128
DEC_SEQ = 4
PAST_LEN = 16384
HEAD_DIM = 64
A_Q_HEADS = 8
A_KV_HEADS = 2
A_GROUP = A_Q_HEADS // A_KV_HEADS
A_WINDOW = 128
B_DIL = ((128, 1), (512, 4), (2048, 16))
B_SLOTS = 4
BLK = 128
ROPE_THETA = 10000.0
D_FF = ((8 * D_MODEL // 3 + 127) // 128) * 128
CONV_W = 3
ALPHA = (2 * DEPTH) ** 0.25
LN_EPS = 1e-5
NEG = -1e30
SCALE = HEAD_DIM ** -0.5
A_Q = A_Q_HEADS * HEAD_DIM
A_KV = A_KV_HEADS * HEAD_DIM
B_G = B_SLOTS * HEAD_DIM
N_QKV = A_Q + 2 * A_KV + 3 * 3 * B_G

LANES = 128
VMEM_LIMIT = 56 * 1024 * 1024

BF16 = jnp.bfloat16
F32 = jnp.float32

_C_QA = 0
_C_KVA = _C_QA + A_Q
_C_QB = (_C_KVA + 2 * A_KV, _C_KVA + 2 * A_KV + 3 * B_G, _C_KVA + 2 * A_KV + 6 * B_G)
_C_KVB = tuple(c + B_G for c in _C_QB)


def _const_spec(shape):
    nd = len(shape)
    return pl.BlockSpec(shape, lambda *_: (0,) * nd, pipeline_mode=pl.Buffered(1))


def _nt_dot(a, b):
    return lax.dot_general(a, b, (((1,), (1,)), ((), ())), preferred_element_type=F32)


def _dot(a, b):
    return jnp.dot(a, b, preferred_element_type=F32)


HALF = HEAD_DIM // 2


def _rope_split(y, cos4, sin4):
    x1, x2 = y[:, :LANES], y[:, LANES:]
    return jnp.concatenate([x1 * cos4 - x2 * sin4, x2 * cos4 + x1 * sin4], axis=1)


def _rope_rot64(y, cos4, sin_signed):
    outs = []
    for j in range(y.shape[1] // LANES):
        yj = y[:, j * LANES:(j + 1) * LANES]
        outs.append(yj * cos4 + pltpu.roll(yj, LANES // 2, 1) * sin_signed)
    return outs[0] if len(outs) == 1 else jnp.concatenate(outs, axis=1)


def _k_feature_rows(n_heads):
    return [(HEAD_DIM * h + HALF * part, n_heads * HALF * part + HALF * h)
            for part in range(2) for h in range(n_heads)]


def _qkv_kernel(x_ref, w_ref, cos_ref, sin_ref, *refs, tm, prompt, tiles_per_seq):
    out_refs, (xb_ref, y_ref, proj0_ref, proj1_ref) = refs[:-4], refs[-4:]
    q_refs, cache_refs, kv_refs = out_refs[0::3], out_refs[1::3], out_refs[2::3]
    groups = ((A_WINDOW, 1),) + B_DIL

    col_q = (_C_QA,) + _C_QB
    col_kv = (_C_KVA,) + _C_KVB
    width_q = (A_Q,) + (B_G,) * len(B_DIL)
    heads_kv = (A_KV_HEADS,) + (B_SLOTS,) * len(B_DIL)

    def project(proj_ref, c0, width):
        proj_ref[:, c0:c0 + width] = _dot(xb_ref[...], w_ref[:, c0:c0 + width])

    def put_cache_rows(cache_ref, lead, kv_t, n_heads):
        wk = n_heads * HEAD_DIM
        for ref_row, our_row in _k_feature_rows(n_heads):
            cache_ref[lead, ref_row:ref_row + HALF, :] = kv_t[our_row:our_row + HALF]
        cache_ref[lead, wk:2 * wk, :] = kv_t[wk:2 * wk]

    def to_planes(val):
        n_col = val.shape[1] // LANES
        for c in range(n_col):
            y_ref[c] = val[:, c * LANES:(c + 1) * LANES]
        return n_col

    def rope(y, n_heads):
        cos = cos_ref[...]
        sin = sin_ref[...]
        if n_heads == B_SLOTS:
            return _rope_split(y, cos, sin)
        lane = lax.broadcasted_iota(jnp.int32, (tm, LANES), 1)
        return _rope_rot64(y, cos, jnp.where(lane < LANES // 2, -sin, sin))

    def roped_q(proj_ref, g):
        return rope(proj_ref[:, col_q[g]:col_q[g] + width_q[g]], heads_kv[g]) * SCALE

    def roped_kv(proj_ref, g):
        wk = heads_kv[g] * HEAD_DIM
        return jnp.concatenate([rope(proj_ref[:, col_kv[g]:col_kv[g] + wk], heads_kv[g]),
                                proj_ref[:, col_kv[g] + wk:col_kv[g] + 2 * wk]], axis=1)

    def deinterleave(val, out_ref, dil):
        n_col = to_planes(val)
        for r in range(dil):
            out_ref[0, r] = jnp.concatenate(
                [y_ref[c, pl.ds(r, tm // dil, stride=dil), :] for c in range(n_col)], axis=1).astype(BF16)

    if not prompt:
        xb_ref[...] = x_ref[...].astype(BF16)
        project(proj0_ref, 0, N_QKV)
        n_seq = tm // DEC_SEQ
        for g in range(len(groups)):
            q_refs[g][...] = roped_q(proj0_ref, g).astype(BF16)
            kv = roped_kv(proj0_ref, g)
            kv_refs[g][...] = kv
            n_col = to_planes(kv)
            for t in range(DEC_SEQ):
                kv_t = jnp.concatenate(
                    [y_ref[c, pl.ds(t, n_seq, stride=DEC_SEQ), :].T for c in range(n_col)], axis=0)
                put_cache_rows(cache_refs[g], t, kv_t, heads_kv[g])
        return

    step = pl.program_id(0)
    tile = jnp.maximum(step - 1, 0)
    last_tile = (tile % tiles_per_seq) == tiles_per_seq - 1

    @pl.when(step == 0)
    def _():
        proj1_ref[...] = jnp.zeros_like(proj1_ref)

    def step_body(mine, other):
        xb_ref[...] = x_ref[...].astype(BF16)
        for g, (win, dil) in enumerate(groups):
            project(mine, col_q[g], width_q[g])
            q = roped_q(other, g)
            if dil == 1:
                q_refs[g][...] = q.astype(BF16)
            else:
                deinterleave(q, q_refs[g], dil)
            project(mine, col_kv[g], 2 * heads_kv[g] * HEAD_DIM)
            kv = roped_kv(other, g)
            if dil == 1:
                kv_refs[g][...] = kv.astype(BF16)
            else:
                deinterleave(kv, kv_refs[g], dil)
            if min(win, SEQ) == SEQ:
                put_cache_rows(cache_refs[g], 0, kv.T, heads_kv[g])

        @pl.when(last_tile)
        def _():
            for g, (win, _) in enumerate(groups):
                keep = min(win, SEQ)
                if keep < SEQ:
                    assert keep <= tm
                    put_cache_rows(cache_refs[g], 0, roped_kv(other, g)[tm - keep:, :].T, heads_kv[g])

    for par, (mine, other) in enumerate(((proj0_ref, proj1_ref), (proj1_ref, proj0_ref))):
        @pl.when(step % 2 == par)
        def _(mine=mine, other=other):
            step_body(mine, other)


def _qkv_call(x2d, w_qkv, cos_t, sin_t, *, tm, prompt, tiles_per_seq, name):
    m = x2d.shape[0]
    n_tiles = m // tm
    if prompt:
        tile = lambda i: jnp.maximum(i - 1, 0)
        x_spec = pl.BlockSpec((tm, D_MODEL), lambda i: (jnp.minimum(i, n_tiles - 1), 0))
    else:
        assert n_tiles == 1
        tile = lambda i: i
        x_spec = pl.BlockSpec((tm, D_MODEL), lambda i: (i, 0))
    row = lambda w: pl.BlockSpec((tm, w), lambda i: (tile(i), 0))
    tab = pl.BlockSpec((tm, LANES), lambda i: (tile(i) % tiles_per_seq, 0))
    in_specs = [x_spec, _const_spec((D_MODEL, N_QKV)), tab, tab]
    sds = jax.ShapeDtypeStruct
    out_shape, out_specs = [], []
    for g, (win, dil) in enumerate(((A_WINDOW, 1),) + B_DIL):
        wq = A_Q if g == 0 else B_G
        wkv = 2 * (A_KV if g == 0 else B_G)
        if not prompt:
            out_shape += [sds((m, wq), BF16), sds((DEC_SEQ, wkv, m // DEC_SEQ), F32), sds((m, wkv), F32)]
            out_specs += [row(wq), pl.BlockSpec((DEC_SEQ, wkv, m // DEC_SEQ), lambda i: (0, 0, 0)),
                          row(wkv)]
            continue
        n_seq = m // SEQ
        keep = min(win, SEQ)
        if keep == SEQ:
            cache_spec = pl.BlockSpec(
                (1, wkv, tm), lambda i: (tile(i) // tiles_per_seq, 0, tile(i) % tiles_per_seq))
        else:
            cache_spec = pl.BlockSpec((1, wkv, keep), lambda i: (tile(i) // tiles_per_seq, 0, 0))
        if dil == 1:
            out_shape += [sds((m, wq), BF16), sds((n_seq, wkv, keep), F32), sds((m, wkv), BF16)]
            out_specs += [row(wq), cache_spec, row(wkv)]
        else:
            dspec = lambda w, dil=dil: pl.BlockSpec(
                (1, dil, tm // dil, w),
                lambda i: (tile(i) // tiles_per_seq, 0, tile(i) % tiles_per_seq, 0))
            out_shape += [sds((n_seq, dil, SEQ // dil, wq), BF16), sds((n_seq, wkv, keep), F32),
                          sds((n_seq, dil, SEQ // dil, wkv), BF16)]
            out_specs += [dspec(wq), cache_spec, dspec(wkv)]
    args = [x2d, w_qkv, cos_t, sin_t]
    proj_bufs = 2 if prompt else 1
    return pl.pallas_call(
        functools.partial(_qkv_kernel, tm=tm, prompt=prompt, tiles_per_seq=tiles_per_seq),
        grid=(n_tiles + 1,) if prompt else (n_tiles,),
        in_specs=in_specs,
        out_specs=out_specs,
        out_shape=out_shape,
        scratch_shapes=[pltpu.VMEM((tm, D_MODEL), BF16),
                        pltpu.VMEM((2 * B_G // LANES, tm, LANES), F32)]
        + [pltpu.VMEM((tm, N_QKV) if k < proj_bufs else (8, LANES), F32) for k in range(2)],
        compiler_params=pltpu.CompilerParams(
            dimension_semantics=("arbitrary",), vmem_limit_bytes=VMEM_LIMIT),
        name=name,
    )(*args)


def _a_query_masks(rows):
    lane = lax.broadcasted_iota(jnp.int32, (rows, LANES), 1)
    return [jnp.where(((lane >> 5) & (A_KV_HEADS - 1)) == j, 1.0, 0.0).astype(BF16)
            for j in range(A_KV_HEADS)]


def _fold_masks():
    row = lax.broadcasted_iota(jnp.int32, (BLK, BLK), 0)
    col = lax.broadcasted_iota(jnp.int32, (BLK, BLK), 1)
    return col > row


def _attn_scratch(heads, blocks_per_trip=1):
    return [pltpu.VMEM((2, blocks_per_trip * heads * BLK, 2 * BLK), F32)]


def _attn_pipeline(n_units, bufs, *, heads, load_q, load_k, load_v, has_prev, sink_of_head, finish,
                   mxu_row_sum=False):
    (sbuf,) = bufs
    upper = _fold_masks()
    upper_bf = jnp.where(upper, 1.0, 0.0).astype(BF16)
    width = BLK if has_prev is None else 2 * BLK
    per_trip = sbuf.shape[1] // (heads * BLK)
    assert n_units % (2 * per_trip) == 0
    n_trips = n_units // per_trip

    def scores(trip, par):
        for j in range(per_trip):
            u = trip * per_trip + j
            sbuf[par, j * heads * BLK:(j + 1) * heads * BLK, 0:width] = _nt_dot(load_q(u), load_k(u))

    def softmax_values(trip, par):
        for j in range(per_trip):
            block_softmax_values(trip * per_trip + j, par, j * heads * BLK)

    def block_softmax_values(u, par, row0):
        prev_ok = None if has_prev is None else has_prev(u)
        ps, ms, ls = [], [], []
        for h in range(heads):
            rows = slice(row0 + h * BLK, row0 + (h + 1) * BLK)
            if has_prev is None:
                sf = jnp.where(upper, NEG, sbuf[par, rows, 0:BLK])
            else:
                sf = jnp.where(upper, jnp.where(prev_ok, sbuf[par, rows, 0:BLK], NEG),
                               sbuf[par, rows, BLK:2 * BLK])
            mx = jnp.max(sf, axis=-1, keepdims=True)
            if sink_of_head is not None:
                mx = jnp.maximum(mx, sink_of_head(h))
            e = jnp.exp(sf - mx)
            if not mxu_row_sum:
                ls.append(jnp.broadcast_to(jnp.sum(e, axis=-1, keepdims=True), (BLK, LANES)))
            e = e.astype(BF16)
            if has_prev is None:
                ps.append(e)
            else:
                p_prev = e * upper_bf
                ps.append(jnp.concatenate([p_prev, e - p_prev], axis=1))
            ms.append(jnp.broadcast_to(mx, (BLK, LANES)))
        finish(u, _dot(jnp.concatenate(ps, axis=0), load_v(u)), jnp.concatenate(ms, axis=0),
               jnp.concatenate(ls, axis=0) if ls else None)

    def trip_pair(t, carry):
        for par in (1, 0):
            i = 2 * t + 2 - par
            scores(i, par)
            softmax_values(i - 1, 1 - par)
        return carry

    def last_trip(i, carry):
        scores(i, 1)
        softmax_values(i - 1, 0)
        return carry

    lax.fori_loop(0, 1, lambda i, carry: (scores(i, 0), carry)[1], 0)
    lax.fori_loop(0, n_trips // 2 - 1, trip_pair, 0)
    lax.fori_loop(n_trips - 1, n_trips, last_trip, 0)
    lax.fori_loop(n_trips - 1, n_trips, lambda i, carry: (softmax_values(i, 1), carry)[1], 0)


def _attn_a_kernel(sink_ref, q_ref, kv_ref, o_ref, *bufs):
    n_blk = q_ref.shape[0] // BLK
    lane = lax.broadcasted_iota(jnp.int32, (BLK, LANES), 1)
    hi = lane >= HEAD_DIM
    mask_bf = _a_query_masks(BLK)

    def rows_of(b):
        return pl.ds(pl.multiple_of(b * BLK, BLK), BLK)

    def prev_cur(b, cols):
        return jnp.concatenate([kv_ref[rows_of(jnp.maximum(b - 1, 0)), cols], kv_ref[rows_of(b), cols]],
                               axis=0)

    def load_q(b):
        qblk = q_ref[rows_of(b), :]
        return jnp.concatenate([qblk[:, g * LANES:(g + 1) * LANES] * mask_bf[j]
                                for j in range(A_KV_HEADS) for g in range(A_GROUP)], axis=0)

    ones = jnp.ones((2 * BLK, LANES), BF16)

    def load_v(b):
        return jnp.concatenate([prev_cur(b, slice(A_KV, 2 * A_KV)), ones], axis=1)

    def finish(b, o, m, _):
        def normalised(h):
            rows = slice(h * BLK, (h + 1) * BLK)
            den = o[rows, LANES:] + jnp.exp(sink_ref[h] - m[rows])
            return o[rows, :LANES] / den

        for g in range(A_GROUP):
            o_ref[rows_of(b), g * LANES:(g + 1) * LANES] = jnp.where(
                hi, normalised(A_GROUP + g), normalised(g)).astype(BF16)

    _attn_pipeline(n_blk, bufs, heads=A_Q_HEADS, load_q=load_q,
                   load_k=lambda b: prev_cur(b, slice(0, A_KV)), load_v=load_v,
                   has_prev=lambda b: b > 0, sink_of_head=lambda h: sink_ref[h], finish=finish,
                   mxu_row_sum=True)


def _attn_a_call(sink, q_a, kva_bf):
    n_seq = q_a.shape[0] // SEQ
    return pl.pallas_call(
        _attn_a_kernel,
        grid=(n_seq,),
        in_specs=[pl.BlockSpec(memory_space=pltpu.SMEM),
                  pl.BlockSpec((SEQ, A_Q), lambda n: (n, 0)),
                  pl.BlockSpec((SEQ, 2 * A_KV), lambda n: (n, 0))],
        out_specs=pl.BlockSpec((SEQ, A_Q), lambda n: (n, 0)),
        out_shape=jax.ShapeDtypeStruct(q_a.shape, BF16),
        scratch_shapes=_attn_scratch(A_Q_HEADS),
        compiler_params=pltpu.CompilerParams(
            dimension_semantics=("arbitrary",), vmem_limit_bytes=VMEM_LIMIT),
        name="attn_a_prompt",
    )(sink, q_a, kva_bf)


def _slot_masks(rows):
    lane = lax.broadcasted_iota(jnp.int32, (rows, B_G), 1)
    masks = [(lane >> 6) == s for s in range(B_SLOTS)]
    masks_bf = [jnp.where(((lane >> 5) & (B_SLOTS - 1)) == s, 1.0, 0.0).astype(BF16) for s in range(B_SLOTS)]
    return masks, masks_bf


def _attn_b_kernel(q1_ref, kv1_ref, q2_ref, kv2_ref, q3_ref, kv3_ref, o_ref,
                   acc2_ref, m2_ref, l2_ref, acc3_ref, m3_ref, l3_ref, *bufs):
    masks, masks_bf = _slot_masks(BLK)
    d2, d3 = B_DIL[1][1], B_DIL[2][1]
    nb2 = SEQ // d2 // BLK
    assert SEQ // d3 == BLK
    n_plane = B_G // LANES

    def put(ref, rows, val):
        for c in range(n_plane):
            ref[c, rows, :] = val[:, c * LANES:(c + 1) * LANES]

    def get(ref, rows):
        return jnp.concatenate([ref[c, rows, :] for c in range(n_plane)], axis=1)

    def stack_q(qblk):
        return jnp.concatenate([qblk * masks_bf[s] for s in range(B_SLOTS)], axis=0)

    def unstack(o, m, l):
        acc = jnp.where(masks[0], o[0:BLK], 0.0)
        rep = lambda x, s: jnp.concatenate([x[s * BLK:(s + 1) * BLK]] * n_plane, axis=1)
        mf, lf = rep(m, 0), rep(l, 0)
        for s in range(1, B_SLOTS):
            acc = jnp.where(masks[s], o[s * BLK:(s + 1) * BLK], acc)
            mf = jnp.where(masks[s], rep(m, s), mf)
            lf = jnp.where(masks[s], rep(l, s), lf)
        return acc, mf, lf

    def rows_of(b):
        return pl.ds(pl.multiple_of(b * BLK, BLK), BLK)

    run = functools.partial(_attn_pipeline, bufs=bufs, heads=B_SLOTS, sink_of_head=None)

    def finish3(r, o, m, l):
        rows = pl.ds(r, BLK, stride=d3)
        for ref, val in zip((acc3_ref, m3_ref, l3_ref), unstack(o, m, l)):
            put(ref, rows, val)

    run(d3, load_q=lambda r: stack_q(q3_ref[r]), load_k=lambda r: kv3_ref[r, :, 0:B_G],
        load_v=lambda r: kv3_ref[r, :, B_G:2 * B_G], has_prev=None, finish=finish3)

    def prev_cur2(u, cols):
        r, b = u // nb2, u % nb2
        return jnp.concatenate([kv2_ref[r, rows_of(jnp.maximum(b - 1, 0)), cols], kv2_ref[r, rows_of(b), cols]],
                               axis=0)

    def finish2(u, o, m, l):
        r, b = u // nb2, u % nb2
        rows = pl.ds(r + b * (BLK * d2), BLK, stride=d2)
        for ref, val in zip((acc2_ref, m2_ref, l2_ref), unstack(o, m, l)):
            put(ref, rows, val)

    run(d2 * nb2, load_q=lambda u: stack_q(q2_ref[u // nb2, rows_of(u % nb2), :]),
        load_k=lambda u: prev_cur2(u, slice(0, B_G)), load_v=lambda u: prev_cur2(u, slice(B_G, 2 * B_G)),
        has_prev=lambda u: (u % nb2) > 0, finish=finish2)

    def prev_cur1(b, cols):
        return jnp.concatenate([kv1_ref[rows_of(jnp.maximum(b - 1, 0)), cols], kv1_ref[rows_of(b), cols]],
                               axis=0)

    def finish1(b, o, m, l):
        acc1, m1, l1 = unstack(o, m, l)
        rows = rows_of(b)
        m2, m3 = get(m2_ref, rows), get(m3_ref, rows)
        mx = jnp.maximum(jnp.maximum(m1, m2), m3)
        w1, w2, w3 = jnp.exp(m1 - mx), jnp.exp(m2 - mx), jnp.exp(m3 - mx)
        num = w1 * acc1 + w2 * get(acc2_ref, rows) + w3 * get(acc3_ref, rows)
        den = w1 * l1 + w2 * get(l2_ref, rows) + w3 * get(l3_ref, rows)
        o_ref[rows, :] = (num / den).astype(BF16)

    run(SEQ // BLK, load_q=lambda b: stack_q(q1_ref[rows_of(b), :]),
        load_k=lambda b: prev_cur1(b, slice(0, B_G)), load_v=lambda b: prev_cur1(b, slice(B_G, 2 * B_G)),
        has_prev=lambda b: b > 0, finish=finish1)


def _attn_b_call(q1, kv1, q2, kv2, q3, kv3):
    n_seq = q1.shape[0] // SEQ
    d2, d3 = B_DIL[1][1], B_DIL[2][1]
    rows = lambda w: pl.BlockSpec((SEQ, w), lambda n: (n, 0))
    dsp = lambda d, w: pl.BlockSpec((None, d, SEQ // d, w), lambda n: (n, 0, 0, 0))
    return pl.pallas_call(
        _attn_b_kernel,
        grid=(n_seq,),
        in_specs=[rows(B_G), rows(2 * B_G), dsp(d2, B_G), dsp(d2, 2 * B_G), dsp(d3, B_G),
                  dsp(d3, 2 * B_G)],
        out_specs=rows(B_G),
        out_shape=jax.ShapeDtypeStruct((q1.shape[0], B_G), BF16),
        scratch_shapes=[pltpu.VMEM((B_G // LANES, SEQ, LANES), F32) for _ in range(6)]
        + _attn_scratch(B_SLOTS, blocks_per_trip=2),
        compiler_params=pltpu.CompilerParams(
            dimension_semantics=("arbitrary",), vmem_limit_bytes=VMEM_LIMIT),
        name="attn_b_prompt",
    )(q1, kv1, q2, kv2, q3, kv3)


def _sample_attn_block(sink_ref, qa_ref, kna_ref, ca_ref, q1_ref, kn1_ref, c1_ref,
                       q2_ref, kn2_ref, c2_ref, q3_ref, kn3_ref, c3_ref, *, tt, seqs, emit):
    pad = BLK - tt
    tt_shift, seq_shift = tt.bit_length() - 1, DEC_SEQ.bit_length() - 1
    assert tt == 1 << tt_shift and DEC_SEQ == 1 << seq_shift

    def pad_rows(x):
        return jnp.concatenate([x.astype(BF16), jnp.zeros((pad, x.shape[1]), BF16)], axis=0)

    def window_group(qm, kn_ref, c_ref, dil, sinkv, finish):
        rows, kd = qm.shape
        lc = c_ref.shape[2]
        kn = kn_ref[...]
        knpad = pad_rows(kn[:, :kd])
        vnpad = pad_rows(kn[:, kd:])
        s_new = _nt_dot(qm, knpad)
        def masks(width):
            col = lax.broadcasted_iota(jnp.int32, (rows, width), 1)
            t = lax.broadcasted_iota(jnp.int32, (rows, width), 0) & (DEC_SEQ - 1)
            same_res = (col & (dil - 1)) == (t & (dil - 1))
            return (col > t) & same_res, (col <= t) & same_res

        valid, _ = masks(lc)
        _, is_new = masks(BLK)
        k_rows = sorted(_k_feature_rows(kd // HEAD_DIM), key=lambda rows: rows[1])
        scores = []
        for pos, _ in seqs:
            k_t = jnp.concatenate([c_ref[pos, ref_row:ref_row + HALF, :] for ref_row, _ in k_rows],
                                  axis=0).astype(BF16)
            scores.append(_dot(qm, k_t))
        yield
        probs = []
        for score, (_, n) in zip(scores, seqs):
            s_c = jnp.where(valid, score, NEG)
            s_n = pltpu.roll(s_new, BLK - DEC_SEQ * n, 1) if n else s_new
            first = jnp.where(is_new, s_n, s_c[:, :BLK])
            s = first if lc == BLK else jnp.concatenate([first, s_c[:, BLK:]], axis=1)
            mx = jnp.max(s, axis=-1, keepdims=True)
            if sinkv is not None:
                mx = jnp.maximum(mx, sinkv)
            e = jnp.exp(s - mx)
            l = jnp.sum(e, axis=-1, keepdims=True)
            if sinkv is not None:
                l = l + jnp.exp(sinkv - mx)
            p_new = jnp.where(is_new, e[:, :BLK], 0.0)
            if n:
                p_new = pltpu.roll(p_new, DEC_SEQ * n, 1)
            probs.append((jnp.where(valid, e, 0.0).astype(BF16), p_new.astype(BF16), mx, l))
        yield
        for (p_c, p_new, mx, l), (pos, n) in zip(probs, seqs):
            v_t = c_ref[pos, kd:2 * kd, :].astype(BF16)
            finish(n, _nt_dot(p_c, v_t) + _dot(p_new, vnpad), mx, l)

    rows_a = A_Q_HEADS * tt
    lane_t = lax.broadcasted_iota(jnp.int32, (tt, LANES), 1)
    hi_t = lane_t >= HEAD_DIM
    mask_bf = _a_query_masks(tt)
    qa = qa_ref[...]
    qm = jnp.concatenate([qa[:, g * LANES:(g + 1) * LANES] * mask_bf[j]
                          for j in range(A_KV_HEADS) for g in range(A_GROUP)], axis=0)
    head = lax.broadcasted_iota(jnp.int32, (rows_a, 1), 0) >> tt_shift
    sinkv = jnp.zeros((rows_a, 1), F32)
    for h in range(A_Q_HEADS):
        sinkv = jnp.where(head == h, sink_ref[h], sinkv)
    seq_t = lax.broadcasted_iota(jnp.int32, (tt, LANES), 0) >> seq_shift
    out_a = [jnp.zeros((tt, LANES), F32) for _ in range(A_GROUP)]

    def finish_a(n, o, mx, l):
        o = o / l
        for g in range(A_GROUP):
            og = jnp.where(hi_t, o[(A_GROUP + g) * tt:(A_GROUP + g + 1) * tt], o[g * tt:(g + 1) * tt])
            out_a[g] = jnp.where(seq_t == n, og, out_a[g])

    stages = [window_group(qm, kna_ref, ca_ref, 1, sinkv, finish_a)]

    masks_t, masks_bf_t = _slot_masks(tt)
    tok_t = lax.broadcasted_iota(jnp.int32, (tt, B_G), 0)

    def stack_q(q):
        return jnp.concatenate([q * masks_bf_t[s] for s in range(B_SLOTS)], axis=0)

    def unstack(o, mx, l):
        acc = jnp.where(masks_t[0], o[0:tt], 0.0)
        mf = jnp.broadcast_to(mx[0:tt], (tt, B_G))
        lf = jnp.broadcast_to(l[0:tt], (tt, B_G))
        for s in range(1, B_SLOTS):
            acc = jnp.where(masks_t[s], o[s * tt:(s + 1) * tt], acc)
            mf = jnp.where(masks_t[s], mx[s * tt:(s + 1) * tt], mf)
            lf = jnp.where(masks_t[s], l[s * tt:(s + 1) * tt], lf)
        return acc, mf, lf

    zero, one = jnp.zeros((tt, B_G), F32), jnp.ones((tt, B_G), F32)
    stats = []
    for (win, dil), q_ref, kn_ref, c_ref in zip(B_DIL, (q1_ref, q2_ref, q3_ref),
                                                (kn1_ref, kn2_ref, kn3_ref), (c1_ref, c2_ref, c3_ref)):
        assert c_ref.shape[2] == win
        group = [zero, zero, one]

        def finish_b(n, o, mx, l, group=group):
            sel = (tok_t >> seq_shift) == n
            for k, new in enumerate(unstack(o, mx, l)):
                group[k] = jnp.where(sel, new, group[k])

        stages.append(window_group(stack_q(q_ref[...]), kn_ref, c_ref, dil, None, finish_b))
        stats.append(group)

    for phase in range(3):
        for stage in stages:
            next(stage, None)
        if phase < 2:
            yield
    (acc1, m1, l1), (acc2, m2, l2), (acc3, m3, l3) = stats
    mx = jnp.maximum(jnp.maximum(m1, m2), m3)
    w1, w2, w3 = jnp.exp(m1 - mx), jnp.exp(m2 - mx), jnp.exp(m3 - mx)
    emit(out_a, (w1 * acc1 + w2 * acc2 + w3 * acc3) / (w1 * l1 + w2 * l2 + w3 * l3))


def _sample_attn_rider(in_refs, oa_ref, ob_ref, k, n_seqs):
    seq_shift = DEC_SEQ.bit_length() - 1

    def merge(ref, cols, new):
        if k:
            mine = (lax.broadcasted_iota(jnp.int32, new.shape, 0) >> seq_shift) == k
            new = jnp.where(mine, new, ref[:, cols].astype(F32))
        ref[:, cols] = new.astype(BF16)

    def emit(out_a, out_b):
        for c in range(A_GROUP):
            merge(oa_ref, slice(c * LANES, (c + 1) * LANES), out_a[c])
        merge(ob_ref, slice(0, B_G), out_b)

    yield from _sample_attn_block(*in_refs, tt=n_seqs * DEC_SEQ, seqs=[(0, k)], emit=emit)


def _layernorm(x, g, b):
    mu = jnp.mean(x, axis=-1, keepdims=True)
    xc = x - mu
    var = jnp.mean(xc * xc, axis=-1, keepdims=True)
    return xc * lax.rsqrt(var + LN_EPS) * g + b


def _gelu_exact(x):
    return 0.5 * x * (1.0 + lax.erf(x * (0.5 ** 0.5)))


_FF_CHUNK = 256
_CARRY = 8


def _ffn_kernel(*refs, tm, sample, tiles_per_seq):
    if sample:
        (x_ref, oa_ref, ob_ref, fill_ref, wg_ref, wa_ref, wb_ref, wo_ref, g1_ref, b1_ref,
         wup_ref, cw_ref, cb_ref, wdn_ref, g2_ref, b2_ref, y_ref, u_ref,
         xb_ref, m_ref, gg_ref, ext_ref, h0_ref, hb0_ref) = refs
    else:
        x_ref, oa_ref, ob_ref = refs[:3]
        rider_in, refs = refs[3:3 + _N_SAMPLE_ATTN_IN], refs[3 + _N_SAMPLE_ATTN_IN:]
        (wg_ref, wa_ref, wb_ref, wo_ref, g1_ref, b1_ref,
         wup_ref, cw_ref, cb_ref, wdn_ref, g2_ref, b2_ref, y_ref, ulast_ref, oa_s_ref, ob_s_ref,
         xb_ref, m_ref, gg_ref, ext_ref, h0_ref, hb0_ref, carry_ref) = refs
    half = D_MODEL // 2

    def merge_pieces(h_ref, hb_ref):
        def gate_half(c):
            if c == 0:
                xb_ref[...] = x_ref[...].astype(BF16)
            cs = slice(c * half, (c + 1) * half)
            ga = _dot(xb_ref[...], wg_ref[:, c * half:(c + 1) * half])
            gb = _dot(xb_ref[...], wg_ref[:, D_MODEL + c * half:D_MODEL + (c + 1) * half])
            ta = _dot(oa_ref[...], wa_ref[:, cs])
            tb = _dot(ob_ref[...], wb_ref[:, cs])
            m_ref[:, cs] = (jax.nn.sigmoid(ga) * ta + jax.nn.sigmoid(gb) * tb).astype(BF16)

        def out_proj():
            mix = _dot(m_ref[...], wo_ref[...])
            h = _layernorm(ALPHA * x_ref[...] + mix, g1_ref[...], b1_ref[...])
            h_ref[...] = h
            hb_ref[...] = h.astype(BF16)

        return [functools.partial(gate_half, 0), functools.partial(gate_half, 1), out_proj]

    def ffn_pieces(h_ref, hb_ref):
        def chunk(c):
            cs = slice(c * _FF_CHUNK, (c + 1) * _FF_CHUNK)
            u = _dot(hb_ref[...], wup_ref[:, c * _FF_CHUNK:(c + 1) * _FF_CHUNK])
            v = _dot(hb_ref[...], wup_ref[:, D_FF + c * _FF_CHUNK:D_FF + (c + 1) * _FF_CHUNK])
            if sample:
                ext_ref[0:_CARRY, :] = jnp.zeros((_CARRY, _FF_CHUNK), F32)
            else:
                ext_ref[0:_CARRY, :] = carry_ref[0:_CARRY, cs]
            ext_ref[_CARRY:_CARRY + tm, :] = u
            u1 = ext_ref[_CARRY - 1:_CARRY - 1 + tm, :]
            u2 = ext_ref[_CARRY - 2:_CARRY - 2 + tm, :]
            if sample:
                t = lax.broadcasted_iota(jnp.int32, (tm, _FF_CHUNK), 0) & (DEC_SEQ - 1)
                fill = fill_ref[:, cs]
                u1 = jnp.where(t >= 1, u1, pltpu.roll(fill, tm - 1, 0))
                u2 = jnp.where(t >= 2, u2, fill)
                u_ref[:, cs] = u
            else:
                tail = u[tm - _CARRY:tm, :]
                carry_ref[0:_CARRY, cs] = tail
                ulast_ref[0, :, cs] = tail
            a = cb_ref[:, cs] + cw_ref[0:1, cs] * u2 + cw_ref[1:2, cs] * u1 + cw_ref[2:3, cs] * u
            gg_ref[:, cs] = (_gelu_exact(a) * v).astype(BF16)

        def down():
            f = _dot(gg_ref[...], wdn_ref[...])
            y_ref[...] = _layernorm(ALPHA * h_ref[...] + f, g2_ref[...], b2_ref[...])

        return [functools.partial(chunk, c) for c in range(D_FF // _FF_CHUNK)] + [down]

    pieces = merge_pieces(h0_ref, hb0_ref) + ffn_pieces(h0_ref, hb0_ref)
    if sample:
        for piece in pieces:
            piece()
        return

    step, sub = pl.program_id(0), pl.program_id(1)

    @pl.when((step % tiles_per_seq == 0) & (sub == 0))
    def _():
        carry_ref[...] = jnp.zeros_like(carry_ref)

    n_chunk = D_FF // _FF_CHUNK
    phases = (pieces[:3], pieces[3:3 + n_chunk // 2], pieces[3 + n_chunk // 2:3 + n_chunk], pieces[3 + n_chunk:])
    assert len(phases) == _FFN_PHASES
    for g, phase in enumerate(phases):
        @pl.when(sub == g)
        def _(g=g, phase=phase):
            rider = _sample_attn_rider(rider_in, oa_s_ref, ob_s_ref, g, _FFN_PHASES)
            for piece in phase:
                next(rider, None)
                piece()
            for _ in rider:
                pass


_FFN_PHASES = 4
_N_SAMPLE_ATTN_IN = 13


def _ffn_call(x2d, oa, ob, weights, *, tm, sample, tiles_per_seq, conv_fill=None, sample_attn_args=(), name):
    m = x2d.shape[0]
    n_tiles = m // tm
    sds = jax.ShapeDtypeStruct
    row_in = row_out = lambda w: pl.BlockSpec((tm, w), lambda i, *_: (i, 0))
    in_specs = [row_in(D_MODEL), row_in(A_Q), row_in(B_G)]
    args = [x2d, oa, ob]
    if sample:
        in_specs += [row_in(D_FF)]
        args += [conv_fill]
    else:
        assert len(sample_attn_args) == _N_SAMPLE_ATTN_IN
        tt = _FFN_PHASES * DEC_SEQ
        n_tok = sample_attn_args[1].shape[0]
        assert n_tok == n_tiles * tt
        in_specs.append(pl.BlockSpec(memory_space=pltpu.SMEM))
        for k, a in enumerate(sample_attn_args[1:]):
            if k % 3 == 2:
                in_specs.append(pl.BlockSpec((1,) + a.shape[1:], lambda i, g: (_FFN_PHASES * i + g, 0, 0)))
            else:
                in_specs.append(pl.BlockSpec((tt, a.shape[1]), lambda i, g: (i, 0)))
        args += list(sample_attn_args)
    in_specs += [_const_spec(w.shape) for w in weights]
    args += list(weights)
    h_bufs = [pltpu.VMEM((tm, D_MODEL), F32), pltpu.VMEM((tm, D_MODEL), BF16)]
    scratch = [pltpu.VMEM((tm, D_MODEL), BF16), pltpu.VMEM((tm, D_MODEL), BF16),
               pltpu.VMEM((tm, D_FF), BF16), pltpu.VMEM((tm + 2 * _CARRY, _FF_CHUNK), F32)] + h_bufs
    if sample:
        out_shape = [sds((m, D_MODEL), F32), sds((m, D_FF), F32)]
        out_specs = [row_out(D_MODEL), row_out(D_FF)]
    else:
        out_shape = [sds((m, D_MODEL), F32), sds((n_tiles // tiles_per_seq, _CARRY, D_FF), F32),
                     sds((n_tok, A_Q), BF16), sds((n_tok, B_G), BF16)]
        out_specs = [row_out(D_MODEL), pl.BlockSpec((1, _CARRY, D_FF), lambda i, g: (i // tiles_per_seq, 0, 0)),
                     pl.BlockSpec((tt, A_Q), lambda i, g: (i, 0)), pl.BlockSpec((tt, B_G), lambda i, g: (i, 0))]
        scratch += [pltpu.VMEM((2 * _CARRY, D_FF), F32)]
    assert all(int(np.prod(s.shape)) * jnp.dtype(s.dtype).itemsize % (16 << 10) == 0 for s in scratch)
    grid = (n_tiles,) if sample else (n_tiles, _FFN_PHASES)
    return pl.pallas_call(
        functools.partial(_ffn_kernel, tm=tm, sample=sample, tiles_per_seq=tiles_per_seq),
        grid=grid,
        in_specs=in_specs,
        out_specs=out_specs,
        out_shape=out_shape,
        scratch_shapes=scratch,
        compiler_params=pltpu.CompilerParams(
            dimension_semantics=("arbitrary",) * len(grid), vmem_limit_bytes=VMEM_LIMIT),
        name=name,
    )(*args)


def _rope_tables(pos):
    half = HEAD_DIM // 2
    inv = ROPE_THETA ** (-jnp.arange(half, dtype=F32) / half)
    ang = pos.astype(F32)[:, None] * inv[None, :]
    cos, sin = jnp.cos(ang), jnp.sin(ang)
    reps = LANES // HALF
    return jnp.tile(cos, (1, reps)), jnp.tile(sin, (1, reps))


def _rotary_order(w, heads):
    x1 = [w[:, h * HEAD_DIM:h * HEAD_DIM + HALF] for h in heads]
    x2 = [w[:, h * HEAD_DIM + HALF:(h + 1) * HEAD_DIM] for h in heads]
    return x1 + x2


def _qkv_source_columns():
    w_in = np.arange(N_QKV, dtype=np.int32)[None, :]
    qa = w_in[:, 0:A_Q]
    ka = w_in[:, A_Q:A_Q + A_KV]
    va = w_in[:, A_Q + A_KV:A_Q + 2 * A_KV]
    b0 = A_Q + 2 * A_KV
    qb = w_in[:, b0:b0 + 3 * B_G]
    kb = w_in[:, b0 + 3 * B_G:b0 + 6 * B_G]
    vb = w_in[:, b0 + 6 * B_G:b0 + 9 * B_G]
    cols = []
    for c in range(A_GROUP):
        cols += _rotary_order(qa, (c, A_GROUP + c))
    cols += _rotary_order(ka, range(A_KV_HEADS)) + [va]
    for g in range(len(B_DIL)):
        gs = slice(g * B_G, (g + 1) * B_G)
        cols += _rotary_order(qb[:, gs], range(B_SLOTS)) + _rotary_order(kb[:, gs], range(B_SLOTS)) + [vb[:, gs]]
    return np.concatenate(cols, axis=1)[0]


def _in_weight_kernel(src_ref, w_ref, qkv_ref, gate_ref):
    src = _qkv_source_columns()
    rows = w_ref.shape[0]
    lane = lax.broadcasted_iota(jnp.int32, (LANES, LANES), 0)
    for j in range(N_QKV // LANES):
        want = src[j * LANES:(j + 1) * LANES]
        if np.array_equal(want, want[0] + np.arange(LANES)) and want[0] % LANES == 0:
            qkv_ref[:, j * LANES:(j + 1) * LANES] = w_ref[:, want[0]:want[0] + LANES].astype(BF16)
            continue
        acc = jnp.zeros((rows, LANES), F32)
        for b in sorted(set(int(c) // LANES for c in want)):
            pick = (lane + b * LANES) == src_ref[:, j * LANES:(j + 1) * LANES]
            acc = acc + _dot(w_ref[:, b * LANES:(b + 1) * LANES].astype(BF16), jnp.where(pick, 1.0, 0.0).astype(BF16))
        qkv_ref[:, j * LANES:(j + 1) * LANES] = acc.astype(BF16)
    gate_ref[...] = w_ref[:, N_QKV:].astype(BF16)


def _prep_in_weights(w_in):
    rows = 256
    n_proj = w_in.shape[1]
    src = jnp.asarray(_qkv_source_columns())[None, :]
    return pl.pallas_call(
        _in_weight_kernel,
        grid=(D_MODEL // rows,),
        in_specs=[pl.BlockSpec((1, N_QKV), lambda i: (0, 0)), pl.BlockSpec((rows, n_proj), lambda i: (i, 0))],
        out_specs=[pl.BlockSpec((rows, N_QKV), lambda i: (i, 0)),
                   pl.BlockSpec((rows, n_proj - N_QKV), lambda i: (i, 0))],
        out_shape=[jax.ShapeDtypeStruct((D_MODEL, N_QKV), BF16),
                   jax.ShapeDtypeStruct((D_MODEL, n_proj - N_QKV), BF16)],
        compiler_params=pltpu.CompilerParams(dimension_semantics=("arbitrary",), vmem_limit_bytes=VMEM_LIMIT),
        name="in_weights",
    )(src, w_in)


def _prep_br_a(w_br_a):
    rows = []
    for c in range(A_GROUP):
        rows += [w_br_a[c * HEAD_DIM:(c + 1) * HEAD_DIM],
                 w_br_a[(A_GROUP + c) * HEAD_DIM:(A_GROUP + c + 1) * HEAD_DIM]]
    return jnp.concatenate(rows, axis=0).astype(BF16)


_TM_PROMPT = 512


def kernel(x_prompt, x_sample, cache_a, cache_b1, cache_b2, cache_b3, state_conv, w_in, sink_a, w_br_a, w_br_b, w_o, ln1_g, ln1_b, w_up, conv_w, conv_b, w_down, ln2_g, ln2_b):
    assert DEPTH == 1
    l = 0
    w_qkv, w_gate = _prep_in_weights(w_in[l])
    ffn_weights = (w_gate, _prep_br_a(w_br_a[l]), w_br_b[l].astype(BF16), w_o[l].astype(BF16),
                   ln1_g[l][None], ln1_b[l][None], w_up[l].astype(BF16), conv_w[l], conv_b[l][None],
                   w_down[l].astype(BF16), ln2_g[l][None], ln2_b[l][None])
    sink = sink_a[l].astype(F32)

    ms = DEC_BATCH * DEC_SEQ
    xs = x_sample.reshape(ms, D_MODEL)
    cos_s, sin_s = _rope_tables(PAST_LEN + (jnp.arange(ms, dtype=jnp.int32) % DEC_SEQ))
    (qa_s, ca_s, kva_s, qb1_s, cb1_s, kvb1_s, qb2_s, cb2_s, kvb2_s, qb3_s, cb3_s, kvb3_s) = _qkv_call(
        xs, w_qkv, cos_s, sin_s, tm=ms, prompt=False, tiles_per_seq=1, name="qkv_sample")

    def window_buffer(c):
        return c.transpose(0, 2, 3, 4, 1).reshape(c.shape[0], -1, c.shape[1])

    ca = window_buffer(cache_a[l])
    cb = [window_buffer(c[l]) for c in (cache_b1, cache_b2, cache_b3)]
    sample_attn_args = (sink, qa_s, kva_s, ca, qb1_s, kvb1_s, cb[0], qb2_s, kvb2_s, cb[1], qb3_s, kvb3_s, cb[2])

    mp = BATCH * SEQ
    xp = x_prompt.reshape(mp, D_MODEL)
    cos_p, sin_p = _rope_tables(jnp.arange(SEQ, dtype=jnp.int32))
    tps = SEQ // _TM_PROMPT
    (qa, ca_p, kva_bf, qb1, cb1_p, kvb1_bf, qb2, cb2_p, kvb2_bf, qb3, cb3_p, kvb3_bf) = _qkv_call(
        xp, w_qkv, cos_p, sin_p, tm=_TM_PROMPT, prompt=True, tiles_per_seq=tps, name="qkv_prompt")
    oa = _attn_a_call(sink, qa, kva_bf)
    ob = _attn_b_call(qb1, kvb1_bf, qb2, kvb2_bf, qb3, kvb3_bf)
    y_p, ulast, oa_s, ob_s = _ffn_call(xp, oa, ob, ffn_weights, tm=_TM_PROMPT, sample=False, tiles_per_seq=tps,
                                       sample_attn_args=sample_attn_args, name="ffn_prompt")

    def prompt_cache(c, heads):
        return c.reshape(BATCH, 2, heads, HEAD_DIM, c.shape[-1]).transpose(0, 4, 1, 2, 3)[None]

    y_prompt = y_p.reshape(BATCH, SEQ, D_MODEL)
    cache_a_prompt = prompt_cache(ca_p, A_KV_HEADS)
    kvb_p = [prompt_cache(c, B_SLOTS) for c in (cb1_p, cb2_p, cb3_p)]
    state_conv_prompt = ulast[None, :, _CARRY - (CONV_W - 1):]

    fill = jnp.pad(state_conv[l], ((0, 0), (0, DEC_SEQ - (CONV_W - 1)), (0, 0))).reshape(ms, D_FF)
    y_s, u_s = _ffn_call(xs, oa_s, ob_s, ffn_weights, tm=ms, sample=True, tiles_per_seq=1,
                         conv_fill=fill, name="ffn_sample")

    def sample_cache(c, heads):
        return c.reshape(DEC_SEQ, 2, heads, HEAD_DIM, DEC_BATCH).transpose(4, 0, 1, 2, 3)[None]

    y_sample = y_s.reshape(DEC_BATCH, DEC_SEQ, D_MODEL)
    cache_a_sample = sample_cache(ca_s, A_KV_HEADS)
    kvb_s = [sample_cache(c, B_SLOTS) for c in (cb1_s, cb2_s, cb3_s)]
    state_conv_sample = u_s.reshape(DEC_BATCH, DEC_SEQ, D_FF)[None, :, DEC_SEQ - (CONV_W - 1):]

    return (y_prompt, y_sample, cache_a_prompt, cache_a_sample, kvb_p[0], kvb_s[0], kvb_p[1], kvb_s[1],
            kvb_p[2], kvb_s[2], state_conv_prompt, state_conv_sample)
```
